```python
import jax, jax.numpy as jnp
from jax import lax
import numpy as np

D_MODEL = 1024
BATCH = 4
SEQ = 4096
DEPTH = 1

CHUNK = 64
LEFT_CHUNKS = 8
BAND = (LEFT_CHUNKS + 1) * CHUNK
ATT_HEADS = 8
ATT_HEAD_DIM = 64
ATT_WIDTH = ATT_HEADS * ATT_HEAD_DIM
MAX_REL = 128
LRU_WIDTH = D_MODEL
LRU_BLOCKS = 16
LRU_BLOCK = LRU_WIDTH // LRU_BLOCKS
CONV_WIDTH = 4
LRU_C = 8.0
D_FF = 2816
N_SUB = 3
EPS = 1e-6
PROJ_SIZES = (ATT_WIDTH, ATT_WIDTH, ATT_WIDTH, LRU_WIDTH, LRU_WIDTH, D_MODEL, D_MODEL)
PROJ_WIDTH = sum(PROJ_SIZES)

kernel_name = "hybrid_chunked_attn_rglru_macaron_block"


def rmsnorm(x, g):
    xf = x.astype(jnp.float32)
    y = xf * lax.rsqrt(jnp.mean(xf * xf, axis=-1, keepdims=True) + EPS)
    return (y * g.astype(jnp.float32)).astype(x.dtype)


def modulate(h, shift, scale):
    return h * (1 + scale[:, None, :]) + shift[:, None, :]


def swiglu(h, w_gu, w_down):
    g, u = jnp.split(h @ w_gu, 2, axis=-1)
    return (jax.nn.silu(g) * u) @ w_down


def chunked_attention(q, k, v, rel_bias):
    b, s, _ = q.shape
    nc = s // CHUNK
    q = q.reshape(b, nc, CHUNK, ATT_HEADS, ATT_HEAD_DIM)
    k = k.reshape(b, nc, CHUNK, ATT_HEADS, ATT_HEAD_DIM)
    v = v.reshape(b, nc, CHUNK, ATT_HEADS, ATT_HEAD_DIM)
    pad = ((0, 0), (LEFT_CHUNKS, 0), (0, 0), (0, 0), (0, 0))
    kp, vp = jnp.pad(k, pad), jnp.pad(v, pad)
    kb = jnp.concatenate([kp[:, j:j + nc] for j in range(LEFT_CHUNKS + 1)], axis=2)
    vb = jnp.concatenate([vp[:, j:j + nc] for j in range(LEFT_CHUNKS + 1)], axis=2)
    scores = jnp.einsum('bnqhd,bnkhd->bhnqk', q, kb).astype(jnp.float32) * (ATT_HEAD_DIM ** -0.5)
    qi = jnp.arange(CHUNK)[:, None]
    kj = jnp.arange(BAND)[None, :]
    rel = jnp.clip(qi - kj + LEFT_CHUNKS * CHUNK, -MAX_REL, MAX_REL) + MAX_REL
    bias = rel_bias.astype(jnp.float32)[:, rel]
    scores = scores + bias[None, :, None, :, :]
    valid = (jnp.arange(nc)[:, None] - LEFT_CHUNKS + jnp.arange(BAND)[None, :] // CHUNK) >= 0
    scores = jnp.where(valid[None, None, :, None, :], scores, jnp.finfo(jnp.float32).min)
    p = jax.nn.softmax(scores, axis=-1).astype(v.dtype)
    out = jnp.einsum('bhnqk,bnkhd->bnqhd', p, vb)
    return out.reshape(b, s, ATT_WIDTH)


def causal_depthwise_conv(x, w, bias):
    rhs = w[:, None, :]
    y = lax.conv_general_dilated(x, rhs, window_strides=(1,), padding=[(CONV_WIDTH - 1, 0)],
                                 dimension_numbers=('NWC', 'WIO', 'NWC'),
                                 feature_group_count=x.shape[-1])
    return y + bias


def block_diag_linear(x, w, bias):
    b, s, _ = x.shape
    xb = x.reshape(b, s, LRU_BLOCKS, LRU_BLOCK)
    return jnp.einsum('bsnk,nkj->bsnj', xb, w).reshape(b, s, LRU_WIDTH) + bias


def rg_lru(x, w_a, b_a, w_x, b_x, lam):
    r = jax.nn.sigmoid(block_diag_linear(x, w_a, b_a).astype(jnp.float32))
    i = jax.nn.sigmoid(block_diag_linear(x, w_x, b_x).astype(jnp.float32))
    log_a = -LRU_C * r * jax.nn.softplus(-lam.astype(jnp.float32))
    a = jnp.exp(log_a)
    mult = jnp.sqrt(-jnp.expm1(2.0 * log_a))
    u = mult * (i * x.astype(jnp.float32))

    def combine(left, right):
        a1, b1 = left
        a2, b2 = right
        return a1 * a2, a2 * b1 + b2

    _, h = lax.associative_scan(combine, (a, u), axis=1)
    return h.astype(x.dtype)


def mixer(h, w_in, rel_bias, conv_w, conv_b, lru_wa, lru_ba, lru_wx, lru_bx, lru_lambda,
          w_att_o, w_rec_o, w_out):
    proj = h @ w_in
    idx = [int(v) for v in np.cumsum(PROJ_SIZES)[:-1]]
    q, k, v, xr, yr, g_att, g_rec = jnp.split(proj, idx, axis=-1)
    att = chunked_attention(q, k, v, rel_bias) @ w_att_o
    xr = causal_depthwise_conv(xr, conv_w, conv_b)
    rec = (rg_lru(xr, lru_wa, lru_ba, lru_wx, lru_bx, lru_lambda) * jax.nn.gelu(yr)) @ w_rec_o
    merged = jax.nn.sigmoid(g_att) * att + jax.nn.sigmoid(g_rec) * rec
    return merged @ w_out


def sandwich(x, fn, g_pre, g_post, shift, scale, gate, res_w):
    h = modulate(rmsnorm(x, g_pre), shift, scale)
    y = rmsnorm(fn(h), g_post)
    return x + res_w * gate[:, None, :] * y


def setup_inputs(seed: int = 0) -> dict:
    key = jax.random.key(seed)
    ks = jax.random.split(key, 24)
    L, D, F, W = DEPTH, D_MODEL, D_FF, LRU_WIDTH
    nrm = lambda k, shape, fan_in: jax.random.normal(k, shape, jnp.float32) * (fan_in ** -0.5)
    u = jax.random.uniform(ks[20], (L, W), jnp.float32, 0.9, 0.999)
    p = u ** (1.0 / LRU_C)
    lam = jnp.log(p) - jnp.log1p(-p)
    return {
        "x": jax.random.normal(ks[0], (BATCH, SEQ, D), jnp.float32),
        "c": jax.random.normal(ks[1], (BATCH, D), jnp.float32),
        "w_ada": nrm(ks[2], (L, D, N_SUB * 3 * D), D) * 0.5,
        "b_ada": 0.02 * jax.random.normal(ks[3], (L, N_SUB * 3 * D), jnp.float32),
        "norm_pre": 1.0 + 0.05 * jax.random.normal(ks[4], (L, N_SUB, D), jnp.float32),
        "norm_post": 1.0 + 0.05 * jax.random.normal(ks[5], (L, N_SUB, D), jnp.float32),
        "ffn1_w_gu": nrm(ks[6], (L, D, 2 * F), D),
        "ffn1_w_down": nrm(ks[7], (L, F, D), F),
        "w_in": nrm(ks[8], (L, D, PROJ_WIDTH), D),
        "rel_bias": 0.5 * jax.random.normal(ks[9], (L, ATT_HEADS, 2 * MAX_REL + 1), jnp.float32),
        "conv_w": nrm(ks[10], (L, CONV_WIDTH, W), CONV_WIDTH),
        "conv_b": 0.02 * jax.random.normal(ks[11], (L, W), jnp.float32),
        "lru_wa": nrm(ks[12], (L, LRU_BLOCKS, LRU_BLOCK, LRU_BLOCK), LRU_BLOCK),
        "lru_ba": 0.02 * jax.random.normal(ks[13], (L, W), jnp.float32),
        "lru_wx": nrm(ks[14], (L, LRU_BLOCKS, LRU_BLOCK, LRU_BLOCK), LRU_BLOCK),
        "lru_bx": 0.02 * jax.random.normal(ks[15], (L, W), jnp.float32),
        "lru_lambda": lam,
        "w_att_o": nrm(ks[16], (L, ATT_WIDTH, D), ATT_WIDTH),
        "w_rec_o": nrm(ks[17], (L, W, D), W),
        "w_out": nrm(ks[18], (L, D, D), D),
        "ffn2_w_gu": nrm(ks[19], (L, D, 2 * F), D),
        "ffn2_w_down": nrm(ks[21], (L, F, D), F),
    }


def reference(x, c, w_ada, b_ada, norm_pre, norm_post, ffn1_w_gu, ffn1_w_down, w_in, rel_bias,
              conv_w, conv_b, lru_wa, lru_ba, lru_wx, lru_bx, lru_lambda, w_att_o, w_rec_o,
              w_out, ffn2_w_gu, ffn2_w_down):
    b = x.shape[0]
    c_act = jax.nn.silu(c)
    for l in range(DEPTH):
        mod = (c_act @ w_ada[l] + b_ada[l]).reshape(b, N_SUB, 3, D_MODEL)
        ffn1 = lambda h: swiglu(h, ffn1_w_gu[l], ffn1_w_down[l])
        mix = lambda h: mixer(h, w_in[l], rel_bias[l], conv_w[l], conv_b[l], lru_wa[l], lru_ba[l],
                              lru_wx[l], lru_bx[l], lru_lambda[l], w_att_o[l], w_rec_o[l], w_out[l])
        ffn2 = lambda h: swiglu(h, ffn2_w_gu[l], ffn2_w_down[l])
        x = sandwich(x, ffn1, norm_pre[l, 0], norm_post[l, 0], mod[:, 0, 0], mod[:, 0, 1], mod[:, 0, 2], 0.5)
        x = sandwich(x, mix, norm_pre[l, 1], norm_post[l, 1], mod[:, 1, 0], mod[:, 1, 1], mod[:, 1, 2], 1.0)
        x = sandwich(x, ffn2, norm_pre[l, 2], norm_post[l, 2], mod[:, 2, 0], mod[:, 2, 1], mod[:, 2, 2], 0.5)
    return x
```

```python
import functools
import math

import jax
import jax.numpy as jnp
from jax import lax
from jax.experimental import pallas as pl
from jax.experimental.pallas import tpu as pltpu

D_MODEL = 1024
CHUNK = 64
LEFT_CHUNKS = 8
ATT_HEADS = 8
ATT_HEAD_DIM = 64
ATT_WIDTH = ATT_HEADS * ATT_HEAD_DIM
MAX_REL = 128
LRU_WIDTH = D_MODEL
LRU_BLOCKS = 16
LRU_BLOCK = LRU_WIDTH // LRU_BLOCKS
CONV_WIDTH = 4
LRU_C = 8.0
D_FF = 2816
N_SUB = 3
EPS = 1e-6

LANES = 128
SUBLANES = 8
VMEM_LIMIT_BYTES = 56 * 1024 * 1024

TOKEN_TILE = 512
FF_TILE = 256
ADA_TILE = 1536
Q_TILE = 4 * CHUNK
K_BLOCKS = LEFT_CHUNKS * CHUNK // Q_TILE + 1
BAND = K_BLOCKS * Q_TILE
LRU_TILE = 256
MASK_VALUE = -1e30
BF16 = jnp.bfloat16
F32 = jnp.float32


def _dot(a, b):
    return jnp.dot(a, b, preferred_element_type=F32)


def _rms(x, g):
    return x * lax.rsqrt(jnp.mean(x * x, axis=-1, keepdims=True) + EPS) * g


def _sigmoid(x):
    return 1.0 / (1.0 + jnp.exp(-x))


def _gelu_tanh(x):
    c = math.sqrt(2.0 / math.pi)
    return x * (0.5 * (1.0 + jnp.tanh(c * (x + 0.044715 * (x * x * x)))))


def _mod_rows(mod_ref, sub):
    shift = mod_ref[3 * sub:3 * sub + 1, :]
    scale = mod_ref[3 * sub + 1:3 * sub + 2, :]
    gate = mod_ref[3 * sub + 2:3 * sub + 3, :]
    return shift, scale, gate


def _resident(shape):
    nd = len(shape)
    return pl.BlockSpec(shape, lambda *_: (0,) * nd, pipeline_mode=pl.Buffered(1))


def _params(*semantics):
    return pltpu.CompilerParams(dimension_semantics=semantics,
                                vmem_limit_bytes=VMEM_LIMIT_BYTES)


def _ada_kernel(c_ref, w_ref, b_ref, o_ref):
    c = c_ref[...]
    c_act = (c * _sigmoid(c)).astype(BF16)
    o_ref[...] = _dot(c_act, w_ref[...].astype(BF16)) + b_ref[...]


def _ada(c, w, b):
    bsz, d = c.shape
    n = w.shape[1]
    return pl.pallas_call(
        _ada_kernel,
        grid=(n // ADA_TILE,),
        in_specs=[pl.BlockSpec((bsz, d), lambda j: (0, 0)),
                  pl.BlockSpec((d, ADA_TILE), lambda j: (0, j)),
                  pl.BlockSpec((1, ADA_TILE), lambda j: (0, j))],
        out_specs=pl.BlockSpec((bsz, ADA_TILE), lambda j: (0, j)),
        out_shape=jax.ShapeDtypeStruct((bsz, n), F32),
        compiler_params=_params("arbitrary"),
        name="adaln",
    )(c, w, b.reshape(1, n))


def _ffn_kernel(x_ref, mod_ref, gpre_ref, gpost_ref, wgu_ref, wdn_ref, o_ref, act_ref,
                *, sub, res_w):
    x = x_ref[...]
    shift, scale, gate = _mod_rows(mod_ref, sub)
    h = (_rms(x, gpre_ref[sub:sub + 1, :]) * (1.0 + scale) + shift).astype(BF16)
    for j in range(D_FF // FF_TILE):
        lo = j * FF_TILE
        g = _dot(h, wgu_ref[:, lo:lo + FF_TILE])
        u = _dot(h, wgu_ref[:, D_FF + lo:D_FF + lo + FF_TILE])
        act_ref[:, lo:lo + FF_TILE] = (g * _sigmoid(g) * u).astype(BF16)
    y = _dot(act_ref[...], wdn_ref[...])
    o_ref[...] = x + (res_w * gate) * _rms(y, gpost_ref[sub:sub + 1, :])


def _ffn(x2, mod, norm_pre, norm_post, w_gu, w_down, *, sub, res_w, seq):
    t, d = x2.shape
    tiles_per_seq = seq // TOKEN_TILE
    return pl.pallas_call(
        functools.partial(_ffn_kernel, sub=sub, res_w=res_w),
        grid=(t // TOKEN_TILE,),
        in_specs=[pl.BlockSpec((TOKEN_TILE, d), lambda i: (i, 0)),
                  pl.BlockSpec((None, 3 * N_SUB, d), lambda i: (i // tiles_per_seq, 0, 0)),
                  _resident(norm_pre.shape),
                  _resident(norm_post.shape),
                  _resident(w_gu.shape),
                  _resident(w_down.shape)],
        out_specs=pl.BlockSpec((TOKEN_TILE, d), lambda i: (i, 0)),
        out_shape=jax.ShapeDtypeStruct((t, d), F32),
        scratch_shapes=[pltpu.VMEM((TOKEN_TILE, D_FF), BF16)],
        compiler_params=_params("arbitrary"),
        name=f"ffn{sub}",
    )(x2, mod, norm_pre, norm_post, w_gu, w_down)


def _proj_kernel(x_ref, mod_ref, gpre_ref, w_ref,
                 q_ref, k_ref, v_ref, xr_ref, gy_ref, sa_ref, sr_ref):
    x = x_ref[...]
    shift, scale, _ = _mod_rows(mod_ref, 1)
    h = (_rms(x, gpre_ref[1:2, :]) * (1.0 + scale) + shift).astype(BF16)
    a = ATT_WIDTH
    w = LRU_WIDTH
    q_ref[...] = (_dot(h, w_ref[:, 0:a]) * (ATT_HEAD_DIM ** -0.5)).astype(BF16)
    k_ref[...] = _dot(h, w_ref[:, a:2 * a]).astype(BF16)
    v_ref[...] = _dot(h, w_ref[:, 2 * a:3 * a]).astype(BF16)
    o = 3 * a
    xr_ref[...] = _dot(h, w_ref[:, o:o + w])
    gy_ref[...] = _gelu_tanh(_dot(h, w_ref[:, o + w:o + 2 * w])).astype(BF16)
    sa_ref[...] = _sigmoid(_dot(h, w_ref[:, o + 2 * w:o + 3 * w])).astype(BF16)
    sr_ref[...] = _sigmoid(_dot(h, w_ref[:, o + 3 * w:o + 4 * w])).astype(BF16)


def _proj(x2, mod, norm_pre, w_in, *, seq):
    t, d = x2.shape
    tiles_per_seq = seq // TOKEN_TILE
    row = lambda n: pl.BlockSpec((TOKEN_TILE, n), lambda i: (i, 0))
    widths = (ATT_WIDTH, ATT_WIDTH, ATT_WIDTH, LRU_WIDTH, LRU_WIDTH, D_MODEL, D_MODEL)
    dtypes = (BF16, BF16, BF16, F32, BF16, BF16, BF16)
    return pl.pallas_call(
        _proj_kernel,
        grid=(t // TOKEN_TILE,),
        in_specs=[row(d),
                  pl.BlockSpec((None, 3 * N_SUB, d), lambda i: (i // tiles_per_seq, 0, 0)),
                  _resident(norm_pre.shape),
                  _resident(w_in.shape)],
        out_specs=[row(n) for n in widths],
        out_shape=[jax.ShapeDtypeStruct((t, n), dt) for n, dt in zip(widths, dtypes)],
        compiler_params=_params("arbitrary"),
        name="mixer_proj",
    )(x2, mod, norm_pre, w_in)


REL_PAD = 384
TOEPLITZ = 1024


def _bias_kernel(tab_ref, o_ref):
    tab = tab_ref[...]
    hi = tab.astype(BF16)
    r1 = tab - hi.astype(F32)
    mid = r1.astype(BF16)
    lo = (r1 - mid.astype(F32)).astype(BF16)
    d_idx = lax.broadcasted_iota(jnp.int32, (REL_PAD, TOEPLITZ), 0)
    m_idx = lax.broadcasted_iota(jnp.int32, (REL_PAD, TOEPLITZ), 1)
    rel = jnp.clip((BAND - 1) - m_idx, -MAX_REL, MAX_REL) + MAX_REL
    onehot = jnp.where(d_idx == rel, 1.0, 0.0).astype(BF16)
    profile = (_dot(lo, onehot) + _dot(mid, onehot)) + _dot(hi, onehot)

    r = lax.broadcasted_iota(jnp.int32, (Q_TILE, BAND), 0)
    kk = lax.broadcasted_iota(jnp.int32, (Q_TILE, BAND), 1)
    qc = r // CHUNK
    kc = kk // CHUNK
    band = jnp.where((kc >= qc) & (kc <= qc + LEFT_CHUNKS), 0.0, MASK_VALUE)
    for h in range(ATT_HEADS):
        rows = jnp.broadcast_to(profile[h:h + 1, :], (Q_TILE, TOEPLITZ))
        skew = pltpu.roll(rows, TOEPLITZ - (Q_TILE - 1), 1, stride=1, stride_axis=0)
        o_ref[h] = skew[:, :BAND] + band


def _bias_table(rel_bias):
    tab = jnp.pad(rel_bias, ((0, 0), (0, REL_PAD - rel_bias.shape[1])))
    return pl.pallas_call(
        _bias_kernel,
        out_shape=jax.ShapeDtypeStruct((ATT_HEADS, Q_TILE, BAND), F32),
        compiler_params=pltpu.CompilerParams(vmem_limit_bytes=VMEM_LIMIT_BYTES),
        name="rel_bias",
    )(tab)


def _attn_kernel(q_ref, k0_ref, k1_ref, k2_ref, v0_ref, v1_ref, v2_ref, bias_ref, o_ref):
    t = pl.program_id(1)
    blk = lax.broadcasted_iota(jnp.int32, (1, BAND), 1) // Q_TILE
    start_pen = jnp.where(blk + t < K_BLOCKS - 1, MASK_VALUE, 0.0)
    lane = lax.broadcasted_iota(jnp.int32, (1, LANES), 1)
    even = lane < ATT_HEAD_DIM
    zero = jnp.zeros((), BF16)
    for p in range(ATT_HEADS // 2):
        sl = slice(p * LANES, (p + 1) * LANES)
        qp = q_ref[:, sl]
        kp = jnp.concatenate([k0_ref[:, sl], k1_ref[:, sl], k2_ref[:, sl]], axis=0)
        vp = jnp.concatenate([v0_ref[:, sl], v1_ref[:, sl], v2_ref[:, sl]], axis=0)
        out = jnp.zeros((Q_TILE, LANES), F32)
        for e in range(2):
            mine = even if e == 0 else jnp.logical_not(even)
            qh = jnp.where(mine, qp, zero)
            vh = jnp.where(mine, vp, zero)
            s = lax.dot_general(qh, kp, (((1,), (1,)), ((), ())), preferred_element_type=F32)
            s = (s + start_pen) + bias_ref[2 * p + e]
            m = jnp.max(s, axis=-1, keepdims=True)
            pr = jnp.exp(s - m)
            l = jnp.sum(pr, axis=-1, keepdims=True)
            out = out + _dot(pr.astype(BF16), vh) * (1.0 / l)
        o_ref[:, sl] = out.astype(BF16)


def _attention(q, k, v, bias):
    bsz, seq, a = q.shape
    cur = pl.BlockSpec((None, Q_TILE, a), lambda b, t: (b, t, 0))
    back = lambda n: pl.BlockSpec((None, Q_TILE, a), lambda b, t: (b, jnp.maximum(t - n, 0), 0))
    kv_specs = [back(2), back(1), cur]
    return pl.pallas_call(
        _attn_kernel,
        grid=(bsz, seq // Q_TILE),
        in_specs=[cur] + kv_specs + kv_specs + [_resident(bias.shape)],
        out_specs=cur,
        out_shape=jax.ShapeDtypeStruct((bsz, seq, a), BF16),
        compiler_params=_params("arbitrary", "arbitrary"),
        name="chunk_attn",
    )(q, k, k, k, v, v, v, bias)


def _lru_kernel(xr_ref, gy_ref, cw_ref, cb_ref, w2_ref, ba_ref, bx_ref, lam_ref, o_ref,
                ext_ref, h_ref):
    t = pl.program_id(1)
    halo = SUBLANES

    @pl.when(t == 0)
    def _():
        ext_ref[0:halo, :] = jnp.zeros((halo, LRU_WIDTH), F32)
        h_ref[...] = jnp.zeros_like(h_ref)

    ext_ref[halo:halo + LRU_TILE, :] = xr_ref[...]
    neg_lam = -lam_ref[...]
    softplus = jnp.maximum(neg_lam, 0.0) + jnp.log1p(jnp.exp(-jnp.abs(neg_lam)))
    sub = lax.broadcasted_iota(jnp.int32, (SUBLANES, LANES), 0)
    n_grp = LRU_TILE // SUBLANES
    for g in range(LRU_WIDTH // LANES):
        sl = slice(g * LANES, (g + 1) * LANES)
        xc = cb_ref[:, sl] + jnp.zeros((LRU_TILE, LANES), F32)
        for w in range(CONV_WIDTH):
            lo = halo - (CONV_WIDTH - 1) + w
            xc = xc + ext_ref[lo:lo + LRU_TILE, sl] * cw_ref[w:w + 1, sl]
        z = _dot(xc.astype(BF16), w2_ref[g])
        r = _sigmoid(z[:, :LANES] + ba_ref[:, sl])
        i = _sigmoid(z[:, LANES:] + bx_ref[:, sl])
        log_a = (-LRU_C * softplus[:, sl]) * r
        a = jnp.exp(log_a)
        th = jnp.tanh(log_a)
        u = jnp.sqrt(-2.0 * th / (1.0 - th)) * (i * xc)
        a3 = a.reshape(n_grp, SUBLANES, LANES)
        u3 = u.reshape(n_grp, SUBLANES, LANES)
        for s in (1, 2, 4):
            keep = sub >= s
            a_prev = jnp.where(keep, pltpu.roll(a3, s, 1), 1.0)
            u_prev = jnp.where(keep, pltpu.roll(u3, s, 1), 0.0)
            u3 = u3 + a3 * u_prev
            a3 = a3 * a_prev
        h = h_ref[:, sl]
        rows = []
        for n in range(n_grp):
            hg = a3[n] * h + u3[n]
            h = jnp.broadcast_to(hg[SUBLANES - 1:SUBLANES, :], (SUBLANES, LANES))
            rows.append(hg)
        h_ref[:, sl] = h
        hs = jnp.concatenate(rows, axis=0)
        o_ref[:, sl] = (hs * gy_ref[:, sl].astype(F32)).astype(BF16)
    ext_ref[0:halo, :] = ext_ref[LRU_TILE:LRU_TILE + halo, :]


def _lru(xr, gy, conv_w, conv_b, w2, ba, bx, lam):
    bsz, seq, w = xr.shape
    tile = pl.BlockSpec((None, LRU_TILE, w), lambda b, t: (b, t, 0))
    return pl.pallas_call(
        _lru_kernel,
        grid=(bsz, seq // LRU_TILE),
        in_specs=[tile, tile,
                  _resident(conv_w.shape), _resident(conv_b.shape), _resident(w2.shape),
                  _resident(ba.shape), _resident(bx.shape), _resident(lam.shape)],
        out_specs=tile,
        out_shape=jax.ShapeDtypeStruct((bsz, seq, w), BF16),
        scratch_shapes=[pltpu.VMEM((SUBLANES + LRU_TILE, w), F32),
                        pltpu.VMEM((SUBLANES, w), F32)],
        compiler_params=_params("arbitrary", "arbitrary"),
        name="rglru",
    )(xr, gy, conv_w, conv_b, w2, ba, bx, lam)


def _pair_block_diag(wa, wx):
    def pairs(w):
        z = jnp.zeros_like(w[0::2])
        top = jnp.concatenate([w[0::2], z], axis=2)
        bot = jnp.concatenate([z, w[1::2]], axis=2)
        return jnp.concatenate([top, bot], axis=1)
    return jnp.concatenate([pairs(wa), pairs(wx)], axis=2)


def _mixout_kernel(x_ref, mod_ref, gpost_ref, att_ref, rec_ref, sa_ref, sr_ref,
                   wao_ref, wro_ref, wout_ref, o_ref):
    _, _, gate = _mod_rows(mod_ref, 1)
    att = _dot(att_ref[...], wao_ref[...])
    rec = _dot(rec_ref[...], wro_ref[...])
    merged = sa_ref[...].astype(F32) * att + sr_ref[...].astype(F32) * rec
    y = _dot(merged.astype(BF16), wout_ref[...])
    o_ref[...] = x_ref[...] + gate * _rms(y, gpost_ref[1:2, :])


def _mixout(x2, mod, norm_post, att, rec, sa, sr, w_att_o, w_rec_o, w_out, *, seq):
    t, d = x2.shape
    tiles_per_seq = seq // TOKEN_TILE
    row = lambda n: pl.BlockSpec((TOKEN_TILE, n), lambda i: (i, 0))
    return pl.pallas_call(
        _mixout_kernel,
        grid=(t // TOKEN_TILE,),
        in_specs=[row(d),
                  pl.BlockSpec((None, 3 * N_SUB, d), lambda i: (i // tiles_per_seq, 0, 0)),
                  _resident(norm_post.shape),
                  row(ATT_WIDTH), row(LRU_WIDTH), row(d), row(d),
                  _resident(w_att_o.shape), _resident(w_rec_o.shape), _resident(w_out.shape)],
        out_specs=row(d),
        out_shape=jax.ShapeDtypeStruct((t, d), F32),
        compiler_params=_params("arbitrary"),
        name="mixer_out",
    )(x2, mod, norm_post, att, rec, sa, sr, w_att_o, w_rec_o, w_out)


def _layer(x2, c, l, bsz, seq, w_ada, b_ada, norm_pre, norm_post, ffn1_w_gu, ffn1_w_down, w_in,
           rel_bias, conv_w, conv_b, lru_wa, lru_ba, lru_wx, lru_bx, lru_lambda, w_att_o,
           w_rec_o, w_out, ffn2_w_gu, ffn2_w_down):
    bf = lambda w: w[l].astype(BF16)
    mod = _ada(c, w_ada[l], b_ada[l]).reshape(bsz, 3 * N_SUB, D_MODEL)
    npre, npost = norm_pre[l], norm_post[l]

    x2 = _ffn(x2, mod, npre, npost, bf(ffn1_w_gu), bf(ffn1_w_down), sub=0, res_w=0.5, seq=seq)

    q, k, v, xr, gy, sa, sr = _proj(x2, mod, npre, bf(w_in), seq=seq)
    split = lambda a: a.reshape(bsz, seq, a.shape[-1])
    att = _attention(split(q), split(k), split(v), _bias_table(rel_bias[l]))
    w2 = _pair_block_diag(lru_wa[l], lru_wx[l]).astype(BF16)
    vec = lambda p: p[l].reshape(1, LRU_WIDTH)
    rec = _lru(split(xr), split(gy), conv_w[l], vec(conv_b), w2, vec(lru_ba), vec(lru_bx),
               vec(lru_lambda))
    flat = lambda a: a.reshape(bsz * seq, a.shape[-1])
    x2 = _mixout(x2, mod, npost, flat(att), flat(rec), sa, sr,
                 bf(w_att_o), bf(w_rec_o), bf(w_out), seq=seq)

    return _ffn(x2, mod, npre, npost, bf(ffn2_w_gu), bf(ffn2_w_down), sub=2, res_w=0.5, seq=seq)


def kernel(x, c, w_ada, b_ada, norm_pre, norm_post, ffn1_w_gu, ffn1_w_down, w_in, rel_bias, conv_w, conv_b, lru_wa, lru_ba, lru_wx, lru_bx, lru_lambda, w_att_o, w_rec_o, w_out, ffn2_w_gu, ffn2_w_down):
    bsz, seq, d = x.shape
    assert d == D_MODEL and seq % TOKEN_TILE == 0 and seq % Q_TILE == 0 and seq % LRU_TILE == 0
    x2 = x.reshape(bsz * seq, d)
    for l in range(w_ada.shape[0]):
        x2 = _layer(x2, c, l, bsz, seq, w_ada, b_ada, norm_pre, norm_post, ffn1_w_gu,
                    ffn1_w_down, w_in, rel_bias, conv_w, conv_b, lru_wa, lru_ba, lru_wx, lru_bx,
                    lru_lambda, w_att_o, w_rec_o, w_out, ffn2_w_gu, ffn2_w_down)
    return x2.reshape(bsz, seq, d)
```

```python
import functools
import math

import jax
import jax.numpy as jnp
from jax import lax
from jax.experimental import pallas as pl
from jax.experimental.pallas import tpu as pltpu

D_MODEL = 1024
CHUNK = 64
LEFT_CHUNKS = 8
ATT_HEADS = 8
ATT_HEAD_DIM = 64
ATT_WIDTH = ATT_HEADS * ATT_HEAD_DIM
MAX_REL = 128
LRU_WIDTH = D_MODEL
LRU_BLOCKS = 16
LRU_BLOCK = LRU_WIDTH // LRU_BLOCKS
CONV_WIDTH = 4
LRU_C = 8.0
D_FF = 2816
N_SUB = 3
EPS = 1e-6

LANES = 128
SUBLANES = 8
VMEM_LIMIT_BYTES = 56 * 1024 * 1024

TOKEN_TILE = 512
FFN_TOKEN_TILE = 1024
FF_TILE = 256
ADA_TILE = 1536
Q_TILE = 4 * CHUNK
K_BLOCKS = LEFT_CHUNKS * CHUNK // Q_TILE + 1
BAND = K_BLOCKS * Q_TILE
LRU_TILE = 256
MASK_VALUE = -1e30
BF16 = jnp.bfloat16
F32 = jnp.float32


def _dot(a, b):
    return jnp.dot(a, b, preferred_element_type=F32)


def _rms(x, g):
    return x * lax.rsqrt(jnp.mean(x * x, axis=-1, keepdims=True) + EPS) * g


def _sigmoid(x):
    return 1.0 / (1.0 + jnp.exp(-x))


def _gelu_tanh(x):
    c = math.sqrt(2.0 / math.pi)
    return x * (0.5 * (1.0 + jnp.tanh(c * (x + 0.044715 * (x * x * x)))))


def _mod_rows(mod_ref, sub):
    shift = mod_ref[3 * sub:3 * sub + 1, :]
    scale = mod_ref[3 * sub + 1:3 * sub + 2, :]
    gate = mod_ref[3 * sub + 2:3 * sub + 3, :]
    return shift, scale, gate


def _resident(shape):
    nd = len(shape)
    return pl.BlockSpec(shape, lambda *_: (0,) * nd, pipeline_mode=pl.Buffered(1))


def _params(*semantics):
    return pltpu.CompilerParams(dimension_semantics=semantics,
                                vmem_limit_bytes=VMEM_LIMIT_BYTES)


def _ada_kernel(c_ref, w_ref, b_ref, o_ref):
    c = c_ref[...]
    c_act = (c * _sigmoid(c)).astype(BF16)
    o_ref[...] = _dot(c_act, w_ref[...].astype(BF16)) + b_ref[...]


def _ada(c, w, b):
    bsz, d = c.shape
    n = w.shape[1]
    return pl.pallas_call(
        _ada_kernel,
        grid=(n // ADA_TILE,),
        in_specs=[pl.BlockSpec((bsz, d), lambda j: (0, 0)),
                  pl.BlockSpec((d, ADA_TILE), lambda j: (0, j)),
                  pl.BlockSpec((1, ADA_TILE), lambda j: (0, j))],
        out_specs=pl.BlockSpec((bsz, ADA_TILE), lambda j: (0, j)),
        out_shape=jax.ShapeDtypeStruct((bsz, n), F32),
        compiler_params=_params("arbitrary"),
        name="adaln",
    )(c, w, b.reshape(1, n))


def _ffn_kernel(x_ref, mod_ref, gpre_ref, gpost_ref, wgu_ref, wdn_ref, o_ref, act_ref,
                *, sub, res_w):
    x = x_ref[...]
    shift, scale, gate = _mod_rows(mod_ref, sub)
    h = (_rms(x, gpre_ref[sub:sub + 1, :]) * (1.0 + scale) + shift).astype(BF16)
    for j in range(D_FF // FF_TILE):
        lo = j * FF_TILE
        g = _dot(h, wgu_ref[:, lo:lo + FF_TILE])
        u = _dot(h, wgu_ref[:, D_FF + lo:D_FF + lo + FF_TILE])
        act_ref[:, lo:lo + FF_TILE] = (g * _sigmoid(g) * u).astype(BF16)
    y = _dot(act_ref[...], wdn_ref[...])
    o_ref[...] = x + (res_w * gate) * _rms(y, gpost_ref[sub:sub + 1, :])


def _ffn(x2, mod, norm_pre, norm_post, w_gu, w_down, *, sub, res_w, seq):
    t, d = x2.shape
    tiles_per_seq = seq // FFN_TOKEN_TILE
    return pl.pallas_call(
        functools.partial(_ffn_kernel, sub=sub, res_w=res_w),
        grid=(t // FFN_TOKEN_TILE,),
        in_specs=[pl.BlockSpec((FFN_TOKEN_TILE, d), lambda i: (i, 0)),
                  pl.BlockSpec((None, 3 * N_SUB, d), lambda i: (i // tiles_per_seq, 0, 0)),
                  _resident(norm_pre.shape),
                  _resident(norm_post.shape),
                  _resident(w_gu.shape),
                  _resident(w_down.shape)],
        out_specs=pl.BlockSpec((FFN_TOKEN_TILE, d), lambda i: (i, 0)),
        out_shape=jax.ShapeDtypeStruct((t, d), F32),
        scratch_shapes=[pltpu.VMEM((FFN_TOKEN_TILE, D_FF), BF16)],
        compiler_params=_params("arbitrary"),
        name=f"ffn{sub}",
    )(x2, mod, norm_pre, norm_post, w_gu, w_down)


def _proj_kernel(x_ref, mod_ref, gpre_ref, w_ref,
                 q_ref, k_ref, v_ref, xr_ref, gy_ref, sa_ref, sr_ref):
    x = x_ref[...]
    shift, scale, _ = _mod_rows(mod_ref, 1)
    h = (_rms(x, gpre_ref[1:2, :]) * (1.0 + scale) + shift).astype(BF16)
    a = ATT_WIDTH
    w = LRU_WIDTH
    q = (_dot(h, w_ref[:, 0:a]) * (ATT_HEAD_DIM ** -0.5)).astype(BF16)
    k = _dot(h, w_ref[:, a:2 * a]).astype(BF16)
    for p in range(a // LANES):
        q_ref[p] = q[:, p * LANES:(p + 1) * LANES]
        k_ref[p] = k[:, p * LANES:(p + 1) * LANES]
    v_t = _dot(h, w_ref[:, 2 * a:3 * a]).T.astype(BF16)
    v_ref[...] = v_t.reshape(ATT_HEADS, ATT_HEAD_DIM, v_t.shape[1])
    o = 3 * a
    xr_ref[...] = _dot(h, w_ref[:, o:o + w])
    gy_ref[...] = _gelu_tanh(_dot(h, w_ref[:, o + w:o + 2 * w])).astype(BF16)
    sa_ref[...] = _sigmoid(_dot(h, w_ref[:, o + 2 * w:o + 3 * w])).astype(BF16)
    sr_ref[...] = _sigmoid(_dot(h, w_ref[:, o + 3 * w:o + 4 * w])).astype(BF16)


def _proj(x2, mod, norm_pre, w_in, *, seq):
    t, d = x2.shape
    tiles_per_seq = seq // TOKEN_TILE
    row = lambda n: pl.BlockSpec((TOKEN_TILE, n), lambda i: (i, 0))
    tok = lambda n, dt: jax.ShapeDtypeStruct((t, n), dt)
    bsz, pairs = t // seq, ATT_WIDTH // LANES
    qk_spec = pl.BlockSpec((None, pairs, TOKEN_TILE, LANES),
                           lambda i: (i // tiles_per_seq, 0, i % tiles_per_seq, 0))
    qk_shape = jax.ShapeDtypeStruct((bsz, pairs, seq, LANES), BF16)
    v_spec = pl.BlockSpec((None, ATT_HEADS, ATT_HEAD_DIM, TOKEN_TILE),
                          lambda i: (i // tiles_per_seq, 0, 0, i % tiles_per_seq))
    v_shape = jax.ShapeDtypeStruct((bsz, ATT_HEADS, ATT_HEAD_DIM, seq), BF16)
    return pl.pallas_call(
        _proj_kernel,
        grid=(t // TOKEN_TILE,),
        in_specs=[row(d),
                  pl.BlockSpec((None, 3 * N_SUB, d), lambda i: (i // tiles_per_seq, 0, 0)),
                  _resident(norm_pre.shape),
                  _resident(w_in.shape)],
        out_specs=[qk_spec, qk_spec, v_spec, row(LRU_WIDTH), row(LRU_WIDTH),
                   row(D_MODEL), row(D_MODEL)],
        out_shape=[qk_shape, qk_shape, v_shape, tok(LRU_WIDTH, F32),
                   tok(LRU_WIDTH, BF16), tok(D_MODEL, BF16), tok(D_MODEL, BF16)],
        compiler_params=_params("arbitrary"),
        name="mixer_proj",
    )(x2, mod, norm_pre, w_in)


REL_PAD = 384
TOEPLITZ = 1024


def _bias_kernel(tab_ref, o_ref):
    tab = tab_ref[...]
    hi = tab.astype(BF16)
    r1 = tab - hi.astype(F32)
    mid = r1.astype(BF16)
    lo = (r1 - mid.astype(F32)).astype(BF16)
    d_idx = lax.broadcasted_iota(jnp.int32, (REL_PAD, TOEPLITZ), 0)
    m_idx = lax.broadcasted_iota(jnp.int32, (REL_PAD, TOEPLITZ), 1)
    rel = jnp.clip(m_idx - (Q_TILE - 1), -MAX_REL, MAX_REL) + MAX_REL
    onehot = jnp.where(d_idx == rel, 1.0, 0.0).astype(BF16)
    profile = (_dot(lo, onehot) + _dot(mid, onehot)) + _dot(hi, onehot)

    kk = lax.broadcasted_iota(jnp.int32, (BAND, Q_TILE), 0)
    r = lax.broadcasted_iota(jnp.int32, (BAND, Q_TILE), 1)
    qc = r // CHUNK
    kc = kk // CHUNK
    band = jnp.where((kc >= qc) & (kc <= qc + LEFT_CHUNKS), 0.0, MASK_VALUE)
    for h in range(ATT_HEADS):
        rows = jnp.broadcast_to(profile[h:h + 1, :], (BAND, TOEPLITZ))
        skew = pltpu.roll(rows, TOEPLITZ - (BAND - 1), 1, stride=1, stride_axis=0)
        o_ref[h] = skew[:, :Q_TILE] + band


def _bias_table(rel_bias):
    tab = jnp.pad(rel_bias, ((0, 0), (0, REL_PAD - rel_bias.shape[1])))
    return pl.pallas_call(
        _bias_kernel,
        out_shape=jax.ShapeDtypeStruct((ATT_HEADS, BAND, Q_TILE), F32),
        compiler_params=pltpu.CompilerParams(vmem_limit_bytes=VMEM_LIMIT_BYTES),
        name="rel_bias",
    )(tab)


def _attn_kernel(q_ref, k0_ref, k1_ref, k2_ref, v0_ref, v1_ref, v2_ref, bias_ref, o_ref,
                 acc_ref, s0_ref, s1_ref, p0_ref, p1_ref):
    t = pl.program_id(1)
    k_refs = (k0_ref, k1_ref, k2_ref)
    v_refs = (v0_ref, v1_ref, v2_ref)
    pens = [jnp.where(t + j < K_BLOCKS - 1, MASK_VALUE, 0.0) for j in range(K_BLOCKS - 1)] + [0.0]
    lane_head = lax.broadcasted_iota(jnp.int32, (1, LANES), 1) // ATT_HEAD_DIM
    zero = jnp.zeros((), BF16)
    n_half = Q_TILE // LANES
    live = [(2 * c * CHUNK, (2 * c + 2 + LEFT_CHUNKS) * CHUNK) for c in range(n_half)]

    s_refs = (s0_ref, s1_ref)
    p_refs = (p0_ref, p1_ref)

    @pl.when((pl.program_id(0) == 0) & (t == 0))
    def _():
        for p_ref in p_refs:
            p_ref[...] = jnp.zeros_like(p_ref)

    def scores(h, slot):
        pair = h // 2
        qh = jnp.where(lane_head == h % 2, q_ref[pair], zero)
        kcat = jnp.concatenate([r[pair] for r in k_refs], axis=0)
        s = lax.dot_general(kcat, qh, (((1,), (1,)), ((), ())), preferred_element_type=F32)
        s_refs[slot][...] = s + bias_ref[h]

    def col_reduce(parts, op):
        while len(parts) > 1:
            parts = [op(parts[i], parts[i + 1]) if i + 1 < len(parts) else parts[i]
                     for i in range(0, len(parts), 2)]
        return parts[0]

    def softmax(slot):
        s_ref, p_ref = s_refs[slot], p_refs[slot]
        groups = CHUNK // SUBLANES
        inv = []
        for c, (r0, r1) in enumerate(live):
            cl = slice(c * LANES, (c + 1) * LANES)
            blocks = range(r0, r1, CHUNK)
            part = [s_ref[r:r + CHUNK, cl].reshape(groups, SUBLANES, LANES).max(axis=0)
                    + pens[r // Q_TILE] for r in blocks]
            m = jnp.max(col_reduce(part, jnp.maximum), axis=0, keepdims=True)
            shifted = [m - pen for pen in pens]
            sums = []
            for r in blocks:
                pr = jnp.exp(s_ref[r:r + CHUNK, cl] - shifted[r // Q_TILE])
                sums.append(pr.reshape(groups, SUBLANES, LANES).sum(axis=0))
                p_ref[r:r + CHUNK, cl] = pr.astype(BF16)
            l = jnp.sum(col_reduce(sums, jnp.add), axis=0, keepdims=True)
            inv.append(1.0 / l)
        return jnp.concatenate(inv, axis=1)

    def weighted_values(h, slot, inv_l):
        v_t = jnp.concatenate([r[h] for r in v_refs], axis=1)
        acc_ref[h] = _dot(v_t, p_refs[slot][...]) * inv_l

    scores(0, 0)
    inv_prev = None
    for h in range(ATT_HEADS):
        if h + 1 < ATT_HEADS:
            scores(h + 1, (h + 1) % 2)
        inv_l = softmax(h % 2)
        if h > 0:
            weighted_values(h - 1, (h - 1) % 2, inv_prev)
        inv_prev = inv_l
    weighted_values(ATT_HEADS - 1, (ATT_HEADS - 1) % 2, inv_prev)
    for p in range(ATT_HEADS // 2):
        pair_t = acc_ref[2 * p:2 * p + 2].reshape(LANES, Q_TILE)
        o_ref[:, p * LANES:(p + 1) * LANES] = pair_t.T.astype(BF16)


def _attention(q, k, v_t, bias):
    bsz, pairs, seq, _ = q.shape
    qk = lambda n: pl.BlockSpec((None, pairs, Q_TILE, LANES),
                                lambda b, t: (b, 0, jnp.maximum(t - n, 0), 0))
    val = lambda n: pl.BlockSpec((None, ATT_HEADS, ATT_HEAD_DIM, Q_TILE),
                                 lambda b, t: (b, 0, 0, jnp.maximum(t - n, 0)))
    return pl.pallas_call(
        _attn_kernel,
        grid=(bsz, seq // Q_TILE),
        in_specs=[qk(0), qk(2), qk(1), qk(0), val(2), val(1), val(0), _resident(bias.shape)],
        out_specs=pl.BlockSpec((None, Q_TILE, ATT_WIDTH), lambda b, t: (b, t, 0)),
        out_shape=jax.ShapeDtypeStruct((bsz, seq, ATT_WIDTH), BF16),
        scratch_shapes=[pltpu.VMEM((ATT_HEADS, ATT_HEAD_DIM, Q_TILE), F32),
                        pltpu.VMEM((BAND, Q_TILE), F32), pltpu.VMEM((BAND, Q_TILE), F32),
                        pltpu.VMEM((BAND, Q_TILE), BF16), pltpu.VMEM((BAND, Q_TILE), BF16)],
        compiler_params=_params("arbitrary", "arbitrary"),
        name="chunk_attn",
    )(q, k, k, k, v_t, v_t, v_t, bias)


def _lru_kernel(xr_ref, gy_ref, cw_ref, cb_ref, w2_ref, ba_ref, bx_ref, lam_ref, o_ref,
                ext_ref, h_ref):
    t = pl.program_id(1)
    halo = SUBLANES

    @pl.when(t == 0)
    def _():
        ext_ref[0:halo, :] = jnp.zeros((halo, LRU_WIDTH), F32)
        h_ref[...] = jnp.zeros_like(h_ref)

    ext_ref[halo:halo + LRU_TILE, :] = xr_ref[...]
    neg_lam = -lam_ref[...]
    softplus = jnp.maximum(neg_lam, 0.0) + jnp.log1p(jnp.exp(-jnp.abs(neg_lam)))
    sub = lax.broadcasted_iota(jnp.int32, (SUBLANES, LANES), 0)
    n_grp = LRU_TILE // SUBLANES
    for g in range(LRU_WIDTH // LANES):
        sl = slice(g * LANES, (g + 1) * LANES)
        xc = cb_ref[:, sl] + jnp.zeros((LRU_TILE, LANES), F32)
        for w in range(CONV_WIDTH):
            lo = halo - (CONV_WIDTH - 1) + w
            xc = xc + ext_ref[lo:lo + LRU_TILE, sl] * cw_ref[w:w + 1, sl]
        z = _dot(xc.astype(BF16), w2_ref[g])
        r = _sigmoid(z[:, :LANES] + ba_ref[:, sl])
        i = _sigmoid(z[:, LANES:] + bx_ref[:, sl])
        log_a = (-LRU_C * softplus[:, sl]) * r
        a = jnp.exp(log_a)
        th = jnp.tanh(log_a)
        u = jnp.sqrt(-2.0 * th / (1.0 - th)) * (i * xc)
        a3 = a.reshape(n_grp, SUBLANES, LANES)
        u3 = u.reshape(n_grp, SUBLANES, LANES)
        for s in (1, 2, 4):
            keep = sub >= s
            a_prev = jnp.where(keep, pltpu.roll(a3, s, 1), 1.0)
            u_prev = jnp.where(keep, pltpu.roll(u3, s, 1), 0.0)
            u3 = u3 + a3 * u_prev
            a3 = a3 * a_prev
        h = h_ref[:, sl]
        rows = []
        for n in range(n_grp):
            hg = a3[n] * h + u3[n]
            h = jnp.broadcast_to(hg[SUBLANES - 1:SUBLANES, :], (SUBLANES, LANES))
            rows.append(hg)
        h_ref[:, sl] = h
        hs = jnp.concatenate(rows, axis=0)
        o_ref[:, sl] = (hs * gy_ref[:, sl].astype(F32)).astype(BF16)
    ext_ref[0:halo, :] = ext_ref[LRU_TILE:LRU_TILE + halo, :]


def _lru(xr, gy, conv_w, conv_b, w2, ba, bx, lam):
    bsz, seq, w = xr.shape
    tile = pl.BlockSpec((None, LRU_TILE, w), lambda b, t: (b, t, 0))
    return pl.pallas_call(
        _lru_kernel,
        grid=(bsz, seq // LRU_TILE),
        in_specs=[tile, tile,
                  _resident(conv_w.shape), _resident(conv_b.shape), _resident(w2.shape),
                  _resident(ba.shape), _resident(bx.shape), _resident(lam.shape)],
        out_specs=tile,
        out_shape=jax.ShapeDtypeStruct((bsz, seq, w), BF16),
        scratch_shapes=[pltpu.VMEM((SUBLANES + LRU_TILE, w), F32),
                        pltpu.VMEM((SUBLANES, w), F32)],
        compiler_params=_params("arbitrary", "arbitrary"),
        name="rglru",
    )(xr, gy, conv_w, conv_b, w2, ba, bx, lam)


def _pair_block_diag(wa, wx):
    def pairs(w):
        z = jnp.zeros_like(w[0::2])
        top = jnp.concatenate([w[0::2], z], axis=2)
        bot = jnp.concatenate([z, w[1::2]], axis=2)
        return jnp.concatenate([top, bot], axis=1)
    return jnp.concatenate([pairs(wa), pairs(wx)], axis=2)


def _mixout_kernel(x_ref, mod_ref, gpost_ref, att_ref, rec_ref, sa_ref, sr_ref,
                   wao_ref, wro_ref, wout_ref, o_ref):
    _, _, gate = _mod_rows(mod_ref, 1)
    att = _dot(att_ref[...], wao_ref[...])
    rec = _dot(rec_ref[...], wro_ref[...])
    merged = sa_ref[...].astype(F32) * att + sr_ref[...].astype(F32) * rec
    y = _dot(merged.astype(BF16), wout_ref[...])
    o_ref[...] = x_ref[...] + gate * _rms(y, gpost_ref[1:2, :])


def _mixout(x2, mod, norm_post, att, rec, sa, sr, w_att_o, w_rec_o, w_out, *, seq):
    t, d = x2.shape
    tiles_per_seq = seq // TOKEN_TILE
    row = lambda n: pl.BlockSpec((TOKEN_TILE, n), lambda i: (i, 0))
    return pl.pallas_call(
        _mixout_kernel,
        grid=(t // TOKEN_TILE,),
        in_specs=[row(d),
                  pl.BlockSpec((None, 3 * N_SUB, d), lambda i: (i // tiles_per_seq, 0, 0)),
                  _resident(norm_post.shape),
                  row(ATT_WIDTH), row(LRU_WIDTH), row(d), row(d),
                  _resident(w_att_o.shape), _resident(w_rec_o.shape), _resident(w_out.shape)],
        out_specs=row(d),
        out_shape=jax.ShapeDtypeStruct((t, d), F32),
        compiler_params=_params("arbitrary"),
        name="mixer_out",
    )(x2, mod, norm_post, att, rec, sa, sr, w_att_o, w_rec_o, w_out)


def _layer(x2, c, l, bsz, seq, w_ada, b_ada, norm_pre, norm_post, ffn1_w_gu, ffn1_w_down, w_in,
           rel_bias, conv_w, conv_b, lru_wa, lru_ba, lru_wx, lru_bx, lru_lambda, w_att_o,
           w_rec_o, w_out, ffn2_w_gu, ffn2_w_down):
    bf = lambda w: w[l].astype(BF16)
    mod = _ada(c, w_ada[l], b_ada[l]).reshape(bsz, 3 * N_SUB, D_MODEL)
    npre, npost = norm_pre[l], norm_post[l]

    x2 = _ffn(x2, mod, npre, npost, bf(ffn1_w_gu), bf(ffn1_w_down), sub=0, res_w=0.5, seq=seq)

    q, k, v, xr, gy, sa, sr = _proj(x2, mod, npre, bf(w_in), seq=seq)
    split = lambda a: a.reshape(bsz, seq, a.shape[-1])
    att = _attention(q, k, v, _bias_table(rel_bias[l]))
    w2 = _pair_block_diag(lru_wa[l], lru_wx[l]).astype(BF16)
    vec = lambda p: p[l].reshape(1, LRU_WIDTH)
    rec = _lru(split(xr), split(gy), conv_w[l], vec(conv_b), w2, vec(lru_ba), vec(lru_bx),
               vec(lru_lambda))
    flat = lambda a: a.reshape(bsz * seq, a.shape[-1])
    x2 = _mixout(x2, mod, npost, flat(att), flat(rec), sa, sr,
                 bf(w_att_o), bf(w_rec_o), bf(w_out), seq=seq)

    return _ffn(x2, mod, npre, npost, bf(ffn2_w_gu), bf(ffn2_w_down), sub=2, res_w=0.5, seq=seq)


def kernel(x, c, w_ada, b_ada, norm_pre, norm_post, ffn1_w_gu, ffn1_w_down, w_in, rel_bias, conv_w, conv_b, lru_wa, lru_ba, lru_wx, lru_bx, lru_lambda, w_att_o, w_rec_o, w_out, ffn2_w_gu, ffn2_w_down):
    bsz, seq, d = x.shape
    assert d == D_MODEL and all(seq % n == 0 for n in (TOKEN_TILE, FFN_TOKEN_TILE, Q_TILE, LRU_TILE))
    x2 = x.reshape(bsz * seq, d)
    for l in range(w_ada.shape[0]):
        x2 = _layer(x2, c, l, bsz, seq, w_ada, b_ada, norm_pre, norm_post, ffn1_w_gu,
                    ffn1_w_down, w_in, rel_bias, conv_w, conv_b, lru_wa, lru_ba, lru_wx, lru_bx,
                    lru_lambda, w_att_o, w_rec_o, w_out, ffn2_w_gu, ffn2_w_down)
    return x2.reshape(bsz, seq, d)
```

```python
import functools
import math

import jax
import jax.numpy as jnp
from jax import lax
from jax.experimental import pallas as pl
from jax.experimental.pallas import tpu as pltpu

D_MODEL = 1024
CHUNK = 64
LEFT_CHUNKS = 8
ATT_HEADS = 8
ATT_HEAD_DIM = 64
ATT_WIDTH = ATT_HEADS * ATT_HEAD_DIM
MAX_REL = 128
LRU_WIDTH = D_MODEL
LRU_BLOCKS = 16
LRU_BLOCK = LRU_WIDTH // LRU_BLOCKS
CONV_WIDTH = 4
LRU_C = 8.0
D_FF = 2816
N_SUB = 3
EPS = 1e-6

LANES = 128
SUBLANES = 8
VMEM_LIMIT_BYTES = 56 * 1024 * 1024

TOKEN_TILE = 512
FFN_TOKEN_TILE = 1024
FF_TILE = 256
ADA_TILE = 1536
Q_TILE = 4 * CHUNK
K_BLOCKS = LEFT_CHUNKS * CHUNK // Q_TILE + 1
BAND = K_BLOCKS * Q_TILE
LRU_TILE = 256
LRU_STEPS = LRU_TILE // SUBLANES
MASK_VALUE = -1e30
BF16 = jnp.bfloat16
F32 = jnp.float32


def _dot(a, b):
    return jnp.dot(a, b, preferred_element_type=F32)


def _rms(x, g):
    return x * lax.rsqrt(jnp.mean(x * x, axis=-1, keepdims=True) + EPS) * g


def _sigmoid(x):
    return 1.0 / (1.0 + jnp.exp(-x))


def _gelu_tanh(x):
    c = math.sqrt(2.0 / math.pi)
    return x * (0.5 * (1.0 + jnp.tanh(c * (x + 0.044715 * (x * x * x)))))


def _mod_rows(mod_ref, sub):
    shift = mod_ref[3 * sub:3 * sub + 1, :]
    scale = mod_ref[3 * sub + 1:3 * sub + 2, :]
    gate = mod_ref[3 * sub + 2:3 * sub + 3, :]
    return shift, scale, gate


def _resident(shape):
    nd = len(shape)
    return pl.BlockSpec(shape, lambda *_: (0,) * nd, pipeline_mode=pl.Buffered(1))


def _params(*semantics):
    return pltpu.CompilerParams(dimension_semantics=semantics,
                                vmem_limit_bytes=VMEM_LIMIT_BYTES)


def _ada_kernel(c_ref, w_ref, b_ref, o_ref):
    c = c_ref[...]
    c_act = (c * _sigmoid(c)).astype(BF16)
    o_ref[...] = _dot(c_act, w_ref[...].astype(BF16)) + b_ref[...]


def _ada(c, w, b):
    bsz, d = c.shape
    n = w.shape[1]
    return pl.pallas_call(
        _ada_kernel,
        grid=(n // ADA_TILE,),
        in_specs=[pl.BlockSpec((bsz, d), lambda j: (0, 0)),
                  pl.BlockSpec((d, ADA_TILE), lambda j: (0, j)),
                  pl.BlockSpec((1, ADA_TILE), lambda j: (0, j))],
        out_specs=pl.BlockSpec((bsz, ADA_TILE), lambda j: (0, j)),
        out_shape=jax.ShapeDtypeStruct((bsz, n), F32),
        compiler_params=_params("arbitrary"),
        name="adaln",
    )(c, w, b.reshape(1, n))


def _ffn_kernel(x_ref, mod_ref, gpre_ref, gpost_ref, wgu_ref, wdn_ref, o_ref, act_ref,
                *, sub, res_w):
    x = x_ref[...]
    shift, scale, gate = _mod_rows(mod_ref, sub)
    h = (_rms(x, gpre_ref[sub:sub + 1, :]) * (1.0 + scale) + shift).astype(BF16)
    for j in range(D_FF // FF_TILE):
        lo = j * FF_TILE
        g = _dot(h, wgu_ref[:, lo:lo + FF_TILE])
        u = _dot(h, wgu_ref[:, D_FF + lo:D_FF + lo + FF_TILE])
        act_ref[:, lo:lo + FF_TILE] = (g * _sigmoid(g) * u).astype(BF16)
    y = _dot(act_ref[...], wdn_ref[...])
    o_ref[...] = x + (res_w * gate) * _rms(y, gpost_ref[sub:sub + 1, :])


def _ffn(x2, mod, norm_pre, norm_post, w_gu, w_down, *, sub, res_w, seq):
    t, d = x2.shape
    tiles_per_seq = seq // FFN_TOKEN_TILE
    return pl.pallas_call(
        functools.partial(_ffn_kernel, sub=sub, res_w=res_w),
        grid=(t // FFN_TOKEN_TILE,),
        in_specs=[pl.BlockSpec((FFN_TOKEN_TILE, d), lambda i: (i, 0)),
                  pl.BlockSpec((None, 3 * N_SUB, d), lambda i: (i // tiles_per_seq, 0, 0)),
                  _resident(norm_pre.shape),
                  _resident(norm_post.shape),
                  _resident(w_gu.shape),
                  _resident(w_down.shape)],
        out_specs=pl.BlockSpec((FFN_TOKEN_TILE, d), lambda i: (i, 0)),
        out_shape=jax.ShapeDtypeStruct((t, d), F32),
        scratch_shapes=[pltpu.VMEM((FFN_TOKEN_TILE, D_FF), BF16)],
        compiler_params=_params("arbitrary"),
        name=f"ffn{sub}",
    )(x2, mod, norm_pre, norm_post, w_gu, w_down)


def _proj_kernel(x_ref, mod_ref, gpre_ref, w_ref,
                 q_ref, k_ref, v_ref, xr_ref, gy_ref, sa_ref, sr_ref):
    x = x_ref[...]
    shift, scale, _ = _mod_rows(mod_ref, 1)
    h = (_rms(x, gpre_ref[1:2, :]) * (1.0 + scale) + shift).astype(BF16)
    a = ATT_WIDTH
    w = LRU_WIDTH
    q = (_dot(h, w_ref[:, 0:a]) * (ATT_HEAD_DIM ** -0.5)).astype(BF16)
    k = _dot(h, w_ref[:, a:2 * a]).astype(BF16)
    for p in range(a // LANES):
        q_ref[p] = q[:, p * LANES:(p + 1) * LANES]
        k_ref[p] = k[:, p * LANES:(p + 1) * LANES]
    v_t = _dot(h, w_ref[:, 2 * a:3 * a]).T.astype(BF16)
    v_ref[...] = v_t.reshape(ATT_HEADS, ATT_HEAD_DIM, v_t.shape[1])
    o = 3 * a
    xr = _dot(h, w_ref[:, o:o + w])
    for src_row in range(0, x.shape[0], LRU_STEPS):
        tile0, j = src_row // LRU_TILE * LRU_TILE, src_row % LRU_TILE // LRU_STEPS
        for g in range(w // LANES):
            xr_ref[g, pl.ds(tile0 + j, LRU_STEPS, stride=SUBLANES), :] = (
                xr[src_row:src_row + LRU_STEPS, g * LANES:(g + 1) * LANES])
    gy_ref[...] = _gelu_tanh(_dot(h, w_ref[:, o + w:o + 2 * w])).astype(BF16)
    sa_ref[...] = _sigmoid(_dot(h, w_ref[:, o + 2 * w:o + 3 * w])).astype(BF16)
    sr_ref[...] = _sigmoid(_dot(h, w_ref[:, o + 3 * w:o + 4 * w])).astype(BF16)


def _proj(x2, mod, norm_pre, w_in, *, seq):
    t, d = x2.shape
    tiles_per_seq = seq // TOKEN_TILE
    row = lambda n: pl.BlockSpec((TOKEN_TILE, n), lambda i: (i, 0))
    tok = lambda n, dt: jax.ShapeDtypeStruct((t, n), dt)
    bsz, pairs = t // seq, ATT_WIDTH // LANES
    qk_spec = pl.BlockSpec((None, pairs, TOKEN_TILE, LANES),
                           lambda i: (i // tiles_per_seq, 0, i % tiles_per_seq, 0))
    qk_shape = jax.ShapeDtypeStruct((bsz, pairs, seq, LANES), BF16)
    v_spec = pl.BlockSpec((None, ATT_HEADS, ATT_HEAD_DIM, TOKEN_TILE),
                          lambda i: (i // tiles_per_seq, 0, 0, i % tiles_per_seq))
    v_shape = jax.ShapeDtypeStruct((bsz, ATT_HEADS, ATT_HEAD_DIM, seq), BF16)
    xr_spec = pl.BlockSpec((LRU_WIDTH // LANES, TOKEN_TILE, LANES), lambda i: (0, i, 0))
    xr_shape = jax.ShapeDtypeStruct((LRU_WIDTH // LANES, t, LANES), F32)
    return pl.pallas_call(
        _proj_kernel,
        grid=(t // TOKEN_TILE,),
        in_specs=[row(d),
                  pl.BlockSpec((None, 3 * N_SUB, d), lambda i: (i // tiles_per_seq, 0, 0)),
                  _resident(norm_pre.shape),
                  _resident(w_in.shape)],
        out_specs=[qk_spec, qk_spec, v_spec, xr_spec, row(LRU_WIDTH),
                   row(D_MODEL), row(D_MODEL)],
        out_shape=[qk_shape, qk_shape, v_shape, xr_shape,
                   tok(LRU_WIDTH, BF16), tok(D_MODEL, BF16), tok(D_MODEL, BF16)],
        compiler_params=_params("arbitrary"),
        name="mixer_proj",
    )(x2, mod, norm_pre, w_in)


REL_PAD = 384
TOEPLITZ = 1024


def _bias_kernel(tab_ref, o_ref):
    tab = tab_ref[...]
    hi = tab.astype(BF16)
    r1 = tab - hi.astype(F32)
    mid = r1.astype(BF16)
    lo = (r1 - mid.astype(F32)).astype(BF16)
    d_idx = lax.broadcasted_iota(jnp.int32, (REL_PAD, TOEPLITZ), 0)
    m_idx = lax.broadcasted_iota(jnp.int32, (REL_PAD, TOEPLITZ), 1)
    rel = jnp.clip(m_idx - (Q_TILE - 1), -MAX_REL, MAX_REL) + MAX_REL
    onehot = jnp.where(d_idx == rel, 1.0, 0.0).astype(BF16)
    profile = (_dot(lo, onehot) + _dot(mid, onehot)) + _dot(hi, onehot)

    kk = lax.broadcasted_iota(jnp.int32, (BAND, Q_TILE), 0)
    r = lax.broadcasted_iota(jnp.int32, (BAND, Q_TILE), 1)
    qc = r // CHUNK
    kc = kk // CHUNK
    band = jnp.where((kc >= qc) & (kc <= qc + LEFT_CHUNKS), 0.0, MASK_VALUE)
    for h in range(ATT_HEADS):
        rows = jnp.broadcast_to(profile[h:h + 1, :], (BAND, TOEPLITZ))
        skew = pltpu.roll(rows, TOEPLITZ - (BAND - 1), 1, stride=1, stride_axis=0)
        o_ref[h] = skew[:, :Q_TILE] + band


def _bias_table(rel_bias):
    tab = jnp.pad(rel_bias, ((0, 0), (0, REL_PAD - rel_bias.shape[1])))
    return pl.pallas_call(
        _bias_kernel,
        out_shape=jax.ShapeDtypeStruct((ATT_HEADS, BAND, Q_TILE), F32),
        compiler_params=pltpu.CompilerParams(vmem_limit_bytes=VMEM_LIMIT_BYTES),
        name="rel_bias",
    )(tab)


def _attn_kernel(q_ref, k0_ref, k1_ref, k2_ref, v0_ref, v1_ref, v2_ref, bias_ref, o_ref,
                 acc_ref, s0_ref, s1_ref, p0_ref, p1_ref):
    t = pl.program_id(1)
    k_refs = (k0_ref, k1_ref, k2_ref)
    v_refs = (v0_ref, v1_ref, v2_ref)
    pens = [jnp.where(t + j < K_BLOCKS - 1, MASK_VALUE, 0.0) for j in range(K_BLOCKS - 1)] + [0.0]
    lane_head = lax.broadcasted_iota(jnp.int32, (1, LANES), 1) // ATT_HEAD_DIM
    zero = jnp.zeros((), BF16)
    n_half = Q_TILE // LANES
    live = [(2 * c * CHUNK, (2 * c + 2 + LEFT_CHUNKS) * CHUNK) for c in range(n_half)]

    s_refs = (s0_ref, s1_ref)
    p_refs = (p0_ref, p1_ref)

    @pl.when((pl.program_id(0) == 0) & (t == 0))
    def _():
        for p_ref in p_refs:
            p_ref[...] = jnp.zeros_like(p_ref)

    def scores(h, slot):
        pair = h // 2
        qh = jnp.where(lane_head == h % 2, q_ref[pair], zero)
        kcat = jnp.concatenate([r[pair] for r in k_refs], axis=0)
        s = lax.dot_general(kcat, qh, (((1,), (1,)), ((), ())), preferred_element_type=F32)
        s_refs[slot][...] = s + bias_ref[h]

    def col_reduce(parts, op):
        while len(parts) > 1:
            parts = [op(parts[i], parts[i + 1]) if i + 1 < len(parts) else parts[i]
                     for i in range(0, len(parts), 2)]
        return parts[0]

    def softmax(slot):
        s_ref, p_ref = s_refs[slot], p_refs[slot]
        groups = CHUNK // SUBLANES
        inv = []
        for c, (r0, r1) in enumerate(live):
            cl = slice(c * LANES, (c + 1) * LANES)
            blocks = range(r0, r1, CHUNK)
            part = [s_ref[r:r + CHUNK, cl].reshape(groups, SUBLANES, LANES).max(axis=0)
                    + pens[r // Q_TILE] for r in blocks]
            m = jnp.max(col_reduce(part, jnp.maximum), axis=0, keepdims=True)
            shifted = [m - pen for pen in pens]
            sums = []
            for r in blocks:
                pr = jnp.exp(s_ref[r:r + CHUNK, cl] - shifted[r // Q_TILE])
                sums.append(pr.reshape(groups, SUBLANES, LANES).sum(axis=0))
                p_ref[r:r + CHUNK, cl] = pr.astype(BF16)
            l = jnp.sum(col_reduce(sums, jnp.add), axis=0, keepdims=True)
            inv.append(1.0 / l)
        return jnp.concatenate(inv, axis=1)

    def weighted_values(h, slot, inv_l):
        v_t = jnp.concatenate([r[h] for r in v_refs], axis=1)
        acc_ref[h] = _dot(v_t, p_refs[slot][...]) * inv_l

    scores(0, 0)
    inv_prev = None
    for h in range(ATT_HEADS):
        if h + 1 < ATT_HEADS:
            scores(h + 1, (h + 1) % 2)
        inv_l = softmax(h % 2)
        if h > 0:
            weighted_values(h - 1, (h - 1) % 2, inv_prev)
        inv_prev = inv_l
    weighted_values(ATT_HEADS - 1, (ATT_HEADS - 1) % 2, inv_prev)
    for p in range(ATT_HEADS // 2):
        pair_t = acc_ref[2 * p:2 * p + 2].reshape(LANES, Q_TILE)
        o_ref[:, p * LANES:(p + 1) * LANES] = pair_t.T.astype(BF16)


def _attention(q, k, v_t, bias):
    bsz, pairs, seq, _ = q.shape
    qk = lambda n: pl.BlockSpec((None, pairs, Q_TILE, LANES),
                                lambda b, t: (b, 0, jnp.maximum(t - n, 0), 0))
    val = lambda n: pl.BlockSpec((None, ATT_HEADS, ATT_HEAD_DIM, Q_TILE),
                                 lambda b, t: (b, 0, 0, jnp.maximum(t - n, 0)))
    return pl.pallas_call(
        _attn_kernel,
        grid=(bsz, seq // Q_TILE),
        in_specs=[qk(0), qk(2), qk(1), qk(0), val(2), val(1), val(0), _resident(bias.shape)],
        out_specs=pl.BlockSpec((None, Q_TILE, ATT_WIDTH), lambda b, t: (b, t, 0)),
        out_shape=jax.ShapeDtypeStruct((bsz, seq, ATT_WIDTH), BF16),
        scratch_shapes=[pltpu.VMEM((ATT_HEADS, ATT_HEAD_DIM, Q_TILE), F32),
                        pltpu.VMEM((BAND, Q_TILE), F32), pltpu.VMEM((BAND, Q_TILE), F32),
                        pltpu.VMEM((BAND, Q_TILE), BF16), pltpu.VMEM((BAND, Q_TILE), BF16)],
        compiler_params=_params("arbitrary", "arbitrary"),
        name="chunk_attn",
    )(q, k, k, k, v_t, v_t, v_t, bias)


def _lru_kernel(xr_ref, gy_ref, cw_ref, cb_ref, w2_ref, ba_ref, bx_ref, lam_ref, o_ref,
                tail_ref, h_ref, hs_ref):
    t = pl.program_id(1)

    @pl.when(t == 0)
    def _():
        tail_ref[...] = jnp.zeros_like(tail_ref)
        h_ref[...] = jnp.zeros_like(h_ref)

    neg_lam = -lam_ref[...]
    softplus = jnp.maximum(neg_lam, 0.0) + jnp.log1p(jnp.exp(-jnp.abs(neg_lam)))
    decay = (-0.5 * LRU_C) * softplus
    sub = lax.broadcasted_iota(jnp.int32, (SUBLANES, LANES), 0)
    taps = CONV_WIDTH - 1
    for g in range(LRU_WIDTH // LANES):
        sl = slice(g * LANES, (g + 1) * LANES)
        x3 = xr_ref[g].reshape(LRU_STEPS, SUBLANES, LANES)
        cur_tail = x3[LRU_STEPS - taps:]
        lead = pltpu.roll(jnp.where(sub == SUBLANES - 1, tail_ref[g], cur_tail), 1, 1)
        tail_ref[g] = cur_tail
        xext = jnp.concatenate([lead, x3], axis=0)
        xc3 = cb_ref[:, sl] + cw_ref[0:1, sl] * xext[0:LRU_STEPS]
        for w in range(1, CONV_WIDTH):
            xc3 = xc3 + cw_ref[w:w + 1, sl] * xext[w:w + LRU_STEPS]
        xc = xc3.reshape(LRU_TILE, LANES)
        z = _dot(xc.astype(BF16), w2_ref[g])
        ta = jnp.tanh(z[:, :LANES] + ba_ref[:, sl])
        tx = jnp.tanh(z[:, LANES:] + bx_ref[:, sl])
        log_a = decay[:, sl] * (ta + 1.0)
        a = jnp.exp(log_a)
        th = jnp.tanh(log_a)
        q = (-0.5 * th) / (1.0 - th)
        half_mult = jnp.where(q > 0.0, q * lax.rsqrt(q), 0.0)
        u = half_mult * ((tx + 1.0) * xc)
        a3 = a.reshape(LRU_STEPS, SUBLANES, LANES)
        u3 = u.reshape(LRU_STEPS, SUBLANES, LANES)
        loc = [u3[0]]
        prod = [a3[0]]
        for i in range(1, LRU_STEPS):
            loc.append(a3[i] * loc[-1] + u3[i])
            prod.append(a3[i] * prod[-1])
        blk_a, blk_u = prod[-1], loc[-1]
        for s in (1, 2, 4):
            keep = sub >= s
            a_prev = jnp.where(keep, pltpu.roll(blk_a, s, 0), 1.0)
            u_prev = jnp.where(keep, pltpu.roll(blk_u, s, 0), 0.0)
            blk_u = blk_u + blk_a * u_prev
            blk_a = blk_a * a_prev
        h_in = h_ref[g]
        end = blk_u + blk_a * h_in
        entry = jnp.where(sub == 0, h_in, pltpu.roll(end, 1, 0))
        h_ref[g] = jnp.broadcast_to(end[SUBLANES - 1:SUBLANES, :], (SUBLANES, LANES))
        for i in range(LRU_STEPS):
            hs_ref[g, i * SUBLANES:(i + 1) * SUBLANES, :] = loc[i] + prod[i] * entry
        rows = []
        for n in range(LRU_TILE // SUBLANES):
            j, i0 = divmod(n * SUBLANES, LRU_STEPS)
            rows.append(hs_ref[g, pl.ds(i0 * SUBLANES + j, SUBLANES, stride=SUBLANES), :])
        hs = jnp.concatenate(rows, axis=0)
        o_ref[:, sl] = (hs * gy_ref[:, sl].astype(F32)).astype(BF16)


def _lru(xr, gy, conv_w, conv_b, w2, ba, bx, lam):
    bsz, seq, w = gy.shape
    groups = w // LANES
    tiles = seq // LRU_TILE
    tile = pl.BlockSpec((None, LRU_TILE, w), lambda b, t: (b, t, 0))
    return pl.pallas_call(
        _lru_kernel,
        grid=(bsz, tiles),
        in_specs=[pl.BlockSpec((groups, LRU_TILE, LANES), lambda b, t: (0, b * tiles + t, 0)),
                  tile,
                  _resident(conv_w.shape), _resident(conv_b.shape), _resident(w2.shape),
                  _resident(ba.shape), _resident(bx.shape), _resident(lam.shape)],
        out_specs=tile,
        out_shape=jax.ShapeDtypeStruct((bsz, seq, w), BF16),
        scratch_shapes=[pltpu.VMEM((groups, CONV_WIDTH - 1, SUBLANES, LANES), F32),
                        pltpu.VMEM((groups, SUBLANES, LANES), F32),
                        pltpu.VMEM((groups, LRU_TILE, LANES), F32)],
        compiler_params=_params("arbitrary", "arbitrary"),
        name="rglru",
    )(xr, gy, conv_w, conv_b, w2, ba, bx, lam)


def _pair_block_diag(wa, wx):
    def pairs(w):
        z = jnp.zeros_like(w[0::2])
        top = jnp.concatenate([w[0::2], z], axis=2)
        bot = jnp.concatenate([z, w[1::2]], axis=2)
        return jnp.concatenate([top, bot], axis=1)
    return jnp.concatenate([pairs(wa), pairs(wx)], axis=2)


def _mixout_kernel(x_ref, mod_ref, gpost_ref, att_ref, rec_ref, sa_ref, sr_ref,
                   wao_ref, wro_ref, wout_ref, o_ref):
    _, _, gate = _mod_rows(mod_ref, 1)
    att = _dot(att_ref[...], wao_ref[...])
    rec = _dot(rec_ref[...], wro_ref[...])
    merged = sa_ref[...].astype(F32) * att + sr_ref[...].astype(F32) * rec
    y = _dot(merged.astype(BF16), wout_ref[...])
    o_ref[...] = x_ref[...] + gate * _rms(y, gpost_ref[1:2, :])


def _mixout(x2, mod, norm_post, att, rec, sa, sr, w_att_o, w_rec_o, w_out, *, seq):
    t, d = x2.shape
    tiles_per_seq = seq // TOKEN_TILE
    row = lambda n: pl.BlockSpec((TOKEN_TILE, n), lambda i: (i, 0))
    return pl.pallas_call(
        _mixout_kernel,
        grid=(t // TOKEN_TILE,),
        in_specs=[row(d),
                  pl.BlockSpec((None, 3 * N_SUB, d), lambda i: (i // tiles_per_seq, 0, 0)),
                  _resident(norm_post.shape),
                  row(ATT_WIDTH), row(LRU_WIDTH), row(d), row(d),
                  _resident(w_att_o.shape), _resident(w_rec_o.shape), _resident(w_out.shape)],
        out_specs=row(d),
        out_shape=jax.ShapeDtypeStruct((t, d), F32),
        compiler_params=_params("arbitrary"),
        name="mixer_out",
    )(x2, mod, norm_post, att, rec, sa, sr, w_att_o, w_rec_o, w_out)


def _layer(x2, c, l, bsz, seq, w_ada, b_ada, norm_pre, norm_post, ffn1_w_gu, ffn1_w_down, w_in,
           rel_bias, conv_w, conv_b, lru_wa, lru_ba, lru_wx, lru_bx, lru_lambda, w_att_o,
           w_rec_o, w_out, ffn2_w_gu, ffn2_w_down):
    bf = lambda w: w[l].astype(BF16)
    mod = _ada(c, w_ada[l], b_ada[l]).reshape(bsz, 3 * N_SUB, D_MODEL)
    npre, npost = norm_pre[l], norm_post[l]

    x2 = _ffn(x2, mod, npre, npost, bf(ffn1_w_gu), bf(ffn1_w_down), sub=0, res_w=0.5, seq=seq)

    q, k, v, xr, gy, sa, sr = _proj(x2, mod, npre, bf(w_in), seq=seq)
    split = lambda a: a.reshape(bsz, seq, a.shape[-1])
    att = _attention(q, k, v, _bias_table(rel_bias[l]))
    w2 = (0.5 * _pair_block_diag(lru_wa[l], lru_wx[l])).astype(BF16)
    vec = lambda p: p[l].reshape(1, LRU_WIDTH)
    rec = _lru(xr, split(gy), conv_w[l], vec(conv_b), w2, 0.5 * vec(lru_ba), 0.5 * vec(lru_bx),
               vec(lru_lambda))
    flat = lambda a: a.reshape(bsz * seq, a.shape[-1])
    x2 = _mixout(x2, mod, npost, flat(att), flat(rec), sa, sr,
                 bf(w_att_o), bf(w_rec_o), bf(w_out), seq=seq)

    return _ffn(x2, mod, npre, npost, bf(ffn2_w_gu), bf(ffn2_w_down), sub=2, res_w=0.5, seq=seq)


def kernel(x, c, w_ada, b_ada, norm_pre, norm_post, ffn1_w_gu, ffn1_w_down, w_in, rel_bias, conv_w, conv_b, lru_wa, lru_ba, lru_wx, lru_bx, lru_lambda, w_att_o, w_rec_o, w_out, ffn2_w_gu, ffn2_w_down):
    bsz, seq, d = x.shape
    assert d == D_MODEL and all(seq % n == 0 for n in (TOKEN_TILE, FFN_TOKEN_TILE, Q_TILE, LRU_TILE))
    x2 = x.reshape(bsz * seq, d)
    for l in range(w_ada.shape[0]):
        x2 = _layer(x2, c, l, bsz, seq, w_ada, b_ada, norm_pre, norm_post, ffn1_w_gu,
                    ffn1_w_down, w_in, rel_bias, conv_w, conv_b, lru_wa, lru_ba, lru_wx, lru_bx,
                    lru_lambda, w_att_o, w_rec_o, w_out, ffn2_w_gu, ffn2_w_down)
    return x2.reshape(bsz, seq, d)
```

```python
import functools
import math

import jax
import jax.numpy as jnp
from jax import lax
from jax.experimental import pallas as pl
from jax.experimental.pallas import tpu as pltpu

D_MODEL = 1024
CHUNK = 64
LEFT_CHUNKS = 8
ATT_HEADS = 8
ATT_HEAD_DIM = 64
ATT_WIDTH = ATT_HEADS * ATT_HEAD_DIM
MAX_REL = 128
LRU_WIDTH = D_MODEL
LRU_BLOCKS = 16
LRU_BLOCK = LRU_WIDTH // LRU_BLOCKS
CONV_WIDTH = 4
LRU_C = 8.0
D_FF = 2816
N_SUB = 3
EPS = 1e-6

LANES = 128
SUBLANES = 8
VMEM_LIMIT_BYTES = 56 * 1024 * 1024

TOKEN_TILE = 512
FFN_TOKEN_TILE = 1024
FF_TILE = 256
PROJ_CHUNK = 256
ADA_TILE = 1536
Q_TILE = 4 * CHUNK
K_BLOCKS = LEFT_CHUNKS * CHUNK // Q_TILE + 1
BAND = K_BLOCKS * Q_TILE
LRU_TILE = 256
LRU_STEPS = LRU_TILE // SUBLANES
MASK_VALUE = -1e30
BF16 = jnp.bfloat16
F32 = jnp.float32


def _dot(a, b):
    return jnp.dot(a, b, preferred_element_type=F32)


def _rms(x, g):
    return x * lax.rsqrt(jnp.mean(x * x, axis=-1, keepdims=True) + EPS) * g


def _sigmoid(x):
    return 1.0 / (1.0 + jnp.exp(-x))


def _gelu_tanh(x):
    c = math.sqrt(2.0 / math.pi)
    return x * (0.5 * (1.0 + jnp.tanh(c * (x + 0.044715 * (x * x * x)))))


def _mod_rows(mod_ref, sub):
    shift = mod_ref[3 * sub:3 * sub + 1, :]
    scale = mod_ref[3 * sub + 1:3 * sub + 2, :]
    gate = mod_ref[3 * sub + 2:3 * sub + 3, :]
    return shift, scale, gate


def _resident(shape):
    nd = len(shape)
    return pl.BlockSpec(shape, lambda *_: (0,) * nd, pipeline_mode=pl.Buffered(1))


def _params(*semantics):
    return pltpu.CompilerParams(dimension_semantics=semantics,
                                vmem_limit_bytes=VMEM_LIMIT_BYTES)


def _ada_kernel(c_ref, w_ref, b_ref, o_ref):
    c = c_ref[...]
    c_act = (c * _sigmoid(c)).astype(BF16)
    o_ref[...] = _dot(c_act, w_ref[...].astype(BF16)) + b_ref[...]


def _ada(c, w, b):
    bsz, d = c.shape
    n = w.shape[1]
    return pl.pallas_call(
        _ada_kernel,
        grid=(n // ADA_TILE,),
        in_specs=[pl.BlockSpec((bsz, d), lambda j: (0, 0)),
                  pl.BlockSpec((d, ADA_TILE), lambda j: (0, j)),
                  pl.BlockSpec((1, ADA_TILE), lambda j: (0, j))],
        out_specs=pl.BlockSpec((bsz, ADA_TILE), lambda j: (0, j)),
        out_shape=jax.ShapeDtypeStruct((bsz, n), F32),
        compiler_params=_params("arbitrary"),
        name="adaln",
    )(c, w, b.reshape(1, n))


def _ffn_kernel(x_ref, mod_ref, gpre_ref, gpost_ref, wgu_ref, wdn_ref, o_ref, act_ref,
                *, sub, res_w):
    x = x_ref[...]
    shift, scale, gate = _mod_rows(mod_ref, sub)
    h = (_rms(x, gpre_ref[sub:sub + 1, :]) * (1.0 + scale) + shift).astype(BF16)
    for j in range(D_FF // FF_TILE):
        lo = j * FF_TILE
        g = _dot(h, wgu_ref[:, lo:lo + FF_TILE])
        u = _dot(h, wgu_ref[:, D_FF + lo:D_FF + lo + FF_TILE])
        act_ref[:, lo:lo + FF_TILE] = (g * _sigmoid(g) * u).astype(BF16)
    y = _dot(act_ref[...], wdn_ref[...])
    o_ref[...] = x + (res_w * gate) * _rms(y, gpost_ref[sub:sub + 1, :])


def _ffn(x2, mod, norm_pre, norm_post, w_gu, w_down, *, sub, res_w, seq):
    t, d = x2.shape
    tiles_per_seq = seq // FFN_TOKEN_TILE
    return pl.pallas_call(
        functools.partial(_ffn_kernel, sub=sub, res_w=res_w),
        grid=(t // FFN_TOKEN_TILE,),
        in_specs=[pl.BlockSpec((FFN_TOKEN_TILE, d), lambda i: (i, 0)),
                  pl.BlockSpec((None, 3 * N_SUB, d), lambda i: (i // tiles_per_seq, 0, 0)),
                  _resident(norm_pre.shape),
                  _resident(norm_post.shape),
                  _resident(w_gu.shape),
                  _resident(w_down.shape)],
        out_specs=pl.BlockSpec((FFN_TOKEN_TILE, d), lambda i: (i, 0)),
        out_shape=jax.ShapeDtypeStruct((t, d), F32),
        scratch_shapes=[pltpu.VMEM((FFN_TOKEN_TILE, D_FF), BF16)],
        compiler_params=_params("arbitrary"),
        name=f"ffn{sub}",
    )(x2, mod, norm_pre, norm_post, w_gu, w_down)


def _lru_unit(g, xr_ref, row0, gy_ref, o_ref, fresh, cw_ref, cb_ref, w2_ref, ba_ref, bx_ref, decay,
              tail_ref, h_ref, hs_ref):
    sub = lax.broadcasted_iota(jnp.int32, (SUBLANES, LANES), 0)
    taps = CONV_WIDTH - 1
    sl = slice(g * LANES, (g + 1) * LANES)
    x3 = xr_ref[g, row0:row0 + LRU_TILE, :].reshape(LRU_STEPS, SUBLANES, LANES)
    cur_tail = x3[LRU_STEPS - taps:]
    lead = pltpu.roll(jnp.where(sub == SUBLANES - 1, tail_ref[g] * fresh, cur_tail), 1, 1)
    tail_ref[g] = cur_tail
    xext = jnp.concatenate([lead, x3], axis=0)
    xc3 = cb_ref[:, sl] + cw_ref[0:1, sl] * xext[0:LRU_STEPS]
    for w in range(1, CONV_WIDTH):
        xc3 = xc3 + cw_ref[w:w + 1, sl] * xext[w:w + LRU_STEPS]
    xc = xc3.reshape(LRU_TILE, LANES)
    z = _dot(xc.astype(BF16), w2_ref[g])
    ta = jnp.tanh(z[:, :LANES] + ba_ref[:, sl])
    tx = jnp.tanh(z[:, LANES:] + bx_ref[:, sl])
    log_a = decay[:, sl] * (ta + 1.0)
    a = jnp.exp(log_a)
    th = jnp.tanh(log_a)
    q = (-0.5 * th) / (1.0 - th)
    half_mult = jnp.where(q > 0.0, q * lax.rsqrt(q), 0.0)
    u = half_mult * ((tx + 1.0) * xc)
    a3 = a.reshape(LRU_STEPS, SUBLANES, LANES)
    u3 = u.reshape(LRU_STEPS, SUBLANES, LANES)
    loc = [u3[0]]
    prod = [a3[0]]
    for i in range(1, LRU_STEPS):
        loc.append(a3[i] * loc[-1] + u3[i])
        prod.append(a3[i] * prod[-1])
    blk_a, blk_u = prod[-1], loc[-1]
    for s in (1, 2, 4):
        keep = sub >= s
        a_prev = jnp.where(keep, pltpu.roll(blk_a, s, 0), 1.0)
        u_prev = jnp.where(keep, pltpu.roll(blk_u, s, 0), 0.0)
        blk_u = blk_u + blk_a * u_prev
        blk_a = blk_a * a_prev
    h_in = h_ref[g] * fresh
    end = blk_u + blk_a * h_in
    entry = jnp.where(sub == 0, h_in, pltpu.roll(end, 1, 0))
    h_ref[g] = jnp.broadcast_to(end[SUBLANES - 1:SUBLANES, :], (SUBLANES, LANES))
    for i in range(LRU_STEPS):
        hs_ref[g, i * SUBLANES:(i + 1) * SUBLANES, :] = loc[i] + prod[i] * entry
    rows = []
    for n in range(LRU_TILE // SUBLANES):
        j, i0 = divmod(n * SUBLANES, LRU_STEPS)
        rows.append(hs_ref[g, pl.ds(i0 * SUBLANES + j, SUBLANES, stride=SUBLANES), :])
    hs = jnp.concatenate(rows, axis=0)
    o_ref[row0:row0 + LRU_TILE, sl] = (hs * gy_ref[row0:row0 + LRU_TILE, sl]).astype(BF16)


def _proj_kernel(x_ref, mod_ref, gpre_ref, w_ref, cw_ref, cb_ref, w2_ref, ba_ref, bx_ref, lam_ref,
                 q_ref, k_ref, v_ref, rec_ref, sa_ref, sr_ref,
                 xr_s, gy_s, tail_ref, h_ref, hs_ref, *, tiles_per_seq):
    i = pl.program_id(0)

    @pl.when(i == 0)
    def _():
        for ref in (xr_s, gy_s, tail_ref, h_ref):
            ref[...] = jnp.zeros_like(ref)

    x = x_ref[...]
    shift, scale, _ = _mod_rows(mod_ref, 1)
    h = (_rms(x, gpre_ref[1:2, :]) * (1.0 + scale) + shift).astype(BF16)
    rows = x.shape[0]
    cw = PROJ_CHUNK
    per_chunk = cw // LANES

    def proj(col):
        return _dot(h, w_ref[:, col:col + cw])

    def q_chunk(c):
        q = (proj(c * cw) * (ATT_HEAD_DIM ** -0.5)).astype(BF16)
        for p in range(per_chunk):
            q_ref[c * per_chunk + p] = q[:, p * LANES:(p + 1) * LANES]

    def k_chunk(c):
        k = proj(ATT_WIDTH + c * cw).astype(BF16)
        for p in range(per_chunk):
            k_ref[c * per_chunk + p] = k[:, p * LANES:(p + 1) * LANES]

    def v_chunk(c):
        heads = cw // ATT_HEAD_DIM
        v_t = proj(2 * ATT_WIDTH + c * cw).T.astype(BF16)
        v_ref[c * heads:(c + 1) * heads] = v_t.reshape(heads, ATT_HEAD_DIM, rows)

    lru0 = 3 * ATT_WIDTH

    def gate_chunk(ref, col0, c):
        ref[:, c * cw:(c + 1) * cw] = _sigmoid(proj(col0 + c * cw)).astype(BF16)

    def gy_chunk(c):
        gy_s[:, c * cw:(c + 1) * cw] = _gelu_tanh(proj(lru0 + LRU_WIDTH + c * cw))

    def xr_chunk(c):
        xr = proj(lru0 + c * cw)
        for src_row in range(0, rows, LRU_STEPS):
            tile0, j = src_row // LRU_TILE * LRU_TILE, src_row % LRU_TILE // LRU_STEPS
            for p in range(per_chunk):
                xr_s[c * per_chunk + p, pl.ds(tile0 + j, LRU_STEPS, stride=SUBLANES), :] = (
                    xr[src_row:src_row + LRU_STEPS, p * LANES:(p + 1) * LANES])

    neg_lam = -lam_ref[...]
    softplus = jnp.maximum(neg_lam, 0.0) + jnp.log1p(jnp.exp(-jnp.abs(neg_lam)))
    decay = (-0.5 * LRU_C) * softplus
    first = jnp.where(lax.rem(i - 1, tiles_per_seq) == 0, 0.0, 1.0)

    def lru_unit(row0, g):
        _lru_unit(g, xr_s, row0, gy_s, rec_ref, first if row0 == 0 else 1.0, cw_ref, cb_ref,
                  w2_ref, ba_ref, bx_ref, decay, tail_ref, h_ref, hs_ref)

    n_att, n_lru = ATT_WIDTH // cw, LRU_WIDTH // cw
    chunks = ([functools.partial(q_chunk, c) for c in range(n_att)]
              + [functools.partial(k_chunk, c) for c in range(n_att)]
              + [functools.partial(v_chunk, c) for c in range(n_att)]
              + [functools.partial(gate_chunk, sa_ref, lru0 + 2 * LRU_WIDTH, c) for c in range(n_lru)]
              + [functools.partial(gate_chunk, sr_ref, lru0 + 3 * LRU_WIDTH, c) for c in range(n_lru)]
              + [functools.partial(gy_chunk, c) for c in range(n_lru)])
    units = [functools.partial(lru_unit, row0, g)
             for row0 in range(0, rows, LRU_TILE) for g in range(LRU_WIDTH // LANES)]
    tail = [functools.partial(xr_chunk, c) for c in range(n_lru)]
    for n in range(max(len(chunks), len(units))):
        if n < len(chunks):
            chunks[n]()
        if n < len(units):
            units[n]()
    for fn in tail:
        fn()


def _proj(x2, mod, norm_pre, w_in, conv_w, conv_b, w2, ba, bx, lam, *, seq):
    t, d = x2.shape
    tiles_per_seq = seq // TOKEN_TILE
    n_tiles = t // TOKEN_TILE
    cur = lambda i: jnp.minimum(i, n_tiles - 1)
    row = lambda n: pl.BlockSpec((TOKEN_TILE, n), lambda i: (cur(i), 0))
    tok = lambda n, dt: jax.ShapeDtypeStruct((t, n), dt)
    bsz, pairs, groups = t // seq, ATT_WIDTH // LANES, LRU_WIDTH // LANES
    qk_spec = pl.BlockSpec((None, pairs, TOKEN_TILE, LANES),
                           lambda i: (cur(i) // tiles_per_seq, 0, cur(i) % tiles_per_seq, 0))
    qk_shape = jax.ShapeDtypeStruct((bsz, pairs, seq, LANES), BF16)
    v_spec = pl.BlockSpec((None, ATT_HEADS, ATT_HEAD_DIM, TOKEN_TILE),
                          lambda i: (cur(i) // tiles_per_seq, 0, 0, cur(i) % tiles_per_seq))
    v_shape = jax.ShapeDtypeStruct((bsz, ATT_HEADS, ATT_HEAD_DIM, seq), BF16)
    rec_spec = pl.BlockSpec((TOKEN_TILE, LRU_WIDTH), lambda i: (jnp.maximum(i - 1, 0), 0))
    return pl.pallas_call(
        functools.partial(_proj_kernel, tiles_per_seq=tiles_per_seq),
        grid=(n_tiles + 1,),
        in_specs=[row(d),
                  pl.BlockSpec((None, 3 * N_SUB, d), lambda i: (cur(i) // tiles_per_seq, 0, 0)),
                  _resident(norm_pre.shape), _resident(w_in.shape),
                  _resident(conv_w.shape), _resident(conv_b.shape), _resident(w2.shape),
                  _resident(ba.shape), _resident(bx.shape), _resident(lam.shape)],
        out_specs=[qk_spec, qk_spec, v_spec, rec_spec, row(D_MODEL), row(D_MODEL)],
        out_shape=[qk_shape, qk_shape, v_shape, tok(LRU_WIDTH, BF16),
                   tok(D_MODEL, BF16), tok(D_MODEL, BF16)],
        scratch_shapes=[pltpu.VMEM((groups, TOKEN_TILE, LANES), F32),
                        pltpu.VMEM((TOKEN_TILE, LRU_WIDTH), F32),
                        pltpu.VMEM((groups, CONV_WIDTH - 1, SUBLANES, LANES), F32),
                        pltpu.VMEM((groups, SUBLANES, LANES), F32),
                        pltpu.VMEM((groups, LRU_TILE, LANES), F32)],
        compiler_params=_params("arbitrary"),
        name="mixer_proj",
    )(x2, mod, norm_pre, w_in, conv_w, conv_b, w2, ba, bx, lam)


REL_PAD = 384
TOEPLITZ = 1024


def _bias_kernel(tab_ref, o_ref):
    tab = tab_ref[...]
    hi = tab.astype(BF16)
    r1 = tab - hi.astype(F32)
    mid = r1.astype(BF16)
    lo = (r1 - mid.astype(F32)).astype(BF16)
    d_idx = lax.broadcasted_iota(jnp.int32, (REL_PAD, TOEPLITZ), 0)
    m_idx = lax.broadcasted_iota(jnp.int32, (REL_PAD, TOEPLITZ), 1)
    rel = jnp.clip(m_idx - (Q_TILE - 1), -MAX_REL, MAX_REL) + MAX_REL
    onehot = jnp.where(d_idx == rel, 1.0, 0.0).astype(BF16)
    profile = (_dot(lo, onehot) + _dot(mid, onehot)) + _dot(hi, onehot)

    kk = lax.broadcasted_iota(jnp.int32, (BAND, Q_TILE), 0)
    r = lax.broadcasted_iota(jnp.int32, (BAND, Q_TILE), 1)
    qc = r // CHUNK
    kc = kk // CHUNK
    band = jnp.where((kc >= qc) & (kc <= qc + LEFT_CHUNKS), 0.0, MASK_VALUE)
    for h in range(ATT_HEADS):
        rows = jnp.broadcast_to(profile[h:h + 1, :], (BAND, TOEPLITZ))
        skew = pltpu.roll(rows, TOEPLITZ - (BAND - 1), 1, stride=1, stride_axis=0)
        o_ref[h] = skew[:, :Q_TILE] + band


def _bias_table(rel_bias):
    tab = jnp.pad(rel_bias, ((0, 0), (0, REL_PAD - rel_bias.shape[1])))
    return pl.pallas_call(
        _bias_kernel,
        out_shape=jax.ShapeDtypeStruct((ATT_HEADS, BAND, Q_TILE), F32),
        compiler_params=pltpu.CompilerParams(vmem_limit_bytes=VMEM_LIMIT_BYTES),
        name="rel_bias",
    )(tab)


def _attn_kernel(q_ref, k0_ref, k1_ref, k2_ref, v0_ref, v1_ref, v2_ref, bias_ref, o_ref,
                 acc_ref, s0_ref, s1_ref, p0_ref, p1_ref):
    t = pl.program_id(1)
    k_refs = (k0_ref, k1_ref, k2_ref)
    v_refs = (v0_ref, v1_ref, v2_ref)
    pens = [jnp.where(t + j < K_BLOCKS - 1, MASK_VALUE, 0.0) for j in range(K_BLOCKS - 1)] + [0.0]
    lane_head = lax.broadcasted_iota(jnp.int32, (1, LANES), 1) // ATT_HEAD_DIM
    zero = jnp.zeros((), BF16)
    n_half = Q_TILE // LANES
    live = [(2 * c * CHUNK, (2 * c + 2 + LEFT_CHUNKS) * CHUNK) for c in range(n_half)]

    s_refs = (s0_ref, s1_ref)
    p_refs = (p0_ref, p1_ref)

    @pl.when((pl.program_id(0) == 0) & (t == 0))
    def _():
        for p_ref in p_refs:
            p_ref[...] = jnp.zeros_like(p_ref)

    def scores(h, slot):
        pair = h // 2
        qh = jnp.where(lane_head == h % 2, q_ref[pair], zero)
        kcat = jnp.concatenate([r[pair] for r in k_refs], axis=0)
        s = lax.dot_general(kcat, qh, (((1,), (1,)), ((), ())), preferred_element_type=F32)
        s_refs[slot][...] = s + bias_ref[h]

    def col_reduce(parts, op):
        while len(parts) > 1:
            parts = [op(parts[i], parts[i + 1]) if i + 1 < len(parts) else parts[i]
                     for i in range(0, len(parts), 2)]
        return parts[0]

    def softmax(slot):
        s_ref, p_ref = s_refs[slot], p_refs[slot]
        groups = CHUNK // SUBLANES
        inv = []
        for c, (r0, r1) in enumerate(live):
            cl = slice(c * LANES, (c + 1) * LANES)
            blocks = range(r0, r1, CHUNK)
            part = [s_ref[r:r + CHUNK, cl].reshape(groups, SUBLANES, LANES).max(axis=0)
                    + pens[r // Q_TILE] for r in blocks]
            m = jnp.max(col_reduce(part, jnp.maximum), axis=0, keepdims=True)
            shifted = [m - pen for pen in pens]
            sums = []
            for r in blocks:
                pr = jnp.exp(s_ref[r:r + CHUNK, cl] - shifted[r // Q_TILE])
                sums.append(pr.reshape(groups, SUBLANES, LANES).sum(axis=0))
                p_ref[r:r + CHUNK, cl] = pr.astype(BF16)
            l = jnp.sum(col_reduce(sums, jnp.add), axis=0, keepdims=True)
            inv.append(1.0 / l)
        return jnp.concatenate(inv, axis=1)

    def weighted_values(h, slot, inv_l):
        v_t = jnp.concatenate([r[h] for r in v_refs], axis=1)
        acc_ref[h] = _dot(v_t, p_refs[slot][...]) * inv_l

    scores(0, 0)
    inv_prev = None
    for h in range(ATT_HEADS):
        if h + 1 < ATT_HEADS:
            scores(h + 1, (h + 1) % 2)
        inv_l = softmax(h % 2)
        if h > 0:
            weighted_values(h - 1, (h - 1) % 2, inv_prev)
        inv_prev = inv_l
    weighted_values(ATT_HEADS - 1, (ATT_HEADS - 1) % 2, inv_prev)
    for p in range(ATT_HEADS // 2):
        pair_t = acc_ref[2 * p:2 * p + 2].reshape(LANES, Q_TILE)
        o_ref[:, p * LANES:(p + 1) * LANES] = pair_t.T.astype(BF16)


def _attention(q, k, v_t, bias):
    bsz, pairs, seq, _ = q.shape
    qk = lambda n: pl.BlockSpec((None, pairs, Q_TILE, LANES),
                                lambda b, t: (b, 0, jnp.maximum(t - n, 0), 0))
    val = lambda n: pl.BlockSpec((None, ATT_HEADS, ATT_HEAD_DIM, Q_TILE),
                                 lambda b, t: (b, 0, 0, jnp.maximum(t - n, 0)))
    return pl.pallas_call(
        _attn_kernel,
        grid=(bsz, seq // Q_TILE),
        in_specs=[qk(0), qk(2), qk(1), qk(0), val(2), val(1), val(0), _resident(bias.shape)],
        out_specs=pl.BlockSpec((None, Q_TILE, ATT_WIDTH), lambda b, t: (b, t, 0)),
        out_shape=jax.ShapeDtypeStruct((bsz, seq, ATT_WIDTH), BF16),
        scratch_shapes=[pltpu.VMEM((ATT_HEADS, ATT_HEAD_DIM, Q_TILE), F32),
                        pltpu.VMEM((BAND, Q_TILE), F32), pltpu.VMEM((BAND, Q_TILE), F32),
                        pltpu.VMEM((BAND, Q_TILE), BF16), pltpu.VMEM((BAND, Q_TILE), BF16)],
        compiler_params=_params("arbitrary", "arbitrary"),
        name="chunk_attn",
    )(q, k, k, k, v_t, v_t, v_t, bias)


def _pair_block_diag(wa, wx):
    def pairs(w):
        z = jnp.zeros_like(w[0::2])
        top = jnp.concatenate([w[0::2], z], axis=2)
        bot = jnp.concatenate([z, w[1::2]], axis=2)
        return jnp.concatenate([top, bot], axis=1)
    return jnp.concatenate([pairs(wa), pairs(wx)], axis=2)


def _mixout_kernel(x_ref, mod_ref, gpost_ref, att_ref, rec_ref, sa_ref, sr_ref,
                   wao_ref, wro_ref, wout_ref, o_ref):
    _, _, gate = _mod_rows(mod_ref, 1)
    att = _dot(att_ref[...], wao_ref[...])
    rec = _dot(rec_ref[...], wro_ref[...])
    merged = sa_ref[...].astype(F32) * att + sr_ref[...].astype(F32) * rec
    y = _dot(merged.astype(BF16), wout_ref[...])
    o_ref[...] = x_ref[...] + gate * _rms(y, gpost_ref[1:2, :])


def _mixout(x2, mod, norm_post, att, rec, sa, sr, w_att_o, w_rec_o, w_out, *, seq):
    t, d = x2.shape
    tiles_per_seq = seq // TOKEN_TILE
    row = lambda n: pl.BlockSpec((TOKEN_TILE, n), lambda i: (i, 0))
    return pl.pallas_call(
        _mixout_kernel,
        grid=(t // TOKEN_TILE,),
        in_specs=[row(d),
                  pl.BlockSpec((None, 3 * N_SUB, d), lambda i: (i // tiles_per_seq, 0, 0)),
                  _resident(norm_post.shape),
                  row(ATT_WIDTH), row(LRU_WIDTH), row(d), row(d),
                  _resident(w_att_o.shape), _resident(w_rec_o.shape), _resident(w_out.shape)],
        out_specs=row(d),
        out_shape=jax.ShapeDtypeStruct((t, d), F32),
        compiler_params=_params("arbitrary"),
        name="mixer_out",
    )(x2, mod, norm_post, att, rec, sa, sr, w_att_o, w_rec_o, w_out)


def _layer(x2, c, l, bsz, seq, w_ada, b_ada, norm_pre, norm_post, ffn1_w_gu, ffn1_w_down, w_in,
           rel_bias, conv_w, conv_b, lru_wa, lru_ba, lru_wx, lru_bx, lru_lambda, w_att_o,
           w_rec_o, w_out, ffn2_w_gu, ffn2_w_down):
    bf = lambda w: w[l].astype(BF16)
    mod = _ada(c, w_ada[l], b_ada[l]).reshape(bsz, 3 * N_SUB, D_MODEL)
    npre, npost = norm_pre[l], norm_post[l]

    x2 = _ffn(x2, mod, npre, npost, bf(ffn1_w_gu), bf(ffn1_w_down), sub=0, res_w=0.5, seq=seq)

    w2 = (0.5 * _pair_block_diag(lru_wa[l], lru_wx[l])).astype(BF16)
    vec = lambda p: p[l].reshape(1, LRU_WIDTH)
    q, k, v, rec, sa, sr = _proj(x2, mod, npre, bf(w_in), conv_w[l], vec(conv_b), w2,
                                 0.5 * vec(lru_ba), 0.5 * vec(lru_bx), vec(lru_lambda), seq=seq)
    att = _attention(q, k, v, _bias_table(rel_bias[l]))
    x2 = _mixout(x2, mod, npost, att.reshape(bsz * seq, ATT_WIDTH), rec, sa, sr,
                 bf(w_att_o), bf(w_rec_o), bf(w_out), seq=seq)

    return _ffn(x2, mod, npre, npost, bf(ffn2_w_gu), bf(ffn2_w_down), sub=2, res_w=0.5, seq=seq)


def kernel(x, c, w_ada, b_ada, norm_pre, norm_post, ffn1_w_gu, ffn1_w_down, w_in, rel_bias, conv_w, conv_b, lru_wa, lru_ba, lru_wx, lru_bx, lru_lambda, w_att_o, w_rec_o, w_out, ffn2_w_gu, ffn2_w_down):
    bsz, seq, d = x.shape
    assert d == D_MODEL and all(seq % n == 0 for n in (TOKEN_TILE, FFN_TOKEN_TILE, Q_TILE))
    assert TOKEN_TILE % LRU_TILE == 0
    x2 = x.reshape(bsz * seq, d)
    for l in range(w_ada.shape[0]):
        x2 = _layer(x2, c, l, bsz, seq, w_ada, b_ada, norm_pre, norm_post, ffn1_w_gu,
                    ffn1_w_down, w_in, rel_bias, conv_w, conv_b, lru_wa, lru_ba, lru_wx, lru_bx,
                    lru_lambda, w_att_o, w_rec_o, w_out, ffn2_w_gu, ffn2_w_down)
    return x2.reshape(bsz, seq, d)
```

```python
import functools
import math

import jax
import jax.numpy as jnp
from jax import lax
from jax.experimental import pallas as pl
from jax.experimental.pallas import tpu as pltpu

D_MODEL = 1024
CHUNK = 64
LEFT_CHUNKS = 8
ATT_HEADS = 8
ATT_HEAD_DIM = 64
ATT_WIDTH = ATT_HEADS * ATT_HEAD_DIM
MAX_REL = 128
LRU_WIDTH = D_MODEL
LRU_BLOCKS = 16
LRU_BLOCK = LRU_WIDTH // LRU_BLOCKS
CONV_WIDTH = 4
LRU_C = 8.0
D_FF = 2816
N_SUB = 3
EPS = 1e-6

LANES = 128
SUBLANES = 8
BF16_ROWS = 16
VMEM_LIMIT_BYTES = 56 * 1024 * 1024

TOKEN_TILE = 512
FFN_TOKEN_TILE = 1024
FF_TILE = 256
PROJ_CHUNK = 256
ADA_TILE = 1536
Q_TILE = 4 * CHUNK
K_BLOCKS = LEFT_CHUNKS * CHUNK // Q_TILE + 1
BAND = K_BLOCKS * Q_TILE
LRU_TILE = 256
LRU_STEPS = LRU_TILE // SUBLANES
MASK_VALUE = -1e30
BF16 = jnp.bfloat16
F32 = jnp.float32


def _dot(a, b):
    return jnp.dot(a, b, preferred_element_type=F32)


def _rms(x, g):
    return x * lax.rsqrt(jnp.mean(x * x, axis=-1, keepdims=True) + EPS) * g


def _sigmoid(x):
    return 1.0 / (1.0 + jnp.exp(-x))


def _gelu_tanh(x):
    c = math.sqrt(2.0 / math.pi)
    return x * (0.5 * (1.0 + jnp.tanh(c * (x + 0.044715 * (x * x * x)))))


def _mod_rows(mod_ref, sub):
    shift = mod_ref[3 * sub:3 * sub + 1, :]
    scale = mod_ref[3 * sub + 1:3 * sub + 2, :]
    gate = mod_ref[3 * sub + 2:3 * sub + 3, :]
    return shift, scale, gate


def _resident(shape):
    nd = len(shape)
    return pl.BlockSpec(shape, lambda *_: (0,) * nd, pipeline_mode=pl.Buffered(1))


def _params(*semantics):
    return pltpu.CompilerParams(dimension_semantics=semantics,
                                vmem_limit_bytes=VMEM_LIMIT_BYTES)


def _slab_view(w, n_steps):
    for cols in (w.shape[-1], D_FF * 2, D_MODEL):
        rows = w.size // cols
        if w.size % cols == 0 and cols % LANES == 0 and rows % (BF16_ROWS * n_steps) == 0:
            return w.reshape(rows, cols)
    raise ValueError(f"no slab view for {w.shape} over {n_steps} steps")


def _cast_streams(weights, n_steps):
    views = [_slab_view(w, n_steps) for w in weights]
    specs = [pl.BlockSpec((v.shape[0] // n_steps, v.shape[1]), lambda i: (i, 0)) for v in views]
    shapes = [jax.ShapeDtypeStruct(v.shape, BF16) for v in views]
    return views, specs, shapes


def _cast_slabs(in_refs, out_refs):
    for src, dst in zip(in_refs, out_refs):
        dst[...] = src[...].astype(BF16)


def _ada_kernel(c_ref, w_ref, b_ref, o_ref):
    c = c_ref[...]
    c_act = (c * _sigmoid(c)).astype(BF16)
    o_ref[...] = _dot(c_act, w_ref[...].astype(BF16)) + b_ref[...]


def _ada(c, w, b):
    bsz, d = c.shape
    n = w.shape[1]
    return pl.pallas_call(
        _ada_kernel,
        grid=(n // ADA_TILE,),
        in_specs=[pl.BlockSpec((bsz, d), lambda j: (0, 0)),
                  pl.BlockSpec((d, ADA_TILE), lambda j: (0, j)),
                  pl.BlockSpec((1, ADA_TILE), lambda j: (0, j))],
        out_specs=pl.BlockSpec((bsz, ADA_TILE), lambda j: (0, j)),
        out_shape=jax.ShapeDtypeStruct((bsz, n), F32),
        compiler_params=_params("arbitrary"),
        name="adaln",
    )(c, w, b.reshape(1, n))


def _ffn_kernel(x_ref, mod_ref, gpre_ref, gpost_ref, wgu_ref, wdn_ref, *rest, sub, res_w):
    n_cast = (len(rest) - 2) // 2
    o_ref, act_ref = rest[n_cast], rest[-1]
    _cast_slabs(rest[:n_cast], rest[n_cast + 1:-1])
    x = x_ref[...]
    shift, scale, gate = _mod_rows(mod_ref, sub)
    h = (_rms(x, gpre_ref[sub:sub + 1, :]) * (1.0 + scale) + shift).astype(BF16)
    for j in range(D_FF // FF_TILE):
        lo = j * FF_TILE
        g = _dot(h, wgu_ref[:, lo:lo + FF_TILE])
        u = _dot(h, wgu_ref[:, D_FF + lo:D_FF + lo + FF_TILE])
        act_ref[:, lo:lo + FF_TILE] = (g * _sigmoid(g) * u).astype(BF16)
    y = _dot(act_ref[...], wdn_ref[...])
    o_ref[...] = x + (res_w * gate) * _rms(y, gpost_ref[sub:sub + 1, :])


def _ffn(x2, mod, norm_pre, norm_post, w_gu, w_down, *, sub, res_w, seq, casts=()):
    t, d = x2.shape
    tiles_per_seq = seq // FFN_TOKEN_TILE
    n_steps = t // FFN_TOKEN_TILE
    views, cast_specs, cast_shapes = _cast_streams(casts, n_steps)
    out, *cast_out = pl.pallas_call(
        functools.partial(_ffn_kernel, sub=sub, res_w=res_w),
        grid=(n_steps,),
        in_specs=[pl.BlockSpec((FFN_TOKEN_TILE, d), lambda i: (i, 0)),
                  pl.BlockSpec((None, 3 * N_SUB, d), lambda i: (i // tiles_per_seq, 0, 0)),
                  _resident(norm_pre.shape),
                  _resident(norm_post.shape),
                  _resident(w_gu.shape),
                  _resident(w_down.shape)] + cast_specs,
        out_specs=[pl.BlockSpec((FFN_TOKEN_TILE, d), lambda i: (i, 0))] + cast_specs,
        out_shape=[jax.ShapeDtypeStruct((t, d), F32)] + cast_shapes,
        scratch_shapes=[pltpu.VMEM((FFN_TOKEN_TILE, D_FF), BF16)],
        compiler_params=_params("arbitrary"),
        name=f"ffn{sub}",
    )(x2, mod, norm_pre, norm_post, w_gu, w_down, *views)
    return out, [c.reshape(w.shape) for c, w in zip(cast_out, casts)]


def _lru_unit(g, xr_ref, row0, gy_ref, o_ref, fresh, cw_ref, cb_ref, w2_ref, ba_ref, bx_ref, decay,
              tail_ref, h_ref, hs_ref):
    sub = lax.broadcasted_iota(jnp.int32, (SUBLANES, LANES), 0)
    taps = CONV_WIDTH - 1
    sl = slice(g * LANES, (g + 1) * LANES)
    x3 = xr_ref[g, row0:row0 + LRU_TILE, :].reshape(LRU_STEPS, SUBLANES, LANES)
    cur_tail = x3[LRU_STEPS - taps:]
    lead = pltpu.roll(jnp.where(sub == SUBLANES - 1, tail_ref[g] * fresh, cur_tail), 1, 1)
    tail_ref[g] = cur_tail
    xext = jnp.concatenate([lead, x3], axis=0)
    xc3 = cb_ref[:, sl] + cw_ref[0:1, sl] * xext[0:LRU_STEPS]
    for w in range(1, CONV_WIDTH):
        xc3 = xc3 + cw_ref[w:w + 1, sl] * xext[w:w + LRU_STEPS]
    xc = xc3.reshape(LRU_TILE, LANES)
    z = _dot(xc.astype(BF16), w2_ref[g])
    ta = jnp.tanh(z[:, :LANES] + ba_ref[:, sl])
    tx = jnp.tanh(z[:, LANES:] + bx_ref[:, sl])
    log_a = decay[:, sl] * (ta + 1.0)
    a = jnp.exp(log_a)
    th = jnp.tanh(log_a)
    q = (-0.5 * th) / (1.0 - th)
    half_mult = jnp.where(q > 0.0, q * lax.rsqrt(q), 0.0)
    u = half_mult * ((tx + 1.0) * xc)
    a3 = a.reshape(LRU_STEPS, SUBLANES, LANES)
    u3 = u.reshape(LRU_STEPS, SUBLANES, LANES)
    loc = [u3[0]]
    prod = [a3[0]]
    for i in range(1, LRU_STEPS):
        loc.append(a3[i] * loc[-1] + u3[i])
        prod.append(a3[i] * prod[-1])
    blk_a, blk_u = prod[-1], loc[-1]
    for s in (1, 2, 4):
        keep = sub >= s
        a_prev = jnp.where(keep, pltpu.roll(blk_a, s, 0), 1.0)
        u_prev = jnp.where(keep, pltpu.roll(blk_u, s, 0), 0.0)
        blk_u = blk_u + blk_a * u_prev
        blk_a = blk_a * a_prev
    h_in = h_ref[g] * fresh
    end = blk_u + blk_a * h_in
    entry = jnp.where(sub == 0, h_in, pltpu.roll(end, 1, 0))
    h_ref[g] = jnp.broadcast_to(end[SUBLANES - 1:SUBLANES, :], (SUBLANES, LANES))
    for i in range(LRU_STEPS):
        hs_ref[g, i * SUBLANES:(i + 1) * SUBLANES, :] = loc[i] + prod[i] * entry
    rows = []
    for n in range(LRU_TILE // SUBLANES):
        j, i0 = divmod(n * SUBLANES, LRU_STEPS)
        rows.append(hs_ref[g, pl.ds(i0 * SUBLANES + j, SUBLANES, stride=SUBLANES), :])
    hs = jnp.concatenate(rows, axis=0)
    o_ref[row0:row0 + LRU_TILE, sl] = (hs * gy_ref[row0:row0 + LRU_TILE, sl]).astype(BF16)


def _proj_kernel(x_ref, mod_ref, gpre_ref, w_ref, cw_ref, cb_ref, w2_ref, ba_ref, bx_ref, lam_ref,
                 q_ref, k_ref, v_ref, rec_ref, sa_ref, sr_ref,
                 xr_s, gy_s, tail_ref, h_ref, hs_ref, *, tiles_per_seq):
    i = pl.program_id(0)

    @pl.when(i == 0)
    def _():
        for ref in (xr_s, gy_s, tail_ref, h_ref):
            ref[...] = jnp.zeros_like(ref)

    x = x_ref[...]
    shift, scale, _ = _mod_rows(mod_ref, 1)
    h = (_rms(x, gpre_ref[1:2, :]) * (1.0 + scale) + shift).astype(BF16)
    rows = x.shape[0]
    cw = PROJ_CHUNK
    per_chunk = cw // LANES

    def proj(col):
        return _dot(h, w_ref[:, col:col + cw])

    def q_chunk(c):
        q = (proj(c * cw) * (ATT_HEAD_DIM ** -0.5)).astype(BF16)
        for p in range(per_chunk):
            q_ref[c * per_chunk + p] = q[:, p * LANES:(p + 1) * LANES]

    def k_chunk(c):
        k = proj(ATT_WIDTH + c * cw).astype(BF16)
        for p in range(per_chunk):
            k_ref[c * per_chunk + p] = k[:, p * LANES:(p + 1) * LANES]

    def v_chunk(c):
        heads = cw // ATT_HEAD_DIM
        v_t = proj(2 * ATT_WIDTH + c * cw).T.astype(BF16)
        v_ref[c * heads:(c + 1) * heads] = v_t.reshape(heads, ATT_HEAD_DIM, rows)

    lru0 = 3 * ATT_WIDTH

    def gate_chunk(ref, col0, c):
        ref[:, c * cw:(c + 1) * cw] = _sigmoid(proj(col0 + c * cw)).astype(BF16)

    def gy_chunk(c):
        gy_s[:, c * cw:(c + 1) * cw] = _gelu_tanh(proj(lru0 + LRU_WIDTH + c * cw))

    def xr_chunk(c):
        xr = proj(lru0 + c * cw)
        for src_row in range(0, rows, LRU_STEPS):
            tile0, j = src_row // LRU_TILE * LRU_TILE, src_row % LRU_TILE // LRU_STEPS
            for p in range(per_chunk):
                xr_s[c * per_chunk + p, pl.ds(tile0 + j, LRU_STEPS, stride=SUBLANES), :] = (
                    xr[src_row:src_row + LRU_STEPS, p * LANES:(p + 1) * LANES])

    neg_lam = -lam_ref[...]
    softplus = jnp.maximum(neg_lam, 0.0) + jnp.log1p(jnp.exp(-jnp.abs(neg_lam)))
    decay = (-0.5 * LRU_C) * softplus
    first = jnp.where(lax.rem(i - 1, tiles_per_seq) == 0, 0.0, 1.0)

    def lru_unit(row0, g):
        _lru_unit(g, xr_s, row0, gy_s, rec_ref, first if row0 == 0 else 1.0, cw_ref, cb_ref,
                  w2_ref, ba_ref, bx_ref, decay, tail_ref, h_ref, hs_ref)

    n_att, n_lru = ATT_WIDTH // cw, LRU_WIDTH // cw
    chunks = ([functools.partial(q_chunk, c) for c in range(n_att)]
              + [functools.partial(k_chunk, c) for c in range(n_att)]
              + [functools.partial(v_chunk, c) for c in range(n_att)]
              + [functools.partial(gate_chunk, sa_ref, lru0 + 2 * LRU_WIDTH, c) for c in range(n_lru)]
              + [functools.partial(gate_chunk, sr_ref, lru0 + 3 * LRU_WIDTH, c) for c in range(n_lru)]
              + [functools.partial(gy_chunk, c) for c in range(n_lru)])
    units = [functools.partial(lru_unit, row0, g)
             for row0 in range(0, rows, LRU_TILE) for g in range(LRU_WIDTH // LANES)]
    tail = [functools.partial(xr_chunk, c) for c in range(n_lru)]
    for n in range(max(len(chunks), len(units))):
        if n < len(chunks):
            chunks[n]()
        if n < len(units):
            units[n]()
    for fn in tail:
        fn()


def _proj(x2, mod, norm_pre, w_in, conv_w, conv_b, w2, ba, bx, lam, *, seq):
    t, d = x2.shape
    tiles_per_seq = seq // TOKEN_TILE
    n_tiles = t // TOKEN_TILE
    cur = lambda i: jnp.minimum(i, n_tiles - 1)
    row = lambda n: pl.BlockSpec((TOKEN_TILE, n), lambda i: (cur(i), 0))
    tok = lambda n, dt: jax.ShapeDtypeStruct((t, n), dt)
    bsz, pairs, groups = t // seq, ATT_WIDTH // LANES, LRU_WIDTH // LANES
    qk_spec = pl.BlockSpec((None, pairs, TOKEN_TILE, LANES),
                           lambda i: (cur(i) // tiles_per_seq, 0, cur(i) % tiles_per_seq, 0))
    qk_shape = jax.ShapeDtypeStruct((bsz, pairs, seq, LANES), BF16)
    v_spec = pl.BlockSpec((None, ATT_HEADS, ATT_HEAD_DIM, TOKEN_TILE),
                          lambda i: (cur(i) // tiles_per_seq, 0, 0, cur(i) % tiles_per_seq))
    v_shape = jax.ShapeDtypeStruct((bsz, ATT_HEADS, ATT_HEAD_DIM, seq), BF16)
    rec_spec = pl.BlockSpec((TOKEN_TILE, LRU_WIDTH), lambda i: (jnp.maximum(i - 1, 0), 0))
    return pl.pallas_call(
        functools.partial(_proj_kernel, tiles_per_seq=tiles_per_seq),
        grid=(n_tiles + 1,),
        in_specs=[row(d),
                  pl.BlockSpec((None, 3 * N_SUB, d), lambda i: (cur(i) // tiles_per_seq, 0, 0)),
                  _resident(norm_pre.shape), _resident(w_in.shape),
                  _resident(conv_w.shape), _resident(conv_b.shape), _resident(w2.shape),
                  _resident(ba.shape), _resident(bx.shape), _resident(lam.shape)],
        out_specs=[qk_spec, qk_spec, v_spec, rec_spec, row(D_MODEL), row(D_MODEL)],
        out_shape=[qk_shape, qk_shape, v_shape, tok(LRU_WIDTH, BF16),
                   tok(D_MODEL, BF16), tok(D_MODEL, BF16)],
        scratch_shapes=[pltpu.VMEM((groups, TOKEN_TILE, LANES), F32),
                        pltpu.VMEM((TOKEN_TILE, LRU_WIDTH), F32),
                        pltpu.VMEM((groups, CONV_WIDTH - 1, SUBLANES, LANES), F32),
                        pltpu.VMEM((groups, SUBLANES, LANES), F32),
                        pltpu.VMEM((groups, LRU_TILE, LANES), F32)],
        compiler_params=_params("arbitrary"),
        name="mixer_proj",
    )(x2, mod, norm_pre, w_in, conv_w, conv_b, w2, ba, bx, lam)


REL_PAD = 384
TOEPLITZ = 1024


def _bias_kernel(tab_ref, o_ref):
    tab = tab_ref[...]
    hi = tab.astype(BF16)
    r1 = tab - hi.astype(F32)
    mid = r1.astype(BF16)
    lo = (r1 - mid.astype(F32)).astype(BF16)
    d_idx = lax.broadcasted_iota(jnp.int32, (REL_PAD, TOEPLITZ), 0)
    m_idx = lax.broadcasted_iota(jnp.int32, (REL_PAD, TOEPLITZ), 1)
    rel = jnp.clip(m_idx - (Q_TILE - 1), -MAX_REL, MAX_REL) + MAX_REL
    onehot = jnp.where(d_idx == rel, 1.0, 0.0).astype(BF16)
    profile = (_dot(lo, onehot) + _dot(mid, onehot)) + _dot(hi, onehot)

    kk = lax.broadcasted_iota(jnp.int32, (BAND, Q_TILE), 0)
    r = lax.broadcasted_iota(jnp.int32, (BAND, Q_TILE), 1)
    qc = r // CHUNK
    kc = kk // CHUNK
    band = jnp.where((kc >= qc) & (kc <= qc + LEFT_CHUNKS), 0.0, MASK_VALUE)
    for h in range(ATT_HEADS):
        rows = jnp.broadcast_to(profile[h:h + 1, :], (BAND, TOEPLITZ))
        skew = pltpu.roll(rows, TOEPLITZ - (BAND - 1), 1, stride=1, stride_axis=0)
        o_ref[h] = skew[:, :Q_TILE] + band


def _bias_table(rel_bias):
    tab = jnp.pad(rel_bias, ((0, 0), (0, REL_PAD - rel_bias.shape[1])))
    return pl.pallas_call(
        _bias_kernel,
        out_shape=jax.ShapeDtypeStruct((ATT_HEADS, BAND, Q_TILE), F32),
        compiler_params=pltpu.CompilerParams(vmem_limit_bytes=VMEM_LIMIT_BYTES),
        name="rel_bias",
    )(tab)


def _attn_kernel(q_ref, k0_ref, k1_ref, k2_ref, v0_ref, v1_ref, v2_ref, bias_ref, o_ref,
                 acc_ref, s0_ref, s1_ref, p0_ref, p1_ref):
    t = pl.program_id(1)
    k_refs = (k0_ref, k1_ref, k2_ref)
    v_refs = (v0_ref, v1_ref, v2_ref)
    pens = [jnp.where(t + j < K_BLOCKS - 1, MASK_VALUE, 0.0) for j in range(K_BLOCKS - 1)] + [0.0]
    lane_head = lax.broadcasted_iota(jnp.int32, (1, LANES), 1) // ATT_HEAD_DIM
    zero = jnp.zeros((), BF16)
    n_half = Q_TILE // LANES
    live = [(2 * c * CHUNK, (2 * c + 2 + LEFT_CHUNKS) * CHUNK) for c in range(n_half)]

    s_refs = (s0_ref, s1_ref)
    p_refs = (p0_ref, p1_ref)

    @pl.when((pl.program_id(0) == 0) & (t == 0))
    def _():
        for p_ref in p_refs:
            p_ref[...] = jnp.zeros_like(p_ref)

    def scores(h, slot):
        pair = h // 2
        qh = jnp.where(lane_head == h % 2, q_ref[pair], zero)
        kcat = jnp.concatenate([r[pair] for r in k_refs], axis=0)
        s = lax.dot_general(kcat, qh, (((1,), (1,)), ((), ())), preferred_element_type=F32)
        s_refs[slot][...] = s + bias_ref[h]

    def col_reduce(parts, op):
        while len(parts) > 1:
            parts = [op(parts[i], parts[i + 1]) if i + 1 < len(parts) else parts[i]
                     for i in range(0, len(parts), 2)]
        return parts[0]

    def softmax(slot):
        s_ref, p_ref = s_refs[slot], p_refs[slot]
        groups = CHUNK // SUBLANES
        inv = []
        for c, (r0, r1) in enumerate(live):
            cl = slice(c * LANES, (c + 1) * LANES)
            blocks = range(r0, r1, CHUNK)
            part = [s_ref[r:r + CHUNK, cl].reshape(groups, SUBLANES, LANES).max(axis=0)
                    + pens[r // Q_TILE] for r in blocks]
            m = jnp.max(col_reduce(part, jnp.maximum), axis=0, keepdims=True)
            shifted = [m - pen for pen in pens]
            sums = []
            for r in blocks:
                pr = jnp.exp(s_ref[r:r + CHUNK, cl] - shifted[r // Q_TILE])
                sums.append(pr.reshape(groups, SUBLANES, LANES).sum(axis=0))
                p_ref[r:r + CHUNK, cl] = pr.astype(BF16)
            l = jnp.sum(col_reduce(sums, jnp.add), axis=0, keepdims=True)
            inv.append(1.0 / l)
        return jnp.concatenate(inv, axis=1)

    def weighted_values(h, slot, inv_l):
        v_t = jnp.concatenate([r[h] for r in v_refs], axis=1)
        acc_ref[h] = _dot(v_t, p_refs[slot][...]) * inv_l

    scores(0, 0)
    inv_prev = None
    for h in range(ATT_HEADS):
        if h + 1 < ATT_HEADS:
            scores(h + 1, (h + 1) % 2)
        inv_l = softmax(h % 2)
        if h > 0:
            weighted_values(h - 1, (h - 1) % 2, inv_prev)
        inv_prev = inv_l
    weighted_values(ATT_HEADS - 1, (ATT_HEADS - 1) % 2, inv_prev)
    for p in range(ATT_HEADS // 2):
        pair_t = acc_ref[2 * p:2 * p + 2].reshape(LANES, Q_TILE)
        o_ref[:, p * LANES:(p + 1) * LANES] = pair_t.T.astype(BF16)


def _attention(q, k, v_t, bias):
    bsz, pairs, seq, _ = q.shape
    qk = lambda n: pl.BlockSpec((None, pairs, Q_TILE, LANES),
                                lambda b, t: (b, 0, jnp.maximum(t - n, 0), 0))
    val = lambda n: pl.BlockSpec((None, ATT_HEADS, ATT_HEAD_DIM, Q_TILE),
                                 lambda b, t: (b, 0, 0, jnp.maximum(t - n, 0)))
    return pl.pallas_call(
        _attn_kernel,
        grid=(bsz, seq // Q_TILE),
        in_specs=[qk(0), qk(2), qk(1), qk(0), val(2), val(1), val(0), _resident(bias.shape)],
        out_specs=pl.BlockSpec((None, Q_TILE, ATT_WIDTH), lambda b, t: (b, t, 0)),
        out_shape=jax.ShapeDtypeStruct((bsz, seq, ATT_WIDTH), BF16),
        scratch_shapes=[pltpu.VMEM((ATT_HEADS, ATT_HEAD_DIM, Q_TILE), F32),
                        pltpu.VMEM((BAND, Q_TILE), F32), pltpu.VMEM((BAND, Q_TILE), F32),
                        pltpu.VMEM((BAND, Q_TILE), BF16), pltpu.VMEM((BAND, Q_TILE), BF16)],
        compiler_params=_params("arbitrary", "arbitrary"),
        name="chunk_attn",
    )(q, k, k, k, v_t, v_t, v_t, bias)


def _pair_block_diag(wa, wx):
    def pairs(w):
        z = jnp.zeros_like(w[0::2])
        top = jnp.concatenate([w[0::2], z], axis=2)
        bot = jnp.concatenate([z, w[1::2]], axis=2)
        return jnp.concatenate([top, bot], axis=1)
    return jnp.concatenate([pairs(wa), pairs(wx)], axis=2)


def _mixout_kernel(x_ref, mod_ref, gpost_ref, att_ref, rec_ref, sa_ref, sr_ref,
                   wao_ref, wro_ref, wout_ref, *rest):
    n_cast = (len(rest) - 1) // 2
    o_ref = rest[n_cast]
    _cast_slabs(rest[:n_cast], rest[n_cast + 1:])
    _, _, gate = _mod_rows(mod_ref, 1)
    att = _dot(att_ref[...], wao_ref[...])
    rec = _dot(rec_ref[...], wro_ref[...])
    merged = sa_ref[...].astype(F32) * att + sr_ref[...].astype(F32) * rec
    y = _dot(merged.astype(BF16), wout_ref[...])
    o_ref[...] = x_ref[...] + gate * _rms(y, gpost_ref[1:2, :])


def _mixout(x2, mod, norm_post, att, rec, sa, sr, w_att_o, w_rec_o, w_out, *, seq, casts=()):
    t, d = x2.shape
    tiles_per_seq = seq // TOKEN_TILE
    n_steps = t // TOKEN_TILE
    row = lambda n: pl.BlockSpec((TOKEN_TILE, n), lambda i: (i, 0))
    views, cast_specs, cast_shapes = _cast_streams(casts, n_steps)
    out, *cast_out = pl.pallas_call(
        _mixout_kernel,
        grid=(n_steps,),
        in_specs=[row(d),
                  pl.BlockSpec((None, 3 * N_SUB, d), lambda i: (i // tiles_per_seq, 0, 0)),
                  _resident(norm_post.shape),
                  row(ATT_WIDTH), row(LRU_WIDTH), row(d), row(d),
                  _resident(w_att_o.shape), _resident(w_rec_o.shape), _resident(w_out.shape)]
                 + cast_specs,
        out_specs=[row(d)] + cast_specs,
        out_shape=[jax.ShapeDtypeStruct((t, d), F32)] + cast_shapes,
        compiler_params=_params("arbitrary"),
        name="mixer_out",
    )(x2, mod, norm_post, att, rec, sa, sr, w_att_o, w_rec_o, w_out, *views)
    return out, [c.reshape(w.shape) for c, w in zip(cast_out, casts)]


def _layer(x2, c, l, bsz, seq, w_ada, b_ada, norm_pre, norm_post, ffn1_w_gu, ffn1_w_down, w_in,
           rel_bias, conv_w, conv_b, lru_wa, lru_ba, lru_wx, lru_bx, lru_lambda, w_att_o,
           w_rec_o, w_out, ffn2_w_gu, ffn2_w_down):
    bf = lambda w: w[l].astype(BF16)
    mod = _ada(c, w_ada[l], b_ada[l]).reshape(bsz, 3 * N_SUB, D_MODEL)
    npre, npost = norm_pre[l], norm_post[l]

    x2, (w_in_b, w_att_o_b, w_rec_o_b, w_out_b) = _ffn(
        x2, mod, npre, npost, bf(ffn1_w_gu), bf(ffn1_w_down), sub=0, res_w=0.5, seq=seq,
        casts=(w_in[l], w_att_o[l], w_rec_o[l], w_out[l]))

    w2 = (0.5 * _pair_block_diag(lru_wa[l], lru_wx[l])).astype(BF16)
    vec = lambda p: p[l].reshape(1, LRU_WIDTH)
    q, k, v, rec, sa, sr = _proj(x2, mod, npre, w_in_b, conv_w[l], vec(conv_b), w2,
                                 0.5 * vec(lru_ba), 0.5 * vec(lru_bx), vec(lru_lambda), seq=seq)
    att = _attention(q, k, v, _bias_table(rel_bias[l]))
    x2, (w_gu2_b, w_down2_b) = _mixout(
        x2, mod, npost, att.reshape(bsz * seq, ATT_WIDTH), rec, sa, sr,
        w_att_o_b, w_rec_o_b, w_out_b, seq=seq, casts=(ffn2_w_gu[l], ffn2_w_down[l]))

    return _ffn(x2, mod, npre, npost, w_gu2_b, w_down2_b, sub=2, res_w=0.5, seq=seq)[0]


def kernel(x, c, w_ada, b_ada, norm_pre, norm_post, ffn1_w_gu, ffn1_w_down, w_in, rel_bias, conv_w, conv_b, lru_wa, lru_ba, lru_wx, lru_bx, lru_lambda, w_att_o, w_rec_o, w_out, ffn2_w_gu, ffn2_w_down):
    bsz, seq, d = x.shape
    assert d == D_MODEL and all(seq % n == 0 for n in (TOKEN_TILE, FFN_TOKEN_TILE, Q_TILE))
    assert TOKEN_TILE % LRU_TILE == 0
    x2 = x.reshape(bsz * seq, d)
    for l in range(w_ada.shape[0]):
        x2 = _layer(x2, c, l, bsz, seq, w_ada, b_ada, norm_pre, norm_post, ffn1_w_gu,
                    ffn1_w_down, w_in, rel_bias, conv_w, conv_b, lru_wa, lru_ba, lru_wx, lru_bx,
                    lru_lambda, w_att_o, w_rec_o, w_out, ffn2_w_gu, ffn2_w_down)
    return x2.reshape(bsz, seq, d)
```

```python
import functools
import math

import jax
import jax.numpy as jnp
from jax import lax
from jax.experimental import pallas as pl
from jax.experimental.pallas import tpu as pltpu

D_MODEL = 1024
CHUNK = 64
LEFT_CHUNKS = 8
ATT_HEADS = 8
ATT_HEAD_DIM = 64
ATT_WIDTH = ATT_HEADS * ATT_HEAD_DIM
MAX_REL = 128
LRU_WIDTH = D_MODEL
LRU_BLOCKS = 16
LRU_BLOCK = LRU_WIDTH // LRU_BLOCKS
CONV_WIDTH = 4
LRU_C = 8.0
D_FF = 2816
N_SUB = 3
EPS = 1e-6

LANES = 128
SUBLANES = 8
BF16_ROWS = 16
VMEM_LIMIT_BYTES = 56 * 1024 * 1024

TOKEN_TILE = 512
MIX_TOKEN_TILE = 1024
FFN_TOKEN_TILE = 1024
FF_TILE = 256
PROJ_CHUNK = 256
ADA_TILE = 1536
Q_TILE = 4 * CHUNK
K_BLOCKS = LEFT_CHUNKS * CHUNK // Q_TILE + 1
BAND = K_BLOCKS * Q_TILE
LRU_TILE = 256
LRU_STEPS = LRU_TILE // SUBLANES
MASK_VALUE = -1e30
BF16 = jnp.bfloat16
F32 = jnp.float32


def _dot(a, b):
    return jnp.dot(a, b, preferred_element_type=F32)


def _rms(x, g):
    return x * lax.rsqrt(jnp.mean(x * x, axis=-1, keepdims=True) + EPS) * g


def _sigmoid(x):
    return 1.0 / (1.0 + jnp.exp(-x))


def _gelu_tanh(x):
    c = math.sqrt(2.0 / math.pi)
    return x * (0.5 * (1.0 + jnp.tanh(c * (x + 0.044715 * (x * x * x)))))


def _mod_rows(mod_ref, sub):
    shift = mod_ref[3 * sub:3 * sub + 1, :]
    scale = mod_ref[3 * sub + 1:3 * sub + 2, :]
    gate = mod_ref[3 * sub + 2:3 * sub + 3, :]
    return shift, scale, gate


def _resident(shape):
    nd = len(shape)
    return pl.BlockSpec(shape, lambda *_: (0,) * nd, pipeline_mode=pl.Buffered(1))


def _params(*semantics):
    return pltpu.CompilerParams(dimension_semantics=semantics,
                                vmem_limit_bytes=VMEM_LIMIT_BYTES)


def _cast_streams(stacked, layer, n_steps):
    for w in stacked:
        assert w.shape[1] % (BF16_ROWS * n_steps) == 0 and w.shape[2] % LANES == 0, w.shape
    in_specs = [pl.BlockSpec((None, w.shape[1] // n_steps, w.shape[2]), lambda i: (layer, i, 0))
                for w in stacked]
    out_specs = [pl.BlockSpec((w.shape[1] // n_steps, w.shape[2]), lambda i: (i, 0))
                 for w in stacked]
    shapes = [jax.ShapeDtypeStruct(w.shape[1:], BF16) for w in stacked]
    return in_specs, out_specs, shapes


def _cast_slabs(in_refs, out_refs):
    for src, dst in zip(in_refs, out_refs):
        dst[...] = src[...].astype(BF16)


def _ada_kernel(c_ref, w_ref, b_ref, o_ref):
    c = c_ref[...]
    c_act = (c * _sigmoid(c)).astype(BF16)
    o_ref[...] = _dot(c_act, w_ref[...].astype(BF16)) + b_ref[...]


def _ada(c, w, b):
    bsz, d = c.shape
    n = w.shape[1]
    return pl.pallas_call(
        _ada_kernel,
        grid=(n // ADA_TILE,),
        in_specs=[pl.BlockSpec((bsz, d), lambda j: (0, 0)),
                  pl.BlockSpec((d, ADA_TILE), lambda j: (0, j)),
                  pl.BlockSpec((1, ADA_TILE), lambda j: (0, j))],
        out_specs=pl.BlockSpec((bsz, ADA_TILE), lambda j: (0, j)),
        out_shape=jax.ShapeDtypeStruct((bsz, n), F32),
        compiler_params=_params("arbitrary"),
        name="adaln",
    )(c, w, b.reshape(1, n))


def _ffn_kernel(x_ref, mod_ref, gpre_ref, gpost_ref, wgu_ref, wdn_ref, *rest, sub, res_w):
    n_cast = (len(rest) - 2) // 2
    o_ref, act_ref = rest[n_cast], rest[-1]
    _cast_slabs(rest[:n_cast], rest[n_cast + 1:-1])
    x = x_ref[...]
    shift, scale, gate = _mod_rows(mod_ref, sub)
    h = (_rms(x, gpre_ref[sub:sub + 1, :]) * (1.0 + scale) + shift).astype(BF16)
    for j in range(D_FF // FF_TILE):
        lo = j * FF_TILE
        g = _dot(h, wgu_ref[:, lo:lo + FF_TILE])
        u = _dot(h, wgu_ref[:, D_FF + lo:D_FF + lo + FF_TILE])
        act_ref[:, lo:lo + FF_TILE] = (g * _sigmoid(g) * u).astype(BF16)
    y = _dot(act_ref[...], wdn_ref[...])
    o_ref[...] = x + (res_w * gate) * _rms(y, gpost_ref[sub:sub + 1, :])


def _ffn(x2, mod, norm_pre, norm_post, w_gu, w_down, *, sub, res_w, seq, layer, casts=()):
    t, d = x2.shape
    tiles_per_seq = seq // FFN_TOKEN_TILE
    n_steps = t // FFN_TOKEN_TILE
    cast_in, cast_specs, cast_shapes = _cast_streams(casts, layer, n_steps)
    out, *cast_out = pl.pallas_call(
        functools.partial(_ffn_kernel, sub=sub, res_w=res_w),
        grid=(n_steps,),
        in_specs=[pl.BlockSpec((FFN_TOKEN_TILE, d), lambda i: (i, 0)),
                  pl.BlockSpec((None, 3 * N_SUB, d), lambda i: (i // tiles_per_seq, 0, 0)),
                  _resident(norm_pre.shape),
                  _resident(norm_post.shape),
                  _resident(w_gu.shape),
                  _resident(w_down.shape)] + cast_in,
        out_specs=[pl.BlockSpec((FFN_TOKEN_TILE, d), lambda i: (i, 0))] + cast_specs,
        out_shape=[jax.ShapeDtypeStruct((t, d), F32)] + cast_shapes,
        scratch_shapes=[pltpu.VMEM((FFN_TOKEN_TILE, D_FF), BF16)],
        compiler_params=_params("arbitrary"),
        name=f"ffn{sub}",
    )(x2, mod, norm_pre, norm_post, w_gu, w_down, *casts)
    return out, cast_out


def _lru_unit(g, xr_ref, row0, gy_ref, o_ref, fresh, cw_ref, cb_ref, w2_ref, ba_ref, bx_ref, decay,
              tail_ref, h_ref, hs_ref):
    sub = lax.broadcasted_iota(jnp.int32, (SUBLANES, LANES), 0)
    taps = CONV_WIDTH - 1
    sl = slice(g * LANES, (g + 1) * LANES)
    x3 = xr_ref[g, row0:row0 + LRU_TILE, :].reshape(LRU_STEPS, SUBLANES, LANES)
    cur_tail = x3[LRU_STEPS - taps:]
    lead = pltpu.roll(jnp.where(sub == SUBLANES - 1, tail_ref[g] * fresh, cur_tail), 1, 1)
    tail_ref[g] = cur_tail
    xext = jnp.concatenate([lead, x3], axis=0)
    xc3 = cb_ref[:, sl] + cw_ref[0:1, sl] * xext[0:LRU_STEPS]
    for w in range(1, CONV_WIDTH):
        xc3 = xc3 + cw_ref[w:w + 1, sl] * xext[w:w + LRU_STEPS]
    xc = xc3.reshape(LRU_TILE, LANES)
    z = _dot(xc.astype(BF16), w2_ref[g])
    ta = jnp.tanh(z[:, :LANES] + ba_ref[:, sl])
    tx = jnp.tanh(z[:, LANES:] + bx_ref[:, sl])
    log_a = decay[:, sl] * (ta + 1.0)
    a = jnp.exp(log_a)
    th = jnp.tanh(log_a)
    q = (-0.5 * th) / (1.0 - th)
    half_mult = jnp.where(q > 0.0, q * lax.rsqrt(q), 0.0)
    u = half_mult * ((tx + 1.0) * xc)
    a3 = a.reshape(LRU_STEPS, SUBLANES, LANES)
    u3 = u.reshape(LRU_STEPS, SUBLANES, LANES)
    loc = [u3[0]]
    prod = [a3[0]]
    for i in range(1, LRU_STEPS):
        loc.append(a3[i] * loc[-1] + u3[i])
        prod.append(a3[i] * prod[-1])
    blk_a, blk_u = prod[-1], loc[-1]
    for s in (1, 2, 4):
        keep = sub >= s
        a_prev = jnp.where(keep, pltpu.roll(blk_a, s, 0), 1.0)
        u_prev = jnp.where(keep, pltpu.roll(blk_u, s, 0), 0.0)
        blk_u = blk_u + blk_a * u_prev
        blk_a = blk_a * a_prev
    h_in = h_ref[g] * fresh
    end = blk_u + blk_a * h_in
    entry = jnp.where(sub == 0, h_in, pltpu.roll(end, 1, 0))
    h_ref[g] = jnp.broadcast_to(end[SUBLANES - 1:SUBLANES, :], (SUBLANES, LANES))
    for i in range(LRU_STEPS):
        hs_ref[g, i * SUBLANES:(i + 1) * SUBLANES, :] = loc[i] + prod[i] * entry
    rows = []
    for n in range(LRU_TILE // SUBLANES):
        j, i0 = divmod(n * SUBLANES, LRU_STEPS)
        rows.append(hs_ref[g, pl.ds(i0 * SUBLANES + j, SUBLANES, stride=SUBLANES), :])
    hs = jnp.concatenate(rows, axis=0)
    o_ref[row0:row0 + LRU_TILE, sl] = (hs * gy_ref[row0:row0 + LRU_TILE, sl]).astype(BF16)


def _proj_kernel(x_ref, mod_ref, gpre_ref, w_ref, cw_ref, cb_ref, w2_ref, ba_ref, bx_ref, lam_ref,
                 q_ref, k_ref, v_ref, rec_ref, sa_ref, sr_ref,
                 xr_s, gy_s, tail_ref, h_ref, hs_ref, *, tiles_per_seq):
    i = pl.program_id(0)

    @pl.when(i == 0)
    def _():
        for ref in (xr_s, gy_s, tail_ref, h_ref):
            ref[...] = jnp.zeros_like(ref)

    x = x_ref[...]
    shift, scale, _ = _mod_rows(mod_ref, 1)
    h = (_rms(x, gpre_ref[1:2, :]) * (1.0 + scale) + shift).astype(BF16)
    rows = x.shape[0]
    cw = PROJ_CHUNK
    per_chunk = cw // LANES

    def proj(col):
        return _dot(h, w_ref[:, col:col + cw])

    def q_chunk(c):
        q = (proj(c * cw) * (ATT_HEAD_DIM ** -0.5)).astype(BF16)
        for p in range(per_chunk):
            q_ref[c * per_chunk + p] = q[:, p * LANES:(p + 1) * LANES]

    def k_chunk(c):
        k = proj(ATT_WIDTH + c * cw).astype(BF16)
        for p in range(per_chunk):
            k_ref[c * per_chunk + p] = k[:, p * LANES:(p + 1) * LANES]

    def v_chunk(c):
        heads = cw // ATT_HEAD_DIM
        v_t = proj(2 * ATT_WIDTH + c * cw).T.astype(BF16)
        v_ref[c * heads:(c + 1) * heads] = v_t.reshape(heads, ATT_HEAD_DIM, rows)

    lru0 = 3 * ATT_WIDTH

    def gate_chunk(ref, col0, c):
        ref[:, c * cw:(c + 1) * cw] = _sigmoid(proj(col0 + c * cw)).astype(BF16)

    def gy_chunk(c):
        gy_s[:, c * cw:(c + 1) * cw] = _gelu_tanh(proj(lru0 + LRU_WIDTH + c * cw))

    def xr_chunk(c):
        xr = proj(lru0 + c * cw)
        for src_row in range(0, rows, LRU_STEPS):
            tile0, j = src_row // LRU_TILE * LRU_TILE, src_row % LRU_TILE // LRU_STEPS
            for p in range(per_chunk):
                xr_s[c * per_chunk + p, pl.ds(tile0 + j, LRU_STEPS, stride=SUBLANES), :] = (
                    xr[src_row:src_row + LRU_STEPS, p * LANES:(p + 1) * LANES])

    neg_lam = -lam_ref[...]
    softplus = jnp.maximum(neg_lam, 0.0) + jnp.log1p(jnp.exp(-jnp.abs(neg_lam)))
    decay = (-0.5 * LRU_C) * softplus
    first = jnp.where(lax.rem(i - 1, tiles_per_seq) == 0, 0.0, 1.0)

    def lru_unit(row0, g):
        _lru_unit(g, xr_s, row0, gy_s, rec_ref, first if row0 == 0 else 1.0, cw_ref, cb_ref,
                  w2_ref, ba_ref, bx_ref, decay, tail_ref, h_ref, hs_ref)

    n_att, n_lru = ATT_WIDTH // cw, LRU_WIDTH // cw
    chunks = ([functools.partial(q_chunk, c) for c in range(n_att)]
              + [functools.partial(k_chunk, c) for c in range(n_att)]
              + [functools.partial(v_chunk, c) for c in range(n_att)]
              + [functools.partial(gate_chunk, sa_ref, lru0 + 2 * LRU_WIDTH, c) for c in range(n_lru)]
              + [functools.partial(gate_chunk, sr_ref, lru0 + 3 * LRU_WIDTH, c) for c in range(n_lru)]
              + [functools.partial(gy_chunk, c) for c in range(n_lru)])
    units = [functools.partial(lru_unit, row0, g)
             for row0 in range(0, rows, LRU_TILE) for g in range(LRU_WIDTH // LANES)]
    tail = [functools.partial(xr_chunk, c) for c in range(n_lru)]
    for n in range(max(len(chunks), len(units))):
        if n < len(chunks):
            chunks[n]()
        if n < len(units):
            units[n]()
    for fn in tail:
        fn()


def _proj(x2, mod, norm_pre, w_in, conv_w, conv_b, w2, ba, bx, lam, *, seq):
    t, d = x2.shape
    tiles_per_seq = seq // TOKEN_TILE
    n_tiles = t // TOKEN_TILE
    cur = lambda i: jnp.minimum(i, n_tiles - 1)
    row = lambda n: pl.BlockSpec((TOKEN_TILE, n), lambda i: (cur(i), 0))
    tok = lambda n, dt: jax.ShapeDtypeStruct((t, n), dt)
    bsz, pairs, groups = t // seq, ATT_WIDTH // LANES, LRU_WIDTH // LANES
    qk_spec = pl.BlockSpec((None, pairs, TOKEN_TILE, LANES),
                           lambda i: (cur(i) // tiles_per_seq, 0, cur(i) % tiles_per_seq, 0))
    qk_shape = jax.ShapeDtypeStruct((bsz, pairs, seq, LANES), BF16)
    v_spec = pl.BlockSpec((None, ATT_HEADS, ATT_HEAD_DIM, TOKEN_TILE),
                          lambda i: (cur(i) // tiles_per_seq, 0, 0, cur(i) % tiles_per_seq))
    v_shape = jax.ShapeDtypeStruct((bsz, ATT_HEADS, ATT_HEAD_DIM, seq), BF16)
    rec_spec = pl.BlockSpec((TOKEN_TILE, LRU_WIDTH), lambda i: (jnp.maximum(i - 1, 0), 0))
    return pl.pallas_call(
        functools.partial(_proj_kernel, tiles_per_seq=tiles_per_seq),
        grid=(n_tiles + 1,),
        in_specs=[row(d),
                  pl.BlockSpec((None, 3 * N_SUB, d), lambda i: (cur(i) // tiles_per_seq, 0, 0)),
                  _resident(norm_pre.shape), _resident(w_in.shape),
                  _resident(conv_w.shape), _resident(conv_b.shape), _resident(w2.shape),
                  _resident(ba.shape), _resident(bx.shape), _resident(lam.shape)],
        out_specs=[qk_spec, qk_spec, v_spec, rec_spec, row(D_MODEL), row(D_MODEL)],
        out_shape=[qk_shape, qk_shape, v_shape, tok(LRU_WIDTH, BF16),
                   tok(D_MODEL, BF16), tok(D_MODEL, BF16)],
        scratch_shapes=[pltpu.VMEM((groups, TOKEN_TILE, LANES), F32),
                        pltpu.VMEM((TOKEN_TILE, LRU_WIDTH), F32),
                        pltpu.VMEM((groups, CONV_WIDTH - 1, SUBLANES, LANES), F32),
                        pltpu.VMEM((groups, SUBLANES, LANES), F32),
                        pltpu.VMEM((groups, LRU_TILE, LANES), F32)],
        compiler_params=_params("arbitrary"),
        name="mixer_proj",
    )(x2, mod, norm_pre, w_in, conv_w, conv_b, w2, ba, bx, lam)


REL_PAD = 384
TOEPLITZ = 1024


def _bias_kernel(tab_ref, o_ref):
    tab = tab_ref[...]
    hi = tab.astype(BF16)
    r1 = tab - hi.astype(F32)
    mid = r1.astype(BF16)
    lo = (r1 - mid.astype(F32)).astype(BF16)
    d_idx = lax.broadcasted_iota(jnp.int32, (REL_PAD, TOEPLITZ), 0)
    m_idx = lax.broadcasted_iota(jnp.int32, (REL_PAD, TOEPLITZ), 1)
    rel = jnp.clip(m_idx - (Q_TILE - 1), -MAX_REL, MAX_REL) + MAX_REL
    onehot = jnp.where(d_idx == rel, 1.0, 0.0).astype(BF16)
    profile = (_dot(lo, onehot) + _dot(mid, onehot)) + _dot(hi, onehot)

    kk = lax.broadcasted_iota(jnp.int32, (BAND, Q_TILE), 0)
    r = lax.broadcasted_iota(jnp.int32, (BAND, Q_TILE), 1)
    qc = r // CHUNK
    kc = kk // CHUNK
    band = jnp.where((kc >= qc) & (kc <= qc + LEFT_CHUNKS), 0.0, MASK_VALUE)
    for h in range(ATT_HEADS):
        rows = jnp.broadcast_to(profile[h:h + 1, :], (BAND, TOEPLITZ))
        skew = pltpu.roll(rows, TOEPLITZ - (BAND - 1), 1, stride=1, stride_axis=0)
        o_ref[h] = skew[:, :Q_TILE] + band


def _bias_table(rel_bias):
    tab = jnp.pad(rel_bias, ((0, 0), (0, REL_PAD - rel_bias.shape[1])))
    return pl.pallas_call(
        _bias_kernel,
        out_shape=jax.ShapeDtypeStruct((ATT_HEADS, BAND, Q_TILE), F32),
        compiler_params=pltpu.CompilerParams(vmem_limit_bytes=VMEM_LIMIT_BYTES),
        name="rel_bias",
    )(tab)


def _attn_kernel(q_ref, k0_ref, k1_ref, k2_ref, v0_ref, v1_ref, v2_ref, bias_ref, o_ref,
                 acc_ref, s0_ref, s1_ref, p0_ref, p1_ref):
    t = pl.program_id(1)
    k_refs = (k0_ref, k1_ref, k2_ref)
    v_refs = (v0_ref, v1_ref, v2_ref)
    pens = [jnp.where(t + j < K_BLOCKS - 1, MASK_VALUE, 0.0) for j in range(K_BLOCKS - 1)] + [0.0]
    lane_head = lax.broadcasted_iota(jnp.int32, (1, LANES), 1) // ATT_HEAD_DIM
    zero = jnp.zeros((), BF16)
    n_half = Q_TILE // LANES
    live = [(2 * c * CHUNK, (2 * c + 2 + LEFT_CHUNKS) * CHUNK) for c in range(n_half)]

    s_refs = (s0_ref, s1_ref)
    p_refs = (p0_ref, p1_ref)

    @pl.when((pl.program_id(0) == 0) & (t == 0))
    def _():
        for p_ref in p_refs:
            p_ref[...] = jnp.zeros_like(p_ref)

    def scores(h, slot):
        pair = h // 2
        qh = jnp.where(lane_head == h % 2, q_ref[pair], zero)
        kcat = jnp.concatenate([r[pair] for r in k_refs], axis=0)
        s = lax.dot_general(kcat, qh, (((1,), (1,)), ((), ())), preferred_element_type=F32)
        s_refs[slot][...] = s + bias_ref[h]

    def col_reduce(parts, op):
        while len(parts) > 1:
            parts = [op(parts[i], parts[i + 1]) if i + 1 < len(parts) else parts[i]
                     for i in range(0, len(parts), 2)]
        return parts[0]

    def softmax(slot):
        s_ref, p_ref = s_refs[slot], p_refs[slot]
        groups = CHUNK // SUBLANES
        inv = []
        for c, (r0, r1) in enumerate(live):
            cl = slice(c * LANES, (c + 1) * LANES)
            blocks = range(r0, r1, CHUNK)
            part = [s_ref[r:r + CHUNK, cl].reshape(groups, SUBLANES, LANES).max(axis=0)
                    + pens[r // Q_TILE] for r in blocks]
            m = jnp.max(col_reduce(part, jnp.maximum), axis=0, keepdims=True)
            shifted = [m - pen for pen in pens]
            sums = []
            for r in blocks:
                pr = jnp.exp(s_ref[r:r + CHUNK, cl] - shifted[r // Q_TILE])
                sums.append(pr.reshape(groups, SUBLANES, LANES).sum(axis=0))
                p_ref[r:r + CHUNK, cl] = pr.astype(BF16)
            l = jnp.sum(col_reduce(sums, jnp.add), axis=0, keepdims=True)
            inv.append(1.0 / l)
        return jnp.concatenate(inv, axis=1)

    def weighted_values(h, slot, inv_l):
        v_t = jnp.concatenate([r[h] for r in v_refs], axis=1)
        acc_ref[h] = _dot(v_t, p_refs[slot][...]) * inv_l

    scores(0, 0)
    inv_prev = None
    for h in range(ATT_HEADS):
        if h + 1 < ATT_HEADS:
            scores(h + 1, (h + 1) % 2)
        inv_l = softmax(h % 2)
        if h > 0:
            weighted_values(h - 1, (h - 1) % 2, inv_prev)
        inv_prev = inv_l
    weighted_values(ATT_HEADS - 1, (ATT_HEADS - 1) % 2, inv_prev)
    for p in range(ATT_HEADS // 2):
        pair_t = acc_ref[2 * p:2 * p + 2].reshape(LANES, Q_TILE)
        o_ref[:, p * LANES:(p + 1) * LANES] = pair_t.T.astype(BF16)


def _attention(q, k, v_t, bias):
    bsz, pairs, seq, _ = q.shape
    qk = lambda n: pl.BlockSpec((None, pairs, Q_TILE, LANES),
                                lambda b, t: (b, 0, jnp.maximum(t - n, 0), 0))
    val = lambda n: pl.BlockSpec((None, ATT_HEADS, ATT_HEAD_DIM, Q_TILE),
                                 lambda b, t: (b, 0, 0, jnp.maximum(t - n, 0)))
    return pl.pallas_call(
        _attn_kernel,
        grid=(bsz, seq // Q_TILE),
        in_specs=[qk(0), qk(2), qk(1), qk(0), val(2), val(1), val(0), _resident(bias.shape)],
        out_specs=pl.BlockSpec((None, Q_TILE, ATT_WIDTH), lambda b, t: (b, t, 0)),
        out_shape=jax.ShapeDtypeStruct((bsz, seq, ATT_WIDTH), BF16),
        scratch_shapes=[pltpu.VMEM((ATT_HEADS, ATT_HEAD_DIM, Q_TILE), F32),
                        pltpu.VMEM((BAND, Q_TILE), F32), pltpu.VMEM((BAND, Q_TILE), F32),
                        pltpu.VMEM((BAND, Q_TILE), BF16), pltpu.VMEM((BAND, Q_TILE), BF16)],
        compiler_params=_params("arbitrary", "arbitrary"),
        name="chunk_attn",
    )(q, k, k, k, v_t, v_t, v_t, bias)


def _pair_block_diag(wa, wx):
    def pairs(w):
        z = jnp.zeros_like(w[0::2])
        top = jnp.concatenate([w[0::2], z], axis=2)
        bot = jnp.concatenate([z, w[1::2]], axis=2)
        return jnp.concatenate([top, bot], axis=1)
    return jnp.concatenate([pairs(wa), pairs(wx)], axis=2)


def _mixout_kernel(x_ref, mod_ref, gpost_ref, att_ref, rec_ref, sa_ref, sr_ref,
                   wao_ref, wro_ref, wout_ref, *rest):
    n_cast = (len(rest) - 1) // 2
    o_ref = rest[n_cast]
    _cast_slabs(rest[:n_cast], rest[n_cast + 1:])
    _, _, gate = _mod_rows(mod_ref, 1)
    att = _dot(att_ref[...], wao_ref[...])
    rec = _dot(rec_ref[...], wro_ref[...])
    merged = sa_ref[...].astype(F32) * att + sr_ref[...].astype(F32) * rec
    y = _dot(merged.astype(BF16), wout_ref[...])
    o_ref[...] = x_ref[...] + gate * _rms(y, gpost_ref[1:2, :])


def _mixout(x2, mod, norm_post, att, rec, sa, sr, w_att_o, w_rec_o, w_out, *, seq, layer,
            casts=()):
    t, d = x2.shape
    tiles_per_seq = seq // MIX_TOKEN_TILE
    n_steps = t // MIX_TOKEN_TILE
    row = lambda n: pl.BlockSpec((MIX_TOKEN_TILE, n), lambda i: (i, 0))
    cast_in, cast_specs, cast_shapes = _cast_streams(casts, layer, n_steps)
    out, *cast_out = pl.pallas_call(
        _mixout_kernel,
        grid=(n_steps,),
        in_specs=[row(d),
                  pl.BlockSpec((None, 3 * N_SUB, d), lambda i: (i // tiles_per_seq, 0, 0)),
                  _resident(norm_post.shape),
                  row(ATT_WIDTH), row(LRU_WIDTH), row(d), row(d),
                  _resident(w_att_o.shape), _resident(w_rec_o.shape), _resident(w_out.shape)]
                 + cast_in,
        out_specs=[row(d)] + cast_specs,
        out_shape=[jax.ShapeDtypeStruct((t, d), F32)] + cast_shapes,
        compiler_params=_params("arbitrary"),
        name="mixer_out",
    )(x2, mod, norm_post, att, rec, sa, sr, w_att_o, w_rec_o, w_out, *casts)
    return out, cast_out


def _layer(x2, c, l, bsz, seq, w_ada, b_ada, norm_pre, norm_post, ffn1_w_gu, ffn1_w_down, w_in,
           rel_bias, conv_w, conv_b, lru_wa, lru_ba, lru_wx, lru_bx, lru_lambda, w_att_o,
           w_rec_o, w_out, ffn2_w_gu, ffn2_w_down):
    bf = lambda w: w[l].astype(BF16)
    mod = _ada(c, w_ada[l], b_ada[l]).reshape(bsz, 3 * N_SUB, D_MODEL)
    npre, npost = norm_pre[l], norm_post[l]

    x2, (w_in_b, w_att_o_b, w_rec_o_b, w_out_b, w_down2_b) = _ffn(
        x2, mod, npre, npost, bf(ffn1_w_gu), bf(ffn1_w_down), sub=0, res_w=0.5, seq=seq, layer=l,
        casts=(w_in, w_att_o, w_rec_o, w_out, ffn2_w_down))

    w2 = (0.5 * _pair_block_diag(lru_wa[l], lru_wx[l])).astype(BF16)
    vec = lambda p: p[l].reshape(1, LRU_WIDTH)
    q, k, v, rec, sa, sr = _proj(x2, mod, npre, w_in_b, conv_w[l], vec(conv_b), w2,
                                 0.5 * vec(lru_ba), 0.5 * vec(lru_bx), vec(lru_lambda), seq=seq)
    att = _attention(q, k, v, _bias_table(rel_bias[l]))
    x2, (w_gu2_b,) = _mixout(
        x2, mod, npost, att.reshape(bsz * seq, ATT_WIDTH), rec, sa, sr,
        w_att_o_b, w_rec_o_b, w_out_b, seq=seq, layer=l, casts=(ffn2_w_gu,))

    return _ffn(x2, mod, npre, npost, w_gu2_b, w_down2_b, sub=2, res_w=0.5, seq=seq, layer=l)[0]


def kernel(x, c, w_ada, b_ada, norm_pre, norm_post, ffn1_w_gu, ffn1_w_down, w_in, rel_bias, conv_w, conv_b, lru_wa, lru_ba, lru_wx, lru_bx, lru_lambda, w_att_o, w_rec_o, w_out, ffn2_w_gu, ffn2_w_down):
    bsz, seq, d = x.shape
    tiles = (TOKEN_TILE, MIX_TOKEN_TILE, FFN_TOKEN_TILE, Q_TILE)
    assert d == D_MODEL and all(seq % n == 0 for n in tiles)
    assert TOKEN_TILE % LRU_TILE == 0
    x2 = x.reshape(bsz * seq, d)
    for l in range(w_ada.shape[0]):
        x2 = _layer(x2, c, l, bsz, seq, w_ada, b_ada, norm_pre, norm_post, ffn1_w_gu,
                    ffn1_w_down, w_in, rel_bias, conv_w, conv_b, lru_wa, lru_ba, lru_wx, lru_bx,
                    lru_lambda, w_att_o, w_rec_o, w_out, ffn2_w_gu, ffn2_w_down)
    return x2.reshape(bsz, seq, d)
```

```python
import functools
import math

import jax
import jax.numpy as jnp
from jax import lax
from jax.experimental import pallas as pl
from jax.experimental.pallas import tpu as pltpu

D_MODEL = 1024
CHUNK = 64
LEFT_CHUNKS = 8
ATT_HEADS = 8
ATT_HEAD_DIM = 64
ATT_WIDTH = ATT_HEADS * ATT_HEAD_DIM
MAX_REL = 128
LRU_WIDTH = D_MODEL
LRU_BLOCKS = 16
LRU_BLOCK = LRU_WIDTH // LRU_BLOCKS
CONV_WIDTH = 4
LRU_C = 8.0
D_FF = 2816
N_SUB = 3
EPS = 1e-6

LANES = 128
SUBLANES = 8
BF16_ROWS = 16
VMEM_LIMIT_BYTES = 56 * 1024 * 1024

TOKEN_TILE = 512
MIX_TOKEN_TILE = 1024
FFN_TOKEN_TILE = 1024
FFN_SUB_TILE = 256
FF_TILE = 256
PROJ_CHUNK = 256
ADA_TILE = 1536
Q_TILE = 4 * CHUNK
Q_STEP_TILES = 2
K_BLOCKS = LEFT_CHUNKS * CHUNK // Q_TILE + 1
BAND = K_BLOCKS * Q_TILE
LRU_TILE = 256
LRU_STEPS = LRU_TILE // SUBLANES
MASK_VALUE = -1e30
LOG2_E = math.log2(math.e)
BF16 = jnp.bfloat16
F32 = jnp.float32


def _dot(a, b):
    return jnp.dot(a, b, preferred_element_type=F32)


def _rms(x, g):
    return x * lax.rsqrt(jnp.mean(x * x, axis=-1, keepdims=True) + EPS) * g


def _sigmoid(x):
    return 1.0 / (1.0 + jnp.exp(-x))


def _gelu_tanh(x):
    c = math.sqrt(2.0 / math.pi)
    return x * (0.5 * (1.0 + jnp.tanh(c * (x + 0.044715 * (x * x * x)))))


def _mod_rows(mod_ref, sub):
    shift = mod_ref[3 * sub:3 * sub + 1, :]
    scale = mod_ref[3 * sub + 1:3 * sub + 2, :]
    gate = mod_ref[3 * sub + 2:3 * sub + 3, :]
    return shift, scale, gate


def _resident(shape):
    nd = len(shape)
    return pl.BlockSpec(shape, lambda *_: (0,) * nd, pipeline_mode=pl.Buffered(1))


def _params(*semantics):
    return pltpu.CompilerParams(dimension_semantics=semantics,
                                vmem_limit_bytes=VMEM_LIMIT_BYTES)


def _cast_streams(stacked, layer, n_steps):
    for w in stacked:
        assert w.shape[1] % (BF16_ROWS * n_steps) == 0 and w.shape[2] % LANES == 0, w.shape
    in_specs = [pl.BlockSpec((None, w.shape[1] // n_steps, w.shape[2]), lambda i: (layer, i, 0))
                for w in stacked]
    out_specs = [pl.BlockSpec((w.shape[1] // n_steps, w.shape[2]), lambda i: (i, 0))
                 for w in stacked]
    shapes = [jax.ShapeDtypeStruct(w.shape[1:], BF16) for w in stacked]
    return in_specs, out_specs, shapes


def _cast_slabs(in_refs, out_refs):
    for src, dst in zip(in_refs, out_refs):
        dst[...] = src[...].astype(BF16)


def _ada_kernel(c_ref, w_ref, b_ref, o_ref):
    c = c_ref[...]
    c_act = (c * _sigmoid(c)).astype(BF16)
    o_ref[...] = _dot(c_act, w_ref[...].astype(BF16)) + b_ref[...]


def _ada(c, w, b):
    bsz, d = c.shape
    n = w.shape[1]
    return pl.pallas_call(
        _ada_kernel,
        grid=(n // ADA_TILE,),
        in_specs=[pl.BlockSpec((bsz, d), lambda j: (0, 0)),
                  pl.BlockSpec((d, ADA_TILE), lambda j: (0, j)),
                  pl.BlockSpec((1, ADA_TILE), lambda j: (0, j))],
        out_specs=pl.BlockSpec((bsz, ADA_TILE), lambda j: (0, j)),
        out_shape=jax.ShapeDtypeStruct((bsz, n), F32),
        compiler_params=_params("arbitrary"),
        name="adaln",
    )(c, w, b.reshape(1, n))


def _ffn_kernel(x_ref, mod_ref, gpre_ref, gpost_ref, wgu_ref, wdn_ref, *rest, sub, res_w):
    n_cast = (len(rest) - 4) // 2
    o_ref = rest[n_cast]
    h_ref, act_ref, y_ref = rest[-3:]
    _cast_slabs(rest[:n_cast], rest[n_cast + 1:-3])
    shift, scale, gate = _mod_rows(mod_ref, sub)
    n_sub = x_ref.shape[0] // FFN_SUB_TILE

    def rows(s):
        return slice(s * FFN_SUB_TILE, (s + 1) * FFN_SUB_TILE)

    def prologue(s):
        x = x_ref[rows(s), :]
        h_ref[s % 2] = (_rms(x, gpre_ref[sub:sub + 1, :]) * (1.0 + scale) + shift).astype(BF16)

    def gate_up(s, j):
        lo = j * FF_TILE
        h = h_ref[s % 2]
        g = _dot(h, wgu_ref[:, lo:lo + FF_TILE])
        u = _dot(h, wgu_ref[:, D_FF + lo:D_FF + lo + FF_TILE])
        act_ref[s % 2, :, lo:lo + FF_TILE] = (g * _sigmoid(g) * u).astype(BF16)

    def down(s):
        y_ref[s % 2] = _dot(act_ref[s % 2], wdn_ref[...])

    def epilogue(s):
        y = _rms(y_ref[s % 2], gpost_ref[sub:sub + 1, :])
        o_ref[rows(s), :] = x_ref[rows(s), :] + (res_w * gate) * y

    n_ff = D_FF // FF_TILE
    prologue(0)
    for s in range(n_sub):
        for j in range(n_ff):
            gate_up(s, j)
            if j == 0 and s > 0:
                epilogue(s - 1)
            if j == n_ff // 2 and s + 1 < n_sub:
                prologue(s + 1)
        down(s)
    epilogue(n_sub - 1)


def _ffn(x2, mod, norm_pre, norm_post, w_gu, w_down, *, sub, res_w, seq, layer, casts=()):
    t, d = x2.shape
    tiles_per_seq = seq // FFN_TOKEN_TILE
    n_steps = t // FFN_TOKEN_TILE
    cast_in, cast_specs, cast_shapes = _cast_streams(casts, layer, n_steps)
    out, *cast_out = pl.pallas_call(
        functools.partial(_ffn_kernel, sub=sub, res_w=res_w),
        grid=(n_steps,),
        in_specs=[pl.BlockSpec((FFN_TOKEN_TILE, d), lambda i: (i, 0)),
                  pl.BlockSpec((None, 3 * N_SUB, d), lambda i: (i // tiles_per_seq, 0, 0)),
                  _resident(norm_pre.shape),
                  _resident(norm_post.shape),
                  _resident(w_gu.shape),
                  _resident(w_down.shape)] + cast_in,
        out_specs=[pl.BlockSpec((FFN_TOKEN_TILE, d), lambda i: (i, 0))] + cast_specs,
        out_shape=[jax.ShapeDtypeStruct((t, d), F32)] + cast_shapes,
        scratch_shapes=[pltpu.VMEM((2, FFN_SUB_TILE, d), BF16),
                        pltpu.VMEM((2, FFN_SUB_TILE, D_FF), BF16),
                        pltpu.VMEM((2, FFN_SUB_TILE, d), F32)],
        compiler_params=_params("arbitrary"),
        name=f"ffn{sub}",
    )(x2, mod, norm_pre, norm_post, w_gu, w_down, *casts)
    return out, cast_out


def _lru_unit(g, xr_ref, row0, gy_ref, o_ref, fresh, cw_ref, cb_ref, w2_ref, ba_ref, bx_ref, decay,
              tail_ref, h_ref, hs_ref):
    sub = lax.broadcasted_iota(jnp.int32, (SUBLANES, LANES), 0)
    taps = CONV_WIDTH - 1
    sl = slice(g * LANES, (g + 1) * LANES)
    x3 = xr_ref[g, row0:row0 + LRU_TILE, :].reshape(LRU_STEPS, SUBLANES, LANES)
    cur_tail = x3[LRU_STEPS - taps:]
    lead = pltpu.roll(jnp.where(sub == SUBLANES - 1, tail_ref[g] * fresh, cur_tail), 1, 1)
    tail_ref[g] = cur_tail
    xext = jnp.concatenate([lead, x3], axis=0)
    xc3 = cb_ref[:, sl] + cw_ref[0:1, sl] * xext[0:LRU_STEPS]
    for w in range(1, CONV_WIDTH):
        xc3 = xc3 + cw_ref[w:w + 1, sl] * xext[w:w + LRU_STEPS]
    xc = xc3.reshape(LRU_TILE, LANES)
    z = _dot(xc.astype(BF16), w2_ref[g])
    ta = jnp.tanh(z[:, :LANES] + ba_ref[:, sl])
    tx = jnp.tanh(z[:, LANES:] + bx_ref[:, sl])
    log_a = decay[:, sl] * (ta + 1.0)
    a = jnp.exp(log_a)
    th = jnp.tanh(log_a)
    q = (-0.5 * th) / (1.0 - th)
    half_mult = jnp.where(q > 0.0, q * lax.rsqrt(q), 0.0)
    u = half_mult * ((tx + 1.0) * xc)
    a3 = a.reshape(LRU_STEPS, SUBLANES, LANES)
    u3 = u.reshape(LRU_STEPS, SUBLANES, LANES)
    loc = [u3[0]]
    prod = [a3[0]]
    for i in range(1, LRU_STEPS):
        loc.append(a3[i] * loc[-1] + u3[i])
        prod.append(a3[i] * prod[-1])
    blk_a, blk_u = prod[-1], loc[-1]
    for s in (1, 2, 4):
        keep = sub >= s
        a_prev = jnp.where(keep, pltpu.roll(blk_a, s, 0), 1.0)
        u_prev = jnp.where(keep, pltpu.roll(blk_u, s, 0), 0.0)
        blk_u = blk_u + blk_a * u_prev
        blk_a = blk_a * a_prev
    h_in = h_ref[g] * fresh
    end = blk_u + blk_a * h_in
    entry = jnp.where(sub == 0, h_in, pltpu.roll(end, 1, 0))
    h_ref[g] = jnp.broadcast_to(end[SUBLANES - 1:SUBLANES, :], (SUBLANES, LANES))
    for i in range(LRU_STEPS):
        hs_ref[g, i * SUBLANES:(i + 1) * SUBLANES, :] = loc[i] + prod[i] * entry
    rows = []
    for n in range(LRU_TILE // SUBLANES):
        j, i0 = divmod(n * SUBLANES, LRU_STEPS)
        rows.append(hs_ref[g, pl.ds(i0 * SUBLANES + j, SUBLANES, stride=SUBLANES), :])
    hs = jnp.concatenate(rows, axis=0)
    o_ref[row0:row0 + LRU_TILE, sl] = (hs * gy_ref[row0:row0 + LRU_TILE, sl]).astype(BF16)


def _proj_kernel(x_ref, mod_ref, gpre_ref, w_ref, cw_ref, cb_ref, w2_ref, ba_ref, bx_ref, lam_ref,
                 q_ref, k_ref, v_ref, rec_ref, sa_ref, sr_ref,
                 xr_s, gy_s, tail_ref, h_ref, hs_ref, *, tiles_per_seq):
    i = pl.program_id(0)

    @pl.when(i == 0)
    def _():
        for ref in (xr_s, gy_s, tail_ref, h_ref):
            ref[...] = jnp.zeros_like(ref)

    x = x_ref[...]
    shift, scale, _ = _mod_rows(mod_ref, 1)
    h = (_rms(x, gpre_ref[1:2, :]) * (1.0 + scale) + shift).astype(BF16)
    rows = x.shape[0]
    cw = PROJ_CHUNK
    per_chunk = cw // LANES

    def proj(col):
        return _dot(h, w_ref[:, col:col + cw])

    def q_chunk(c):
        q = (proj(c * cw) * (ATT_HEAD_DIM ** -0.5 * LOG2_E)).astype(BF16)
        for p in range(per_chunk):
            q_ref[c * per_chunk + p] = q[:, p * LANES:(p + 1) * LANES]

    def k_chunk(c):
        k = proj(ATT_WIDTH + c * cw).astype(BF16)
        for p in range(per_chunk):
            k_ref[c * per_chunk + p] = k[:, p * LANES:(p + 1) * LANES]

    def v_chunk(c):
        heads = cw // ATT_HEAD_DIM
        v_t = proj(2 * ATT_WIDTH + c * cw).T.astype(BF16)
        v_ref[c * heads:(c + 1) * heads] = v_t.reshape(heads, ATT_HEAD_DIM, rows)

    lru0 = 3 * ATT_WIDTH

    def gate_chunk(ref, col0, c):
        ref[:, c * cw:(c + 1) * cw] = _sigmoid(proj(col0 + c * cw)).astype(BF16)

    def gy_chunk(c):
        gy_s[:, c * cw:(c + 1) * cw] = _gelu_tanh(proj(lru0 + LRU_WIDTH + c * cw))

    def xr_chunk(c):
        xr = proj(lru0 + c * cw)
        for src_row in range(0, rows, LRU_STEPS):
            tile0, j = src_row // LRU_TILE * LRU_TILE, src_row % LRU_TILE // LRU_STEPS
            for p in range(per_chunk):
                xr_s[c * per_chunk + p, pl.ds(tile0 + j, LRU_STEPS, stride=SUBLANES), :] = (
                    xr[src_row:src_row + LRU_STEPS, p * LANES:(p + 1) * LANES])

    neg_lam = -lam_ref[...]
    softplus = jnp.maximum(neg_lam, 0.0) + jnp.log1p(jnp.exp(-jnp.abs(neg_lam)))
    decay = (-0.5 * LRU_C) * softplus
    first = jnp.where(lax.rem(i - 1, tiles_per_seq) == 0, 0.0, 1.0)

    def lru_unit(row0, g):
        _lru_unit(g, xr_s, row0, gy_s, rec_ref, first if row0 == 0 else 1.0, cw_ref, cb_ref,
                  w2_ref, ba_ref, bx_ref, decay, tail_ref, h_ref, hs_ref)

    n_att, n_lru = ATT_WIDTH // cw, LRU_WIDTH // cw
    chunks = ([functools.partial(q_chunk, c) for c in range(n_att)]
              + [functools.partial(k_chunk, c) for c in range(n_att)]
              + [functools.partial(v_chunk, c) for c in range(n_att)]
              + [functools.partial(gate_chunk, sa_ref, lru0 + 2 * LRU_WIDTH, c) for c in range(n_lru)]
              + [functools.partial(gate_chunk, sr_ref, lru0 + 3 * LRU_WIDTH, c) for c in range(n_lru)]
              + [functools.partial(gy_chunk, c) for c in range(n_lru)])
    units = [functools.partial(lru_unit, row0, g)
             for row0 in range(0, rows, LRU_TILE) for g in range(LRU_WIDTH // LANES)]
    tail = [functools.partial(xr_chunk, c) for c in range(n_lru)]
    for n in range(max(len(chunks), len(units))):
        if n < len(chunks):
            chunks[n]()
        if n < len(units):
            units[n]()
    for fn in tail:
        fn()


def _proj(x2, mod, norm_pre, w_in, conv_w, conv_b, w2, ba, bx, lam, *, seq):
    t, d = x2.shape
    tiles_per_seq = seq // TOKEN_TILE
    n_tiles = t // TOKEN_TILE
    cur = lambda i: jnp.minimum(i, n_tiles - 1)
    row = lambda n: pl.BlockSpec((TOKEN_TILE, n), lambda i: (cur(i), 0))
    tok = lambda n, dt: jax.ShapeDtypeStruct((t, n), dt)
    bsz, pairs, groups = t // seq, ATT_WIDTH // LANES, LRU_WIDTH // LANES
    qk_spec = pl.BlockSpec((None, pairs, TOKEN_TILE, LANES),
                           lambda i: (cur(i) // tiles_per_seq, 0, cur(i) % tiles_per_seq, 0))
    qk_shape = jax.ShapeDtypeStruct((bsz, pairs, seq, LANES), BF16)
    v_spec = pl.BlockSpec((None, ATT_HEADS, ATT_HEAD_DIM, TOKEN_TILE),
                          lambda i: (cur(i) // tiles_per_seq, 0, 0, cur(i) % tiles_per_seq))
    v_shape = jax.ShapeDtypeStruct((bsz, ATT_HEADS, ATT_HEAD_DIM, seq), BF16)
    rec_spec = pl.BlockSpec((TOKEN_TILE, LRU_WIDTH), lambda i: (jnp.maximum(i - 1, 0), 0))
    return pl.pallas_call(
        functools.partial(_proj_kernel, tiles_per_seq=tiles_per_seq),
        grid=(n_tiles + 1,),
        in_specs=[row(d),
                  pl.BlockSpec((None, 3 * N_SUB, d), lambda i: (cur(i) // tiles_per_seq, 0, 0)),
                  _resident(norm_pre.shape), _resident(w_in.shape),
                  _resident(conv_w.shape), _resident(conv_b.shape), _resident(w2.shape),
                  _resident(ba.shape), _resident(bx.shape), _resident(lam.shape)],
        out_specs=[qk_spec, qk_spec, v_spec, rec_spec, row(D_MODEL), row(D_MODEL)],
        out_shape=[qk_shape, qk_shape, v_shape, tok(LRU_WIDTH, BF16),
                   tok(D_MODEL, BF16), tok(D_MODEL, BF16)],
        scratch_shapes=[pltpu.VMEM((groups, TOKEN_TILE, LANES), F32),
                        pltpu.VMEM((TOKEN_TILE, LRU_WIDTH), F32),
                        pltpu.VMEM((groups, CONV_WIDTH - 1, SUBLANES, LANES), F32),
                        pltpu.VMEM((groups, SUBLANES, LANES), F32),
                        pltpu.VMEM((groups, LRU_TILE, LANES), F32)],
        compiler_params=_params("arbitrary"),
        name="mixer_proj",
    )(x2, mod, norm_pre, w_in, conv_w, conv_b, w2, ba, bx, lam)


REL_PAD = 384
TOEPLITZ = 1024


def _bias_kernel(tab_ref, o_ref):
    tab = tab_ref[...]
    hi = tab.astype(BF16)
    r1 = tab - hi.astype(F32)
    mid = r1.astype(BF16)
    lo = (r1 - mid.astype(F32)).astype(BF16)
    d_idx = lax.broadcasted_iota(jnp.int32, (REL_PAD, TOEPLITZ), 0)
    m_idx = lax.broadcasted_iota(jnp.int32, (REL_PAD, TOEPLITZ), 1)
    rel = jnp.clip(m_idx - (Q_TILE - 1), -MAX_REL, MAX_REL) + MAX_REL
    onehot = jnp.where(d_idx == rel, 1.0, 0.0).astype(BF16)
    profile = ((_dot(lo, onehot) + _dot(mid, onehot)) + _dot(hi, onehot)) * LOG2_E

    kk = lax.broadcasted_iota(jnp.int32, (BAND, Q_TILE), 0)
    r = lax.broadcasted_iota(jnp.int32, (BAND, Q_TILE), 1)
    qc = r // CHUNK
    kc = kk // CHUNK
    band = jnp.where((kc >= qc) & (kc <= qc + LEFT_CHUNKS), 0.0, MASK_VALUE)
    for h in range(ATT_HEADS):
        rows = jnp.broadcast_to(profile[h:h + 1, :], (BAND, TOEPLITZ))
        skew = pltpu.roll(rows, TOEPLITZ - (BAND - 1), 1, stride=1, stride_axis=0)
        o_ref[h] = skew[:, :Q_TILE] + band


def _bias_table(rel_bias):
    tab = jnp.pad(rel_bias, ((0, 0), (0, REL_PAD - rel_bias.shape[1])))
    return pl.pallas_call(
        _bias_kernel,
        out_shape=jax.ShapeDtypeStruct((ATT_HEADS, BAND, Q_TILE), F32),
        compiler_params=pltpu.CompilerParams(vmem_limit_bytes=VMEM_LIMIT_BYTES),
        name="rel_bias",
    )(tab)


def _attn_kernel(q_ref, *refs):
    n_kv = K_BLOCKS - 1 + Q_STEP_TILES
    k_refs, v_refs = refs[:n_kv], refs[n_kv:2 * n_kv]
    bias_ref, o_ref, acc_ref = refs[2 * n_kv:2 * n_kv + 3]
    s_refs, p_refs = refs[-4:-2], refs[-2:]
    first_tile = pl.program_id(1) * Q_STEP_TILES
    lane_head = lax.broadcasted_iota(jnp.int32, (1, LANES), 1) // ATT_HEAD_DIM
    zero = jnp.zeros((), BF16)
    live = [(2 * c * CHUNK, (2 * c + 2 + LEFT_CHUNKS) * CHUNK) for c in range(Q_TILE // LANES)]

    @pl.when((pl.program_id(0) == 0) & (pl.program_id(1) == 0))
    def _():
        for p_ref in p_refs:
            p_ref[...] = jnp.zeros_like(p_ref)

    def start_pens(u):
        return [jnp.where(first_tile + u + j < K_BLOCKS - 1, MASK_VALUE, 0.0)
                for j in range(K_BLOCKS - 1)] + [0.0]

    def scores(u, h, slot):
        pair = h // 2
        qh = jnp.where(lane_head == h % 2, q_ref[pair, u * Q_TILE:(u + 1) * Q_TILE, :], zero)
        kcat = jnp.concatenate([r[pair] for r in k_refs[u:u + K_BLOCKS]], axis=0)
        s = lax.dot_general(kcat, qh, (((1,), (1,)), ((), ())), preferred_element_type=F32)
        s_refs[slot][...] = s + bias_ref[h]

    def col_reduce(parts, op):
        while len(parts) > 1:
            parts = [op(parts[i], parts[i + 1]) if i + 1 < len(parts) else parts[i]
                     for i in range(0, len(parts), 2)]
        return parts[0]

    def softmax(u, slot):
        s_ref, p_ref = s_refs[slot], p_refs[slot]
        pens = start_pens(u)
        groups = CHUNK // SUBLANES
        inv = []
        for c, (r0, r1) in enumerate(live):
            cl = slice(c * LANES, (c + 1) * LANES)
            blocks = range(r0, r1, CHUNK)
            part = [s_ref[r:r + CHUNK, cl].reshape(groups, SUBLANES, LANES).max(axis=0)
                    + pens[r // Q_TILE] for r in blocks]
            m = jnp.max(col_reduce(part, jnp.maximum), axis=0, keepdims=True)
            shifted = [m - pen for pen in pens]
            sums = []
            for r in blocks:
                pr = jnp.exp2(s_ref[r:r + CHUNK, cl] - shifted[r // Q_TILE])
                sums.append(pr.reshape(groups, SUBLANES, LANES).sum(axis=0))
                p_ref[r:r + CHUNK, cl] = pr.astype(BF16)
            l = jnp.sum(col_reduce(sums, jnp.add), axis=0, keepdims=True)
            inv.append(1.0 / l)
        return jnp.concatenate(inv, axis=1)

    def weighted_values(u, h, slot, inv_l):
        v_t = jnp.concatenate([r[h] for r in v_refs[u:u + K_BLOCKS]], axis=1)
        acc_ref[u, h] = _dot(v_t, p_refs[slot][...]) * inv_l

    items = [(u, h) for u in range(Q_STEP_TILES) for h in range(ATT_HEADS)]
    scores(*items[0], 0)
    inv_prev = None
    for n, (u, h) in enumerate(items):
        if n + 1 < len(items):
            scores(*items[n + 1], (n + 1) % 2)
        inv_l = softmax(u, n % 2)
        if n > 0:
            weighted_values(*items[n - 1], (n - 1) % 2, inv_prev)
        inv_prev = inv_l
    weighted_values(*items[-1], (len(items) - 1) % 2, inv_prev)
    for u in range(Q_STEP_TILES):
        for p in range(ATT_HEADS // 2):
            pair_t = acc_ref[u, 2 * p:2 * p + 2].reshape(LANES, Q_TILE)
            o_ref[u * Q_TILE:(u + 1) * Q_TILE, p * LANES:(p + 1) * LANES] = pair_t.T.astype(BF16)


def _attention(q, k, v_t, bias):
    bsz, pairs, seq, _ = q.shape
    step_rows = Q_STEP_TILES * Q_TILE
    rel = range(-(K_BLOCKS - 1), Q_STEP_TILES)
    blk = lambda t, n: jnp.maximum(t * Q_STEP_TILES + n, 0)
    key = lambda n: pl.BlockSpec((None, pairs, Q_TILE, LANES), lambda b, t: (b, 0, blk(t, n), 0))
    val = lambda n: pl.BlockSpec((None, ATT_HEADS, ATT_HEAD_DIM, Q_TILE),
                                 lambda b, t: (b, 0, 0, blk(t, n)))
    slot = lambda dt: pltpu.VMEM((BAND, Q_TILE), dt)
    return pl.pallas_call(
        _attn_kernel,
        grid=(bsz, seq // step_rows),
        in_specs=[pl.BlockSpec((None, pairs, step_rows, LANES), lambda b, t: (b, 0, t, 0))]
                 + [key(n) for n in rel] + [val(n) for n in rel] + [_resident(bias.shape)],
        out_specs=pl.BlockSpec((None, step_rows, ATT_WIDTH), lambda b, t: (b, t, 0)),
        out_shape=jax.ShapeDtypeStruct((bsz, seq, ATT_WIDTH), BF16),
        scratch_shapes=[pltpu.VMEM((Q_STEP_TILES, ATT_HEADS, ATT_HEAD_DIM, Q_TILE), F32),
                        slot(F32), slot(F32), slot(BF16), slot(BF16)],
        compiler_params=_params("arbitrary", "arbitrary"),
        name="chunk_attn",
    )(q, *([k] * len(rel)), *([v_t] * len(rel)), bias)


def _pair_block_diag(wa, wx):
    def pairs(w):
        z = jnp.zeros_like(w[0::2])
        top = jnp.concatenate([w[0::2], z], axis=2)
        bot = jnp.concatenate([z, w[1::2]], axis=2)
        return jnp.concatenate([top, bot], axis=1)
    return jnp.concatenate([pairs(wa), pairs(wx)], axis=2)


def _mixout_kernel(x_ref, mod_ref, gpost_ref, att_ref, rec_ref, sa_ref, sr_ref,
                   wao_ref, wro_ref, wout_ref, *rest):
    n_cast = (len(rest) - 1) // 2
    o_ref = rest[n_cast]
    _cast_slabs(rest[:n_cast], rest[n_cast + 1:])
    _, _, gate = _mod_rows(mod_ref, 1)
    att = _dot(att_ref[...], wao_ref[...])
    rec = _dot(rec_ref[...], wro_ref[...])
    merged = sa_ref[...].astype(F32) * att + sr_ref[...].astype(F32) * rec
    y = _dot(merged.astype(BF16), wout_ref[...])
    o_ref[...] = x_ref[...] + gate * _rms(y, gpost_ref[1:2, :])


def _mixout(x2, mod, norm_post, att, rec, sa, sr, w_att_o, w_rec_o, w_out, *, seq, layer,
            casts=()):
    t, d = x2.shape
    tiles_per_seq = seq // MIX_TOKEN_TILE
    n_steps = t // MIX_TOKEN_TILE
    row = lambda n: pl.BlockSpec((MIX_TOKEN_TILE, n), lambda i: (i, 0))
    cast_in, cast_specs, cast_shapes = _cast_streams(casts, layer, n_steps)
    out, *cast_out = pl.pallas_call(
        _mixout_kernel,
        grid=(n_steps,),
        in_specs=[row(d),
                  pl.BlockSpec((None, 3 * N_SUB, d), lambda i: (i // tiles_per_seq, 0, 0)),
                  _resident(norm_post.shape),
                  row(ATT_WIDTH), row(LRU_WIDTH), row(d), row(d),
                  _resident(w_att_o.shape), _resident(w_rec_o.shape), _resident(w_out.shape)]
                 + cast_in,
        out_specs=[row(d)] + cast_specs,
        out_shape=[jax.ShapeDtypeStruct((t, d), F32)] + cast_shapes,
        compiler_params=_params("arbitrary"),
        name="mixer_out",
    )(x2, mod, norm_post, att, rec, sa, sr, w_att_o, w_rec_o, w_out, *casts)
    return out, cast_out


def _layer(x2, c, l, bsz, seq, w_ada, b_ada, norm_pre, norm_post, ffn1_w_gu, ffn1_w_down, w_in,
           rel_bias, conv_w, conv_b, lru_wa, lru_ba, lru_wx, lru_bx, lru_lambda, w_att_o,
           w_rec_o, w_out, ffn2_w_gu, ffn2_w_down):
    bf = lambda w: w[l].astype(BF16)
    mod = _ada(c, w_ada[l], b_ada[l]).reshape(bsz, 3 * N_SUB, D_MODEL)
    npre, npost = norm_pre[l], norm_post[l]

    x2, (w_in_b, w_att_o_b, w_rec_o_b, w_out_b, w_down2_b) = _ffn(
        x2, mod, npre, npost, bf(ffn1_w_gu), bf(ffn1_w_down), sub=0, res_w=0.5, seq=seq, layer=l,
        casts=(w_in, w_att_o, w_rec_o, w_out, ffn2_w_down))

    w2 = (0.5 * _pair_block_diag(lru_wa[l], lru_wx[l])).astype(BF16)
    vec = lambda p: p[l].reshape(1, LRU_WIDTH)
    q, k, v, rec, sa, sr = _proj(x2, mod, npre, w_in_b, conv_w[l], vec(conv_b), w2,
                                 0.5 * vec(lru_ba), 0.5 * vec(lru_bx), vec(lru_lambda), seq=seq)
    att = _attention(q, k, v, _bias_table(rel_bias[l]))
    x2, (w_gu2_b,) = _mixout(
        x2, mod, npost, att.reshape(bsz * seq, ATT_WIDTH), rec, sa, sr,
        w_att_o_b, w_rec_o_b, w_out_b, seq=seq, layer=l, casts=(ffn2_w_gu,))

    return _ffn(x2, mod, npre, npost, w_gu2_b, w_down2_b, sub=2, res_w=0.5, seq=seq, layer=l)[0]


def kernel(x, c, w_ada, b_ada, norm_pre, norm_post, ffn1_w_gu, ffn1_w_down, w_in, rel_bias, conv_w, conv_b, lru_wa, lru_ba, lru_wx, lru_bx, lru_lambda, w_att_o, w_rec_o, w_out, ffn2_w_gu, ffn2_w_down):
    bsz, seq, d = x.shape
    tiles = (TOKEN_TILE, MIX_TOKEN_TILE, FFN_TOKEN_TILE, Q_TILE * Q_STEP_TILES)
    assert d == D_MODEL and all(seq % n == 0 for n in tiles)
    assert TOKEN_TILE % LRU_TILE == 0
    x2 = x.reshape(bsz * seq, d)
    for l in range(w_ada.shape[0]):
        x2 = _layer(x2, c, l, bsz, seq, w_ada, b_ada, norm_pre, norm_post, ffn1_w_gu,
                    ffn1_w_down, w_in, rel_bias, conv_w, conv_b, lru_wa, lru_ba, lru_wx, lru_bx,
                    lru_lambda, w_att_o, w_rec_o, w_out, ffn2_w_gu, ffn2_w_down)
    return x2.reshape(bsz, seq, d)
```

```python
import functools
import math

import jax
import jax.numpy as jnp
from jax import lax
from jax.experimental import pallas as pl
from jax.experimental.pallas import tpu as pltpu

D_MODEL = 1024
CHUNK = 64
LEFT_CHUNKS = 8
ATT_HEADS = 8
ATT_HEAD_DIM = 64
ATT_WIDTH = ATT_HEADS * ATT_HEAD_DIM
MAX_REL = 128
LRU_WIDTH = D_MODEL
LRU_BLOCKS = 16
LRU_BLOCK = LRU_WIDTH // LRU_BLOCKS
CONV_WIDTH = 4
LRU_C = 8.0
D_FF = 2816
N_SUB = 3
EPS = 1e-6

LANES = 128
SUBLANES = 8
BF16_ROWS = 16
VMEM_LIMIT_BYTES = 56 * 1024 * 1024

TOKEN_TILE = 512
MIX_TOKEN_TILE = 1024
FFN_TOKEN_TILE = 1024
FFN_SUB_TILE = 256
FF_TILE = 256
PROJ_CHUNK = 256
ADA_TILE = 1536
Q_TILE = 4 * CHUNK
Q_STEP_TILES = 4
K_BLOCKS = LEFT_CHUNKS * CHUNK // Q_TILE + 1
BAND = K_BLOCKS * Q_TILE
LRU_TILE = 256
LRU_STEPS = LRU_TILE // SUBLANES
MASK_VALUE = -1e30
LOG2_E = math.log2(math.e)
BF16 = jnp.bfloat16
F32 = jnp.float32


def _dot(a, b):
    return jnp.dot(a, b, preferred_element_type=F32)


def _rms(x, g):
    return x * lax.rsqrt(jnp.mean(x * x, axis=-1, keepdims=True) + EPS) * g


def _sigmoid(x):
    return 1.0 / (1.0 + jnp.exp(-x))


def _gelu_tanh(x):
    c = math.sqrt(2.0 / math.pi)
    return x * (0.5 * (1.0 + jnp.tanh(c * (x + 0.044715 * (x * x * x)))))


def _mod_rows(mod_ref, sub):
    shift = mod_ref[3 * sub:3 * sub + 1, :]
    scale = mod_ref[3 * sub + 1:3 * sub + 2, :]
    gate = mod_ref[3 * sub + 2:3 * sub + 3, :]
    return shift, scale, gate


def _resident(shape):
    nd = len(shape)
    return pl.BlockSpec(shape, lambda *_: (0,) * nd, pipeline_mode=pl.Buffered(1))


def _params(*semantics):
    return pltpu.CompilerParams(dimension_semantics=semantics,
                                vmem_limit_bytes=VMEM_LIMIT_BYTES)


def _cast_streams(stacked, layer, n_steps):
    for w in stacked:
        assert w.shape[1] % (BF16_ROWS * n_steps) == 0 and w.shape[2] % LANES == 0, w.shape
    in_specs = [pl.BlockSpec((None, w.shape[1] // n_steps, w.shape[2]), lambda i: (layer, i, 0))
                for w in stacked]
    out_specs = [pl.BlockSpec((w.shape[1] // n_steps, w.shape[2]), lambda i: (i, 0))
                 for w in stacked]
    shapes = [jax.ShapeDtypeStruct(w.shape[1:], BF16) for w in stacked]
    return in_specs, out_specs, shapes


def _cast_slabs(in_refs, out_refs):
    for src, dst in zip(in_refs, out_refs):
        dst[...] = src[...].astype(BF16)


def _ada_kernel(c_ref, w_ref, b_ref, o_ref):
    c = c_ref[...]
    c_act = (c * _sigmoid(c)).astype(BF16)
    o_ref[...] = _dot(c_act, w_ref[...].astype(BF16)) + b_ref[...]


def _ada(c, w, b):
    bsz, d = c.shape
    n = w.shape[1]
    return pl.pallas_call(
        _ada_kernel,
        grid=(n // ADA_TILE,),
        in_specs=[pl.BlockSpec((bsz, d), lambda j: (0, 0)),
                  pl.BlockSpec((d, ADA_TILE), lambda j: (0, j)),
                  pl.BlockSpec((1, ADA_TILE), lambda j: (0, j))],
        out_specs=pl.BlockSpec((bsz, ADA_TILE), lambda j: (0, j)),
        out_shape=jax.ShapeDtypeStruct((bsz, n), F32),
        compiler_params=_params("arbitrary"),
        name="adaln",
    )(c, w, b.reshape(1, n))


def _ffn_kernel(x_ref, mod_ref, gpre_ref, gpost_ref, wgu_ref, wdn_ref, *rest, sub, res_w):
    n_cast = (len(rest) - 4) // 2
    o_ref = rest[n_cast]
    h_ref, act_ref, y_ref = rest[-3:]
    _cast_slabs(rest[:n_cast], rest[n_cast + 1:-3])
    shift, scale, gate = _mod_rows(mod_ref, sub)
    n_sub = x_ref.shape[0] // FFN_SUB_TILE

    def rows(s):
        return slice(s * FFN_SUB_TILE, (s + 1) * FFN_SUB_TILE)

    def prologue(s):
        x = x_ref[rows(s), :]
        h_ref[s % 2] = (_rms(x, gpre_ref[sub:sub + 1, :]) * (1.0 + scale) + shift).astype(BF16)

    def gate_up(s, j):
        lo = j * FF_TILE
        h = h_ref[s % 2]
        g = _dot(h, wgu_ref[:, lo:lo + FF_TILE])
        u = _dot(h, wgu_ref[:, D_FF + lo:D_FF + lo + FF_TILE])
        act_ref[s % 2, :, lo:lo + FF_TILE] = (g * _sigmoid(g) * u).astype(BF16)

    def down(s):
        y_ref[s % 2] = _dot(act_ref[s % 2], wdn_ref[...])

    def epilogue(s):
        y = _rms(y_ref[s % 2], gpost_ref[sub:sub + 1, :])
        o_ref[rows(s), :] = x_ref[rows(s), :] + (res_w * gate) * y

    n_ff = D_FF // FF_TILE
    prologue(0)
    for s in range(n_sub):
        for j in range(n_ff):
            gate_up(s, j)
            if j == 0 and s > 0:
                epilogue(s - 1)
            if j == n_ff // 2 and s + 1 < n_sub:
                prologue(s + 1)
        down(s)
    epilogue(n_sub - 1)


def _ffn(x2, mod, norm_pre, norm_post, w_gu, w_down, *, sub, res_w, seq, layer, casts=()):
    t, d = x2.shape
    tiles_per_seq = seq // FFN_TOKEN_TILE
    n_steps = t // FFN_TOKEN_TILE
    cast_in, cast_specs, cast_shapes = _cast_streams(casts, layer, n_steps)
    out, *cast_out = pl.pallas_call(
        functools.partial(_ffn_kernel, sub=sub, res_w=res_w),
        grid=(n_steps,),
        in_specs=[pl.BlockSpec((FFN_TOKEN_TILE, d), lambda i: (i, 0)),
                  pl.BlockSpec((None, 3 * N_SUB, d), lambda i: (i // tiles_per_seq, 0, 0)),
                  _resident(norm_pre.shape),
                  _resident(norm_post.shape),
                  _resident(w_gu.shape),
                  _resident(w_down.shape)] + cast_in,
        out_specs=[pl.BlockSpec((FFN_TOKEN_TILE, d), lambda i: (i, 0))] + cast_specs,
        out_shape=[jax.ShapeDtypeStruct((t, d), F32)] + cast_shapes,
        scratch_shapes=[pltpu.VMEM((2, FFN_SUB_TILE, d), BF16),
                        pltpu.VMEM((2, FFN_SUB_TILE, D_FF), BF16),
                        pltpu.VMEM((2, FFN_SUB_TILE, d), F32)],
        compiler_params=_params("arbitrary"),
        name=f"ffn{sub}",
    )(x2, mod, norm_pre, norm_post, w_gu, w_down, *casts)
    return out, cast_out


def _lru_unit(g, xr_ref, row0, gy_ref, o_ref, fresh, cw_ref, cb_ref, w2_ref, ba_ref, bx_ref, decay,
              tail_ref, h_ref, hs_ref):
    sub = lax.broadcasted_iota(jnp.int32, (SUBLANES, LANES), 0)
    taps = CONV_WIDTH - 1
    sl = slice(g * LANES, (g + 1) * LANES)
    x3 = xr_ref[g, row0:row0 + LRU_TILE, :].reshape(LRU_STEPS, SUBLANES, LANES)
    cur_tail = x3[LRU_STEPS - taps:]
    lead = pltpu.roll(jnp.where(sub == SUBLANES - 1, tail_ref[g] * fresh, cur_tail), 1, 1)
    tail_ref[g] = cur_tail
    xext = jnp.concatenate([lead, x3], axis=0)
    xc3 = cb_ref[:, sl] + cw_ref[0:1, sl] * xext[0:LRU_STEPS]
    for w in range(1, CONV_WIDTH):
        xc3 = xc3 + cw_ref[w:w + 1, sl] * xext[w:w + LRU_STEPS]
    xc = xc3.reshape(LRU_TILE, LANES)
    z = _dot(xc.astype(BF16), w2_ref[g])
    ta = jnp.tanh(z[:, :LANES] + ba_ref[:, sl])
    tx = jnp.tanh(z[:, LANES:] + bx_ref[:, sl])
    log_a = decay[:, sl] * (ta + 1.0)
    a = jnp.exp(log_a)
    th = jnp.tanh(log_a)
    q = (-0.5 * th) / (1.0 - th)
    half_mult = jnp.where(q > 0.0, q * lax.rsqrt(q), 0.0)
    u = half_mult * ((tx + 1.0) * xc)
    a3 = a.reshape(LRU_STEPS, SUBLANES, LANES)
    u3 = u.reshape(LRU_STEPS, SUBLANES, LANES)
    loc = [u3[0]]
    prod = [a3[0]]
    for i in range(1, LRU_STEPS):
        loc.append(a3[i] * loc[-1] + u3[i])
        prod.append(a3[i] * prod[-1])
    blk_a, blk_u = prod[-1], loc[-1]
    for s in (1, 2, 4):
        keep = sub >= s
        a_prev = jnp.where(keep, pltpu.roll(blk_a, s, 0), 1.0)
        u_prev = jnp.where(keep, pltpu.roll(blk_u, s, 0), 0.0)
        blk_u = blk_u + blk_a * u_prev
        blk_a = blk_a * a_prev
    h_in = h_ref[g] * fresh
    end = blk_u + blk_a * h_in
    entry = jnp.where(sub == 0, h_in, pltpu.roll(end, 1, 0))
    h_ref[g] = jnp.broadcast_to(end[SUBLANES - 1:SUBLANES, :], (SUBLANES, LANES))
    for i in range(LRU_STEPS):
        hs_ref[g, i * SUBLANES:(i + 1) * SUBLANES, :] = loc[i] + prod[i] * entry
    rows = []
    for n in range(LRU_TILE // SUBLANES):
        j, i0 = divmod(n * SUBLANES, LRU_STEPS)
        rows.append(hs_ref[g, pl.ds(i0 * SUBLANES + j, SUBLANES, stride=SUBLANES), :])
    hs = jnp.concatenate(rows, axis=0)
    o_ref[row0:row0 + LRU_TILE, sl] = (hs * gy_ref[row0:row0 + LRU_TILE, sl]).astype(BF16)


def _proj_kernel(x_ref, mod_ref, gpre_ref, w_ref, cw_ref, cb_ref, w2_ref, ba_ref, bx_ref, lam_ref,
                 q_ref, k_ref, v_ref, rec_ref, sa_ref, sr_ref,
                 xr_s, gy_s, tail_ref, h_ref, hs_ref, *, tiles_per_seq, n_tiles):
    i = pl.program_id(0)
    rows = x_ref.shape[0]
    cw = PROJ_CHUNK
    per_chunk = cw // LANES
    lru0 = 3 * ATT_WIDTH
    n_att, n_lru = ATT_WIDTH // cw, LRU_WIDTH // cw

    @pl.when(i == 0)
    def _():
        for ref in (xr_s, gy_s, tail_ref, h_ref):
            ref[...] = jnp.zeros_like(ref)

    def scan_units():
        neg_lam = -lam_ref[...]
        softplus = jnp.maximum(neg_lam, 0.0) + jnp.log1p(jnp.exp(-jnp.abs(neg_lam)))
        decay = (-0.5 * LRU_C) * softplus
        first = jnp.where(lax.rem(i - 1, tiles_per_seq) == 0, 0.0, 1.0)

        def unit(row0, g):
            _lru_unit(g, xr_s, row0, gy_s, rec_ref, first if row0 == 0 else 1.0, cw_ref, cb_ref,
                      w2_ref, ba_ref, bx_ref, decay, tail_ref, h_ref, hs_ref)

        return [functools.partial(unit, row0, g)
                for c in range(n_lru) for row0 in range(0, rows, LRU_TILE)
                for g in range(c * per_chunk, (c + 1) * per_chunk)]

    @pl.when(i < n_tiles)
    def _():
        x = x_ref[...]
        shift, scale, _ = _mod_rows(mod_ref, 1)
        gain = gpre_ref[1:2, :] * (1.0 + scale)
        inv = lax.rsqrt(jnp.mean(x * x, axis=-1, keepdims=True) + EPS)
        h = (x * inv * gain + shift).astype(BF16)

        def proj(col):
            return _dot(h, w_ref[:, col:col + cw])

        def q_chunk(c):
            q = (proj(c * cw) * (ATT_HEAD_DIM ** -0.5 * LOG2_E)).astype(BF16)
            for p in range(per_chunk):
                q_ref[c * per_chunk + p] = q[:, p * LANES:(p + 1) * LANES]

        def k_chunk(c):
            k = proj(ATT_WIDTH + c * cw).astype(BF16)
            for p in range(per_chunk):
                k_ref[c * per_chunk + p] = k[:, p * LANES:(p + 1) * LANES]

        def v_chunk(c):
            heads = cw // ATT_HEAD_DIM
            v_t = proj(2 * ATT_WIDTH + c * cw).T.astype(BF16)
            v_ref[c * heads:(c + 1) * heads] = v_t.reshape(heads, ATT_HEAD_DIM, rows)

        def gate_chunk(ref, col0, c):
            ref[:, c * cw:(c + 1) * cw] = _sigmoid(proj(col0 + c * cw)).astype(BF16)

        def gy_chunk(c):
            gy_s[:, c * cw:(c + 1) * cw] = _gelu_tanh(proj(lru0 + LRU_WIDTH + c * cw))

        def xr_chunk(c):
            xr = proj(lru0 + c * cw)
            for src_row in range(0, rows, LRU_STEPS):
                tile0, j = src_row // LRU_TILE * LRU_TILE, src_row % LRU_TILE // LRU_STEPS
                for p in range(per_chunk):
                    xr_s[c * per_chunk + p, pl.ds(tile0 + j, LRU_STEPS, stride=SUBLANES), :] = (
                        xr[src_row:src_row + LRU_STEPS, p * LANES:(p + 1) * LANES])

        free = ([functools.partial(q_chunk, c) for c in range(n_att)]
                + [functools.partial(k_chunk, c) for c in range(n_att)]
                + [functools.partial(v_chunk, c) for c in range(n_att)]
                + [functools.partial(gate_chunk, sa_ref, lru0 + 2 * LRU_WIDTH, c) for c in range(n_lru)]
                + [functools.partial(gate_chunk, sr_ref, lru0 + 3 * LRU_WIDTH, c) for c in range(n_lru)])
        last = [free.pop(), free.pop()]
        units = scan_units()
        per_bunch = len(units) // n_lru
        gated = [(per_bunch * (c + 1), fn) for c in range(n_lru)
                 for fn in (functools.partial(xr_chunk, c), functools.partial(gy_chunk, c))]
        n_chunks = len(free) + len(gated)
        done = 0
        for n in range(n_chunks):
            if gated and gated[0][0] <= done:
                gated.pop(0)[1]()
            else:
                free.pop(0)()
            target = min(len(units), -(-(n + 1) * len(units) // (n_chunks - 2)))
            while done < target:
                units[done]()
                done += 1
        for fn in last:
            fn()

    @pl.when(i == n_tiles)
    def _():
        for unit in scan_units():
            unit()


def _proj(x2, mod, norm_pre, w_in, conv_w, conv_b, w2, ba, bx, lam, *, seq):
    t, d = x2.shape
    tiles_per_seq = seq // TOKEN_TILE
    n_tiles = t // TOKEN_TILE
    cur = lambda i: jnp.minimum(i, n_tiles - 1)
    row = lambda n: pl.BlockSpec((TOKEN_TILE, n), lambda i: (cur(i), 0))
    tok = lambda n, dt: jax.ShapeDtypeStruct((t, n), dt)
    bsz, pairs, groups = t // seq, ATT_WIDTH // LANES, LRU_WIDTH // LANES
    qk_spec = pl.BlockSpec((None, pairs, TOKEN_TILE, LANES),
                           lambda i: (cur(i) // tiles_per_seq, 0, cur(i) % tiles_per_seq, 0))
    qk_shape = jax.ShapeDtypeStruct((bsz, pairs, seq, LANES), BF16)
    v_spec = pl.BlockSpec((None, ATT_HEADS, ATT_HEAD_DIM, TOKEN_TILE),
                          lambda i: (cur(i) // tiles_per_seq, 0, 0, cur(i) % tiles_per_seq))
    v_shape = jax.ShapeDtypeStruct((bsz, ATT_HEADS, ATT_HEAD_DIM, seq), BF16)
    rec_spec = pl.BlockSpec((TOKEN_TILE, LRU_WIDTH), lambda i: (jnp.maximum(i - 1, 0), 0))
    return pl.pallas_call(
        functools.partial(_proj_kernel, tiles_per_seq=tiles_per_seq, n_tiles=n_tiles),
        grid=(n_tiles + 1,),
        in_specs=[row(d),
                  pl.BlockSpec((None, 3 * N_SUB, d), lambda i: (cur(i) // tiles_per_seq, 0, 0)),
                  _resident(norm_pre.shape), _resident(w_in.shape),
                  _resident(conv_w.shape), _resident(conv_b.shape), _resident(w2.shape),
                  _resident(ba.shape), _resident(bx.shape), _resident(lam.shape)],
        out_specs=[qk_spec, qk_spec, v_spec, rec_spec, row(D_MODEL), row(D_MODEL)],
        out_shape=[qk_shape, qk_shape, v_shape, tok(LRU_WIDTH, BF16),
                   tok(D_MODEL, BF16), tok(D_MODEL, BF16)],
        scratch_shapes=[pltpu.VMEM((groups, TOKEN_TILE, LANES), F32),
                        pltpu.VMEM((TOKEN_TILE, LRU_WIDTH), F32),
                        pltpu.VMEM((groups, CONV_WIDTH - 1, SUBLANES, LANES), F32),
                        pltpu.VMEM((groups, SUBLANES, LANES), F32),
                        pltpu.VMEM((groups, LRU_TILE, LANES), F32)],
        compiler_params=_params("arbitrary"),
        name="mixer_proj",
    )(x2, mod, norm_pre, w_in, conv_w, conv_b, w2, ba, bx, lam)


REL_PAD = 384
TOEPLITZ = 1024


def _bias_kernel(tab_ref, o_ref):
    tab = tab_ref[...]
    hi = tab.astype(BF16)
    r1 = tab - hi.astype(F32)
    mid = r1.astype(BF16)
    lo = (r1 - mid.astype(F32)).astype(BF16)
    d_idx = lax.broadcasted_iota(jnp.int32, (REL_PAD, TOEPLITZ), 0)
    m_idx = lax.broadcasted_iota(jnp.int32, (REL_PAD, TOEPLITZ), 1)
    rel = jnp.clip(m_idx - (Q_TILE - 1), -MAX_REL, MAX_REL) + MAX_REL
    onehot = jnp.where(d_idx == rel, 1.0, 0.0).astype(BF16)
    profile = ((_dot(lo, onehot) + _dot(mid, onehot)) + _dot(hi, onehot)) * LOG2_E

    kk = lax.broadcasted_iota(jnp.int32, (BAND, Q_TILE), 0)
    r = lax.broadcasted_iota(jnp.int32, (BAND, Q_TILE), 1)
    qc = r // CHUNK
    kc = kk // CHUNK
    band = jnp.where((kc >= qc) & (kc <= qc + LEFT_CHUNKS), 0.0, MASK_VALUE)
    for h in range(ATT_HEADS):
        rows = jnp.broadcast_to(profile[h:h + 1, :], (BAND, TOEPLITZ))
        skew = pltpu.roll(rows, TOEPLITZ - (BAND - 1), 1, stride=1, stride_axis=0)
        o_ref[h] = skew[:, :Q_TILE] + band


def _bias_table(rel_bias):
    tab = jnp.pad(rel_bias, ((0, 0), (0, REL_PAD - rel_bias.shape[1])))
    return pl.pallas_call(
        _bias_kernel,
        out_shape=jax.ShapeDtypeStruct((ATT_HEADS, BAND, Q_TILE), F32),
        compiler_params=pltpu.CompilerParams(vmem_limit_bytes=VMEM_LIMIT_BYTES),
        name="rel_bias",
    )(tab)


def _attn_kernel(q_ref, *refs):
    n_kv = K_BLOCKS - 1 + Q_STEP_TILES
    k_refs, v_refs = refs[:n_kv], refs[n_kv:2 * n_kv]
    bias_ref, o_ref, acc_ref = refs[2 * n_kv:2 * n_kv + 3]
    s_refs, p_refs = refs[-4:-2], refs[-2:]
    first_tile = pl.program_id(1) * Q_STEP_TILES
    lane_head = lax.broadcasted_iota(jnp.int32, (1, LANES), 1) // ATT_HEAD_DIM
    zero = jnp.zeros((), BF16)
    live = [(2 * c * CHUNK, (2 * c + 2 + LEFT_CHUNKS) * CHUNK) for c in range(Q_TILE // LANES)]

    @pl.when((pl.program_id(0) == 0) & (pl.program_id(1) == 0))
    def _():
        for p_ref in p_refs:
            p_ref[...] = jnp.zeros_like(p_ref)

    def start_pens(u):
        return [jnp.where(first_tile + u + j < K_BLOCKS - 1, MASK_VALUE, 0.0)
                for j in range(K_BLOCKS - 1)] + [0.0]

    def scores(u, h, slot):
        pair = h // 2
        qh = jnp.where(lane_head == h % 2, q_ref[pair, u * Q_TILE:(u + 1) * Q_TILE, :], zero)
        kcat = jnp.concatenate([r[pair] for r in k_refs[u:u + K_BLOCKS]], axis=0)
        s = lax.dot_general(kcat, qh, (((1,), (1,)), ((), ())), preferred_element_type=F32)
        s_refs[slot][...] = s + bias_ref[h]

    def col_reduce(parts, op):
        while len(parts) > 1:
            parts = [op(parts[i], parts[i + 1]) if i + 1 < len(parts) else parts[i]
                     for i in range(0, len(parts), 2)]
        return parts[0]

    def softmax(u, slot):
        s_ref, p_ref = s_refs[slot], p_refs[slot]
        pens = start_pens(u)
        groups = CHUNK // SUBLANES
        inv = []
        for c, (r0, r1) in enumerate(live):
            cl = slice(c * LANES, (c + 1) * LANES)
            blocks = range(r0, r1, CHUNK)
            part = [s_ref[r:r + CHUNK, cl].reshape(groups, SUBLANES, LANES).max(axis=0)
                    + pens[r // Q_TILE] for r in blocks]
            m = jnp.max(col_reduce(part, jnp.maximum), axis=0, keepdims=True)
            shifted = [m - pen for pen in pens]
            sums = []
            for r in blocks:
                pr = jnp.exp2(s_ref[r:r + CHUNK, cl] - shifted[r // Q_TILE])
                sums.append(pr.reshape(groups, SUBLANES, LANES).sum(axis=0))
                p_ref[r:r + CHUNK, cl] = pr.astype(BF16)
            l = jnp.sum(col_reduce(sums, jnp.add), axis=0, keepdims=True)
            inv.append(1.0 / l)
        return jnp.concatenate(inv, axis=1)

    def weighted_values(u, h, slot, inv_l):
        v_t = jnp.concatenate([r[h] for r in v_refs[u:u + K_BLOCKS]], axis=1)
        acc_ref[u, h] = _dot(v_t, p_refs[slot][...]) * inv_l

    items = [(u, h) for u in range(Q_STEP_TILES) for h in range(ATT_HEADS)]
    scores(*items[0], 0)
    inv_prev = None
    for n, (u, h) in enumerate(items):
        if n + 1 < len(items):
            scores(*items[n + 1], (n + 1) % 2)
        inv_l = softmax(u, n % 2)
        if n > 0:
            weighted_values(*items[n - 1], (n - 1) % 2, inv_prev)
        inv_prev = inv_l
    weighted_values(*items[-1], (len(items) - 1) % 2, inv_prev)
    for u in range(Q_STEP_TILES):
        for p in range(ATT_HEADS // 2):
            pair_t = acc_ref[u, 2 * p:2 * p + 2].reshape(LANES, Q_TILE)
            o_ref[u * Q_TILE:(u + 1) * Q_TILE, p * LANES:(p + 1) * LANES] = pair_t.T.astype(BF16)


def _attention(q, k, v_t, bias):
    bsz, pairs, seq, _ = q.shape
    step_rows = Q_STEP_TILES * Q_TILE
    rel = range(-(K_BLOCKS - 1), Q_STEP_TILES)
    blk = lambda t, n: jnp.maximum(t * Q_STEP_TILES + n, 0)
    key = lambda n: pl.BlockSpec((None, pairs, Q_TILE, LANES), lambda b, t: (b, 0, blk(t, n), 0))
    val = lambda n: pl.BlockSpec((None, ATT_HEADS, ATT_HEAD_DIM, Q_TILE),
                                 lambda b, t: (b, 0, 0, blk(t, n)))
    slot = lambda dt: pltpu.VMEM((BAND, Q_TILE), dt)
    return pl.pallas_call(
        _attn_kernel,
        grid=(bsz, seq // step_rows),
        in_specs=[pl.BlockSpec((None, pairs, step_rows, LANES), lambda b, t: (b, 0, t, 0))]
                 + [key(n) for n in rel] + [val(n) for n in rel] + [_resident(bias.shape)],
        out_specs=pl.BlockSpec((None, step_rows, ATT_WIDTH), lambda b, t: (b, t, 0)),
        out_shape=jax.ShapeDtypeStruct((bsz, seq, ATT_WIDTH), BF16),
        scratch_shapes=[pltpu.VMEM((Q_STEP_TILES, ATT_HEADS, ATT_HEAD_DIM, Q_TILE), F32),
                        slot(F32), slot(F32), slot(BF16), slot(BF16)],
        compiler_params=_params("arbitrary", "arbitrary"),
        name="chunk_attn",
    )(q, *([k] * len(rel)), *([v_t] * len(rel)), bias)


def _pair_block_diag(wa, wx):
    def pairs(w):
        z = jnp.zeros_like(w[0::2])
        top = jnp.concatenate([w[0::2], z], axis=2)
        bot = jnp.concatenate([z, w[1::2]], axis=2)
        return jnp.concatenate([top, bot], axis=1)
    return jnp.concatenate([pairs(wa), pairs(wx)], axis=2)


def _mixout_kernel(x_ref, mod_ref, gpost_ref, att_ref, rec_ref, sa_ref, sr_ref,
                   wao_ref, wro_ref, wout_ref, *rest):
    n_cast = (len(rest) - 1) // 2
    o_ref = rest[n_cast]
    _cast_slabs(rest[:n_cast], rest[n_cast + 1:])
    _, _, gate = _mod_rows(mod_ref, 1)
    att = _dot(att_ref[...], wao_ref[...])
    rec = _dot(rec_ref[...], wro_ref[...])
    merged = sa_ref[...].astype(F32) * att + sr_ref[...].astype(F32) * rec
    y = _dot(merged.astype(BF16), wout_ref[...])
    o_ref[...] = x_ref[...] + gate * _rms(y, gpost_ref[1:2, :])


def _mixout(x2, mod, norm_post, att, rec, sa, sr, w_att_o, w_rec_o, w_out, *, seq, layer,
            casts=()):
    t, d = x2.shape
    tiles_per_seq = seq // MIX_TOKEN_TILE
    n_steps = t // MIX_TOKEN_TILE
    row = lambda n: pl.BlockSpec((MIX_TOKEN_TILE, n), lambda i: (i, 0))
    cast_in, cast_specs, cast_shapes = _cast_streams(casts, layer, n_steps)
    out, *cast_out = pl.pallas_call(
        _mixout_kernel,
        grid=(n_steps,),
        in_specs=[row(d),
                  pl.BlockSpec((None, 3 * N_SUB, d), lambda i: (i // tiles_per_seq, 0, 0)),
                  _resident(norm_post.shape),
                  row(ATT_WIDTH), row(LRU_WIDTH), row(d), row(d),
                  _resident(w_att_o.shape), _resident(w_rec_o.shape), _resident(w_out.shape)]
                 + cast_in,
        out_specs=[row(d)] + cast_specs,
        out_shape=[jax.ShapeDtypeStruct((t, d), F32)] + cast_shapes,
        compiler_params=_params("arbitrary"),
        name="mixer_out",
    )(x2, mod, norm_post, att, rec, sa, sr, w_att_o, w_rec_o, w_out, *casts)
    return out, cast_out


def _layer(x2, c, l, bsz, seq, w_ada, b_ada, norm_pre, norm_post, ffn1_w_gu, ffn1_w_down, w_in,
           rel_bias, conv_w, conv_b, lru_wa, lru_ba, lru_wx, lru_bx, lru_lambda, w_att_o,
           w_rec_o, w_out, ffn2_w_gu, ffn2_w_down):
    bf = lambda w: w[l].astype(BF16)
    mod = _ada(c, w_ada[l], b_ada[l]).reshape(bsz, 3 * N_SUB, D_MODEL)
    npre, npost = norm_pre[l], norm_post[l]

    x2, (w_in_b, w_att_o_b, w_rec_o_b, w_out_b, w_down2_b) = _ffn(
        x2, mod, npre, npost, bf(ffn1_w_gu), bf(ffn1_w_down), sub=0, res_w=0.5, seq=seq, layer=l,
        casts=(w_in, w_att_o, w_rec_o, w_out, ffn2_w_down))

    w2 = (0.5 * _pair_block_diag(lru_wa[l], lru_wx[l])).astype(BF16)
    vec = lambda p: p[l].reshape(1, LRU_WIDTH)
    q, k, v, rec, sa, sr = _proj(x2, mod, npre, w_in_b, conv_w[l], vec(conv_b), w2,
                                 0.5 * vec(lru_ba), 0.5 * vec(lru_bx), vec(lru_lambda), seq=seq)
    att = _attention(q, k, v, _bias_table(rel_bias[l]))
    x2, (w_gu2_b,) = _mixout(
        x2, mod, npost, att.reshape(bsz * seq, ATT_WIDTH), rec, sa, sr,
        w_att_o_b, w_rec_o_b, w_out_b, seq=seq, layer=l, casts=(ffn2_w_gu,))

    return _ffn(x2, mod, npre, npost, w_gu2_b, w_down2_b, sub=2, res_w=0.5, seq=seq, layer=l)[0]


def kernel(x, c, w_ada, b_ada, norm_pre, norm_post, ffn1_w_gu, ffn1_w_down, w_in, rel_bias, conv_w, conv_b, lru_wa, lru_ba, lru_wx, lru_bx, lru_lambda, w_att_o, w_rec_o, w_out, ffn2_w_gu, ffn2_w_down):
    bsz, seq, d = x.shape
    tiles = (TOKEN_TILE, MIX_TOKEN_TILE, FFN_TOKEN_TILE, Q_TILE * Q_STEP_TILES)
    assert d == D_MODEL and all(seq % n == 0 for n in tiles)
    assert TOKEN_TILE % LRU_TILE == 0
    x2 = x.reshape(bsz * seq, d)
    for l in range(w_ada.shape[0]):
        x2 = _layer(x2, c, l, bsz, seq, w_ada, b_ada, norm_pre, norm_post, ffn1_w_gu,
                    ffn1_w_down, w_in, rel_bias, conv_w, conv_b, lru_wa, lru_ba, lru_wx, lru_bx,
                    lru_lambda, w_att_o, w_rec_o, w_out, ffn2_w_gu, ffn2_w_down)
    return x2.reshape(bsz, seq, d)
```

```python
import functools
import math

import jax
import jax.numpy as jnp
from jax import lax
from jax.experimental import pallas as pl
from jax.experimental.pallas import tpu as pltpu

D_MODEL = 1024
CHUNK = 64
LEFT_CHUNKS = 8
ATT_HEADS = 8
ATT_HEAD_DIM = 64
ATT_WIDTH = ATT_HEADS * ATT_HEAD_DIM
MAX_REL = 128
LRU_WIDTH = D_MODEL
LRU_BLOCKS = 16
LRU_BLOCK = LRU_WIDTH // LRU_BLOCKS
CONV_WIDTH = 4
LRU_C = 8.0
D_FF = 2816
N_SUB = 3
EPS = 1e-6

LANES = 128
SUBLANES = 8
BF16_ROWS = 16
VMEM_LIMIT_BYTES = 56 * 1024 * 1024

TOKEN_TILE = 512
MIX_TOKEN_TILE = 1024
FFN_TOKEN_TILE = 1024
FFN_SUB_TILE = 256
FF_TILE = 256
PROJ_CHUNK = 256
ADA_TILE = 1536
Q_TILE = 4 * CHUNK
Q_STEP_TILES = 4
K_BLOCKS = LEFT_CHUNKS * CHUNK // Q_TILE + 1
BAND = K_BLOCKS * Q_TILE
LRU_TILE = 256
LRU_STEPS = LRU_TILE // SUBLANES
MASK_VALUE = -1e30
LOG2_E = math.log2(math.e)
BF16 = jnp.bfloat16
F32 = jnp.float32


def _dot(a, b):
    return jnp.dot(a, b, preferred_element_type=F32)


def _rms(x, g):
    return x * lax.rsqrt(jnp.mean(x * x, axis=-1, keepdims=True) + EPS) * g


def _sigmoid(x):
    return 1.0 / (1.0 + jnp.exp(-x))


def _gelu_tanh(x):
    c = math.sqrt(2.0 / math.pi)
    return x * (0.5 * (1.0 + jnp.tanh(c * (x + 0.044715 * (x * x * x)))))


def _mod_rows(mod_ref, sub):
    shift = mod_ref[3 * sub:3 * sub + 1, :]
    scale = mod_ref[3 * sub + 1:3 * sub + 2, :]
    gate = mod_ref[3 * sub + 2:3 * sub + 3, :]
    return shift, scale, gate


def _resident(shape):
    nd = len(shape)
    return pl.BlockSpec(shape, lambda *_: (0,) * nd, pipeline_mode=pl.Buffered(1))


def _params(*semantics):
    return pltpu.CompilerParams(dimension_semantics=semantics,
                                vmem_limit_bytes=VMEM_LIMIT_BYTES)


def _cast_streams(stacked, layer, n_steps):
    for w in stacked:
        assert w.shape[1] % (BF16_ROWS * n_steps) == 0 and w.shape[2] % LANES == 0, w.shape
    in_specs = [pl.BlockSpec((None, w.shape[1] // n_steps, w.shape[2]), lambda i: (layer, i, 0))
                for w in stacked]
    out_specs = [pl.BlockSpec((w.shape[1] // n_steps, w.shape[2]), lambda i: (i, 0))
                 for w in stacked]
    shapes = [jax.ShapeDtypeStruct(w.shape[1:], BF16) for w in stacked]
    return in_specs, out_specs, shapes


def _cast_slabs(in_refs, out_refs):
    for src, dst in zip(in_refs, out_refs):
        dst[...] = src[...].astype(BF16)


def _ada_kernel(c_ref, w_ref, b_ref, o_ref):
    c = c_ref[...]
    c_act = (c * _sigmoid(c)).astype(BF16)
    o_ref[...] = _dot(c_act, w_ref[...].astype(BF16)) + b_ref[...]


def _ada(c, w, b):
    bsz, d = c.shape
    n = w.shape[1]
    return pl.pallas_call(
        _ada_kernel,
        grid=(n // ADA_TILE,),
        in_specs=[pl.BlockSpec((bsz, d), lambda j: (0, 0)),
                  pl.BlockSpec((d, ADA_TILE), lambda j: (0, j)),
                  pl.BlockSpec((1, ADA_TILE), lambda j: (0, j))],
        out_specs=pl.BlockSpec((bsz, ADA_TILE), lambda j: (0, j)),
        out_shape=jax.ShapeDtypeStruct((bsz, n), F32),
        compiler_params=_params("arbitrary"),
        name="adaln",
    )(c, w, b.reshape(1, n))


def _ffn_kernel(x_ref, mod_ref, gpre_ref, gpost_ref, wgu_ref, wdn_ref, *rest, sub, res_w):
    n_cast = (len(rest) - 4) // 2
    o_ref = rest[n_cast]
    h_ref, act_ref, y_ref = rest[-3:]
    _cast_slabs(rest[:n_cast], rest[n_cast + 1:-3])
    shift, scale, gate = _mod_rows(mod_ref, sub)
    n_sub = x_ref.shape[0] // FFN_SUB_TILE

    def rows(s):
        return slice(s * FFN_SUB_TILE, (s + 1) * FFN_SUB_TILE)

    def prologue(s):
        x = x_ref[rows(s), :]
        h_ref[s % 2] = (_rms(x, gpre_ref[sub:sub + 1, :]) * (1.0 + scale) + shift).astype(BF16)

    def gate_up(s, j):
        lo = j * FF_TILE
        h = h_ref[s % 2]
        g = _dot(h, wgu_ref[:, lo:lo + FF_TILE])
        u = _dot(h, wgu_ref[:, D_FF + lo:D_FF + lo + FF_TILE])
        act_ref[s % 2, :, lo:lo + FF_TILE] = (g * _sigmoid(g) * u).astype(BF16)

    def down(s):
        y_ref[s % 2] = _dot(act_ref[s % 2], wdn_ref[...])

    def epilogue(s):
        y = _rms(y_ref[s % 2], gpost_ref[sub:sub + 1, :])
        o_ref[rows(s), :] = x_ref[rows(s), :] + (res_w * gate) * y

    n_ff = D_FF // FF_TILE
    prologue(0)
    for s in range(n_sub):
        for j in range(n_ff):
            gate_up(s, j)
            if j == 0 and s > 0:
                epilogue(s - 1)
            if j == n_ff // 2 and s + 1 < n_sub:
                prologue(s + 1)
        down(s)
    epilogue(n_sub - 1)


def _ffn(x2, mod, norm_pre, norm_post, w_gu, w_down, *, sub, res_w, seq, layer, casts=()):
    t, d = x2.shape
    tiles_per_seq = seq // FFN_TOKEN_TILE
    n_steps = t // FFN_TOKEN_TILE
    cast_in, cast_specs, cast_shapes = _cast_streams(casts, layer, n_steps)
    out, *cast_out = pl.pallas_call(
        functools.partial(_ffn_kernel, sub=sub, res_w=res_w),
        grid=(n_steps,),
        in_specs=[pl.BlockSpec((FFN_TOKEN_TILE, d), lambda i: (i, 0)),
                  pl.BlockSpec((None, 3 * N_SUB, d), lambda i: (i // tiles_per_seq, 0, 0)),
                  _resident(norm_pre.shape),
                  _resident(norm_post.shape),
                  _resident(w_gu.shape),
                  _resident(w_down.shape)] + cast_in,
        out_specs=[pl.BlockSpec((FFN_TOKEN_TILE, d), lambda i: (i, 0))] + cast_specs,
        out_shape=[jax.ShapeDtypeStruct((t, d), F32)] + cast_shapes,
        scratch_shapes=[pltpu.VMEM((2, FFN_SUB_TILE, d), BF16),
                        pltpu.VMEM((2, FFN_SUB_TILE, D_FF), BF16),
                        pltpu.VMEM((2, FFN_SUB_TILE, d), F32)],
        compiler_params=_params("arbitrary"),
        name=f"ffn{sub}",
    )(x2, mod, norm_pre, norm_post, w_gu, w_down, *casts)
    return out, cast_out


def _lru_unit(g, xr_ref, row0, gy_ref, o_ref, fresh, cw_ref, cb_ref, w2_ref, ba_ref, bx_ref, decay,
              tail_ref, h_ref, hs_ref):
    sub = lax.broadcasted_iota(jnp.int32, (SUBLANES, LANES), 0)
    taps = CONV_WIDTH - 1
    sl = slice(g * LANES, (g + 1) * LANES)
    x3 = xr_ref[g, row0:row0 + LRU_TILE, :].reshape(LRU_STEPS, SUBLANES, LANES)
    cur_tail = x3[LRU_STEPS - taps:]
    lead = pltpu.roll(jnp.where(sub == SUBLANES - 1, tail_ref[g] * fresh, cur_tail), 1, 1)
    tail_ref[g] = cur_tail
    xext = jnp.concatenate([lead, x3], axis=0)
    xc3 = cb_ref[:, sl] + cw_ref[0:1, sl] * xext[0:LRU_STEPS]
    for w in range(1, CONV_WIDTH):
        xc3 = xc3 + cw_ref[w:w + 1, sl] * xext[w:w + LRU_STEPS]
    xc = xc3.reshape(LRU_TILE, LANES)
    z = _dot(xc.astype(BF16), w2_ref[g])
    ta = jnp.tanh(z[:, :LANES] + ba_ref[:, sl])
    tx = jnp.tanh(z[:, LANES:] + bx_ref[:, sl])
    log_a = decay[:, sl] * (ta + 1.0)
    a = jnp.exp(log_a)
    th = jnp.tanh(log_a)
    q = (-0.5 * th) / (1.0 - th)
    half_mult = jnp.where(q > 0.0, q * lax.rsqrt(q), 0.0)
    u = half_mult * ((tx + 1.0) * xc)
    a3 = a.reshape(LRU_STEPS, SUBLANES, LANES)
    u3 = u.reshape(LRU_STEPS, SUBLANES, LANES)
    loc = [u3[0]]
    prod = [a3[0]]
    for i in range(1, LRU_STEPS):
        loc.append(a3[i] * loc[-1] + u3[i])
        prod.append(a3[i] * prod[-1])
    blk_a, blk_u = prod[-1], loc[-1]
    for s in (1, 2, 4):
        keep = sub >= s
        a_prev = jnp.where(keep, pltpu.roll(blk_a, s, 0), 1.0)
        u_prev = jnp.where(keep, pltpu.roll(blk_u, s, 0), 0.0)
        blk_u = blk_u + blk_a * u_prev
        blk_a = blk_a * a_prev
    h_in = h_ref[g] * fresh
    end = blk_u + blk_a * h_in
    entry = jnp.where(sub == 0, h_in, pltpu.roll(end, 1, 0))
    h_ref[g] = jnp.broadcast_to(end[SUBLANES - 1:SUBLANES, :], (SUBLANES, LANES))
    for i in range(LRU_STEPS):
        hs_ref[g, i * SUBLANES:(i + 1) * SUBLANES, :] = loc[i] + prod[i] * entry
    rows = []
    for n in range(LRU_TILE // SUBLANES):
        j, i0 = divmod(n * SUBLANES, LRU_STEPS)
        rows.append(hs_ref[g, pl.ds(i0 * SUBLANES + j, SUBLANES, stride=SUBLANES), :])
    hs = jnp.concatenate(rows, axis=0)
    o_ref[row0:row0 + LRU_TILE, sl] = (hs * gy_ref[row0:row0 + LRU_TILE, sl]).astype(BF16)


def _proj_kernel(x_ref, mod_ref, gpre_ref, w_ref, cw_ref, cb_ref, w2_ref, ba_ref, bx_ref, lam_ref,
                 q_ref, k_ref, v_ref, rec_ref, sa_ref, sr_ref,
                 xr_s, gy_s, tail_ref, h_ref, hs_ref, *, tiles_per_seq, n_tiles):
    i = pl.program_id(0)
    rows = x_ref.shape[0]
    cw = PROJ_CHUNK
    per_chunk = cw // LANES
    lru0 = 3 * ATT_WIDTH
    n_att, n_lru = ATT_WIDTH // cw, LRU_WIDTH // cw

    @pl.when(i == 0)
    def _():
        for ref in (xr_s, gy_s, tail_ref, h_ref):
            ref[...] = jnp.zeros_like(ref)

    def scan_units():
        neg_lam = -lam_ref[...]
        softplus = jnp.maximum(neg_lam, 0.0) + jnp.log1p(jnp.exp(-jnp.abs(neg_lam)))
        decay = (-0.5 * LRU_C) * softplus
        first = jnp.where(lax.rem(i - 1, tiles_per_seq) == 0, 0.0, 1.0)

        def unit(row0, g):
            _lru_unit(g, xr_s, row0, gy_s, rec_ref, first if row0 == 0 else 1.0, cw_ref, cb_ref,
                      w2_ref, ba_ref, bx_ref, decay, tail_ref, h_ref, hs_ref)

        return [functools.partial(unit, row0, g)
                for row0 in range(0, rows, LRU_TILE) for g in range(LRU_WIDTH // LANES)]

    @pl.when(i < n_tiles)
    def _():
        x = x_ref[...]
        shift, scale, _ = _mod_rows(mod_ref, 1)
        gain = gpre_ref[1:2, :] * (1.0 + scale)
        inv = lax.rsqrt(jnp.mean(x * x, axis=-1, keepdims=True) + EPS)
        h = (x * inv * gain + shift).astype(BF16)

        def proj(col):
            return _dot(h, w_ref[:, col:col + cw])

        def q_chunk(c):
            q = (proj(c * cw) * (ATT_HEAD_DIM ** -0.5 * LOG2_E)).astype(BF16)
            for p in range(per_chunk):
                q_ref[c * per_chunk + p] = q[:, p * LANES:(p + 1) * LANES]

        def k_chunk(c):
            k = proj(ATT_WIDTH + c * cw).astype(BF16)
            for p in range(per_chunk):
                k_ref[c * per_chunk + p] = k[:, p * LANES:(p + 1) * LANES]

        def v_chunk(c):
            heads = cw // ATT_HEAD_DIM
            v_t = proj(2 * ATT_WIDTH + c * cw).T.astype(BF16)
            v_ref[c * heads:(c + 1) * heads] = v_t.reshape(heads, ATT_HEAD_DIM, rows)

        def gate_chunk(ref, col0, c):
            ref[:, c * cw:(c + 1) * cw] = _sigmoid(proj(col0 + c * cw)).astype(BF16)

        def gy_chunk(c):
            gy_s[:, c * cw:(c + 1) * cw] = _gelu_tanh(proj(lru0 + LRU_WIDTH + c * cw))

        def xr_chunk(c):
            xr = proj(lru0 + c * cw)
            for src_row in range(0, rows, LRU_STEPS):
                tile0, j = src_row // LRU_TILE * LRU_TILE, src_row % LRU_TILE // LRU_STEPS
                for p in range(per_chunk):
                    xr_s[c * per_chunk + p, pl.ds(tile0 + j, LRU_STEPS, stride=SUBLANES), :] = (
                        xr[src_row:src_row + LRU_STEPS, p * LANES:(p + 1) * LANES])

        chunks = ([functools.partial(q_chunk, c) for c in range(n_att)]
                  + [functools.partial(k_chunk, c) for c in range(n_att)]
                  + [functools.partial(v_chunk, c) for c in range(n_att)]
                  + [functools.partial(gate_chunk, sa_ref, lru0 + 2 * LRU_WIDTH, c) for c in range(n_lru)]
                  + [functools.partial(gate_chunk, sr_ref, lru0 + 3 * LRU_WIDTH, c) for c in range(n_lru)]
                  + [functools.partial(gy_chunk, c) for c in range(n_lru)]
                  + [functools.partial(xr_chunk, c) for c in range(n_lru)])
        units = scan_units()
        for n, chunk in enumerate(chunks):
            chunk()
            if n < len(units):
                units[n]()

    @pl.when(i == n_tiles)
    def _():
        for unit in scan_units():
            unit()


def _proj(x2, mod, norm_pre, w_in, conv_w, conv_b, w2, ba, bx, lam, *, seq):
    t, d = x2.shape
    tiles_per_seq = seq // TOKEN_TILE
    n_tiles = t // TOKEN_TILE
    cur = lambda i: jnp.minimum(i, n_tiles - 1)
    row = lambda n: pl.BlockSpec((TOKEN_TILE, n), lambda i: (cur(i), 0))
    tok = lambda n, dt: jax.ShapeDtypeStruct((t, n), dt)
    bsz, pairs, groups = t // seq, ATT_WIDTH // LANES, LRU_WIDTH // LANES
    qk_spec = pl.BlockSpec((None, pairs, TOKEN_TILE, LANES),
                           lambda i: (cur(i) // tiles_per_seq, 0, cur(i) % tiles_per_seq, 0))
    qk_shape = jax.ShapeDtypeStruct((bsz, pairs, seq, LANES), BF16)
    v_spec = pl.BlockSpec((None, ATT_HEADS, ATT_HEAD_DIM, TOKEN_TILE),
                          lambda i: (cur(i) // tiles_per_seq, 0, 0, cur(i) % tiles_per_seq))
    v_shape = jax.ShapeDtypeStruct((bsz, ATT_HEADS, ATT_HEAD_DIM, seq), BF16)
    rec_spec = pl.BlockSpec((TOKEN_TILE, LRU_WIDTH), lambda i: (jnp.maximum(i - 1, 0), 0))
    return pl.pallas_call(
        functools.partial(_proj_kernel, tiles_per_seq=tiles_per_seq, n_tiles=n_tiles),
        grid=(n_tiles + 1,),
        in_specs=[row(d),
                  pl.BlockSpec((None, 3 * N_SUB, d), lambda i: (cur(i) // tiles_per_seq, 0, 0)),
                  _resident(norm_pre.shape), _resident(w_in.shape),
                  _resident(conv_w.shape), _resident(conv_b.shape), _resident(w2.shape),
                  _resident(ba.shape), _resident(bx.shape), _resident(lam.shape)],
        out_specs=[qk_spec, qk_spec, v_spec, rec_spec, row(D_MODEL), row(D_MODEL)],
        out_shape=[qk_shape, qk_shape, v_shape, tok(LRU_WIDTH, BF16),
                   tok(D_MODEL, BF16), tok(D_MODEL, BF16)],
        scratch_shapes=[pltpu.VMEM((groups, TOKEN_TILE, LANES), F32),
                        pltpu.VMEM((TOKEN_TILE, LRU_WIDTH), F32),
                        pltpu.VMEM((groups, CONV_WIDTH - 1, SUBLANES, LANES), F32),
                        pltpu.VMEM((groups, SUBLANES, LANES), F32),
                        pltpu.VMEM((groups, LRU_TILE, LANES), F32)],
        compiler_params=_params("arbitrary"),
        name="mixer_proj",
    )(x2, mod, norm_pre, w_in, conv_w, conv_b, w2, ba, bx, lam)


REL_PAD = 384
TOEPLITZ = 1024


def _bias_kernel(tab_ref, o_ref):
    tab = tab_ref[...]
    hi = tab.astype(BF16)
    r1 = tab - hi.astype(F32)
    mid = r1.astype(BF16)
    lo = (r1 - mid.astype(F32)).astype(BF16)
    d_idx = lax.broadcasted_iota(jnp.int32, (REL_PAD, TOEPLITZ), 0)
    m_idx = lax.broadcasted_iota(jnp.int32, (REL_PAD, TOEPLITZ), 1)
    rel = jnp.clip(m_idx - (Q_TILE - 1), -MAX_REL, MAX_REL) + MAX_REL
    onehot = jnp.where(d_idx == rel, 1.0, 0.0).astype(BF16)
    profile = ((_dot(lo, onehot) + _dot(mid, onehot)) + _dot(hi, onehot)) * LOG2_E

    kk = lax.broadcasted_iota(jnp.int32, (BAND, Q_TILE), 0)
    r = lax.broadcasted_iota(jnp.int32, (BAND, Q_TILE), 1)
    qc = r // CHUNK
    kc = kk // CHUNK
    band = jnp.where((kc >= qc) & (kc <= qc + LEFT_CHUNKS), 0.0, MASK_VALUE)
    for h in range(ATT_HEADS):
        rows = jnp.broadcast_to(profile[h:h + 1, :], (BAND, TOEPLITZ))
        skew = pltpu.roll(rows, TOEPLITZ - (BAND - 1), 1, stride=1, stride_axis=0)
        o_ref[h] = skew[:, :Q_TILE] + band


def _bias_table(rel_bias):
    tab = jnp.pad(rel_bias, ((0, 0), (0, REL_PAD - rel_bias.shape[1])))
    return pl.pallas_call(
        _bias_kernel,
        out_shape=jax.ShapeDtypeStruct((ATT_HEADS, BAND, Q_TILE), F32),
        compiler_params=pltpu.CompilerParams(vmem_limit_bytes=VMEM_LIMIT_BYTES),
        name="rel_bias",
    )(tab)


def _attn_kernel(q_ref, *refs):
    n_kv = K_BLOCKS - 1 + Q_STEP_TILES
    k_refs, v_refs = refs[:n_kv], refs[n_kv:2 * n_kv]
    bias_ref, o_ref, acc_ref = refs[2 * n_kv:2 * n_kv + 3]
    s_refs, p_refs = refs[-4:-2], refs[-2:]
    first_tile = pl.program_id(1) * Q_STEP_TILES
    lane_head = lax.broadcasted_iota(jnp.int32, (1, LANES), 1) // ATT_HEAD_DIM
    zero = jnp.zeros((), BF16)
    live = [(2 * c * CHUNK, (2 * c + 2 + LEFT_CHUNKS) * CHUNK) for c in range(Q_TILE // LANES)]

    @pl.when((pl.program_id(0) == 0) & (pl.program_id(1) == 0))
    def _():
        for p_ref in p_refs:
            p_ref[...] = jnp.zeros_like(p_ref)

    def start_pens(u):
        return [jnp.where(first_tile + u + j < K_BLOCKS - 1, MASK_VALUE, 0.0)
                for j in range(K_BLOCKS - 1)] + [0.0]

    def scores(u, h, slot):
        pair = h // 2
        qh = jnp.where(lane_head == h % 2, q_ref[pair, u * Q_TILE:(u + 1) * Q_TILE, :], zero)
        kcat = jnp.concatenate([r[pair] for r in k_refs[u:u + K_BLOCKS]], axis=0)
        s = lax.dot_general(kcat, qh, (((1,), (1,)), ((), ())), preferred_element_type=F32)
        s_refs[slot][...] = s + bias_ref[h]

    def col_reduce(parts, op):
        while len(parts) > 1:
            parts = [op(parts[i], parts[i + 1]) if i + 1 < len(parts) else parts[i]
                     for i in range(0, len(parts), 2)]
        return parts[0]

    def softmax(u, slot):
        s_ref, p_ref = s_refs[slot], p_refs[slot]
        pens = start_pens(u)
        groups = CHUNK // SUBLANES
        inv = []
        for c, (r0, r1) in enumerate(live):
            cl = slice(c * LANES, (c + 1) * LANES)
            blocks = range(r0, r1, CHUNK)
            part = [s_ref[r:r + CHUNK, cl].reshape(groups, SUBLANES, LANES).max(axis=0)
                    + pens[r // Q_TILE] for r in blocks]
            m = jnp.max(col_reduce(part, jnp.maximum), axis=0, keepdims=True)
            shifted = [m - pen for pen in pens]
            sums = []
            for r in blocks:
                pr = jnp.exp2(s_ref[r:r + CHUNK, cl] - shifted[r // Q_TILE])
                sums.append(pr.reshape(groups, SUBLANES, LANES).sum(axis=0))
                p_ref[r:r + CHUNK, cl] = pr.astype(BF16)
            l = jnp.sum(col_reduce(sums, jnp.add), axis=0, keepdims=True)
            inv.append(1.0 / l)
        return jnp.concatenate(inv, axis=1)

    def weighted_values(u, h, slot, inv_l):
        v_t = jnp.concatenate([r[h] for r in v_refs[u:u + K_BLOCKS]], axis=1)
        acc_ref[u, h] = _dot(v_t, p_refs[slot][...]) * inv_l

    items = [(u, h) for u in range(Q_STEP_TILES) for h in range(ATT_HEADS)]
    scores(*items[0], 0)
    inv_prev = None
    for n, (u, h) in enumerate(items):
        if n + 1 < len(items):
            scores(*items[n + 1], (n + 1) % 2)
        inv_l = softmax(u, n % 2)
        if n > 0:
            weighted_values(*items[n - 1], (n - 1) % 2, inv_prev)
        inv_prev = inv_l
    weighted_values(*items[-1], (len(items) - 1) % 2, inv_prev)
    for u in range(Q_STEP_TILES):
        for p in range(ATT_HEADS // 2):
            pair_t = acc_ref[u, 2 * p:2 * p + 2].reshape(LANES, Q_TILE)
            o_ref[u * Q_TILE:(u + 1) * Q_TILE, p * LANES:(p + 1) * LANES] = pair_t.T.astype(BF16)


def _attention(q, k, v_t, bias):
    bsz, pairs, seq, _ = q.shape
    step_rows = Q_STEP_TILES * Q_TILE
    rel = range(-(K_BLOCKS - 1), Q_STEP_TILES)
    blk = lambda t, n: jnp.maximum(t * Q_STEP_TILES + n, 0)
    key = lambda n: pl.BlockSpec((None, pairs, Q_TILE, LANES), lambda b, t: (b, 0, blk(t, n), 0))
    val = lambda n: pl.BlockSpec((None, ATT_HEADS, ATT_HEAD_DIM, Q_TILE),
                                 lambda b, t: (b, 0, 0, blk(t, n)))
    slot = lambda dt: pltpu.VMEM((BAND, Q_TILE), dt)
    return pl.pallas_call(
        _attn_kernel,
        grid=(bsz, seq // step_rows),
        in_specs=[pl.BlockSpec((None, pairs, step_rows, LANES), lambda b, t: (b, 0, t, 0))]
                 + [key(n) for n in rel] + [val(n) for n in rel] + [_resident(bias.shape)],
        out_specs=pl.BlockSpec((None, step_rows, ATT_WIDTH), lambda b, t: (b, t, 0)),
        out_shape=jax.ShapeDtypeStruct((bsz, seq, ATT_WIDTH), BF16),
        scratch_shapes=[pltpu.VMEM((Q_STEP_TILES, ATT_HEADS, ATT_HEAD_DIM, Q_TILE), F32),
                        slot(F32), slot(F32), slot(BF16), slot(BF16)],
        compiler_params=_params("arbitrary", "arbitrary"),
        name="chunk_attn",
    )(q, *([k] * len(rel)), *([v_t] * len(rel)), bias)


def _pair_block_diag(wa, wx):
    def pairs(w):
        z = jnp.zeros_like(w[0::2])
        top = jnp.concatenate([w[0::2], z], axis=2)
        bot = jnp.concatenate([z, w[1::2]], axis=2)
        return jnp.concatenate([top, bot], axis=1)
    return jnp.concatenate([pairs(wa), pairs(wx)], axis=2)


def _mixout_kernel(x_ref, mod_ref, gpost_ref, att_ref, rec_ref, sa_ref, sr_ref,
                   wao_ref, wro_ref, wout_ref, *rest):
    n_cast = (len(rest) - 1) // 2
    o_ref = rest[n_cast]
    _cast_slabs(rest[:n_cast], rest[n_cast + 1:])
    _, _, gate = _mod_rows(mod_ref, 1)
    att = _dot(att_ref[...], wao_ref[...])
    rec = _dot(rec_ref[...], wro_ref[...])
    merged = sa_ref[...].astype(F32) * att + sr_ref[...].astype(F32) * rec
    y = _dot(merged.astype(BF16), wout_ref[...])
    o_ref[...] = x_ref[...] + gate * _rms(y, gpost_ref[1:2, :])


def _mixout(x2, mod, norm_post, att, rec, sa, sr, w_att_o, w_rec_o, w_out, *, seq, layer,
            casts=()):
    t, d = x2.shape
    tiles_per_seq = seq // MIX_TOKEN_TILE
    n_steps = t // MIX_TOKEN_TILE
    row = lambda n: pl.BlockSpec((MIX_TOKEN_TILE, n), lambda i: (i, 0))
    cast_in, cast_specs, cast_shapes = _cast_streams(casts, layer, n_steps)
    out, *cast_out = pl.pallas_call(
        _mixout_kernel,
        grid=(n_steps,),
        in_specs=[row(d),
                  pl.BlockSpec((None, 3 * N_SUB, d), lambda i: (i // tiles_per_seq, 0, 0)),
                  _resident(norm_post.shape),
                  row(ATT_WIDTH), row(LRU_WIDTH), row(d), row(d),
                  _resident(w_att_o.shape), _resident(w_rec_o.shape), _resident(w_out.shape)]
                 + cast_in,
        out_specs=[row(d)] + cast_specs,
        out_shape=[jax.ShapeDtypeStruct((t, d), F32)] + cast_shapes,
        compiler_params=_params("arbitrary"),
        name="mixer_out",
    )(x2, mod, norm_post, att, rec, sa, sr, w_att_o, w_rec_o, w_out, *casts)
    return out, cast_out


def _layer(x2, c, l, bsz, seq, w_ada, b_ada, norm_pre, norm_post, ffn1_w_gu, ffn1_w_down, w_in,
           rel_bias, conv_w, conv_b, lru_wa, lru_ba, lru_wx, lru_bx, lru_lambda, w_att_o,
           w_rec_o, w_out, ffn2_w_gu, ffn2_w_down):
    bf = lambda w: w[l].astype(BF16)
    mod = _ada(c, w_ada[l], b_ada[l]).reshape(bsz, 3 * N_SUB, D_MODEL)
    npre, npost = norm_pre[l], norm_post[l]

    x2, (w_in_b, w_att_o_b, w_rec_o_b, w_out_b, w_down2_b) = _ffn(
        x2, mod, npre, npost, bf(ffn1_w_gu), bf(ffn1_w_down), sub=0, res_w=0.5, seq=seq, layer=l,
        casts=(w_in, w_att_o, w_rec_o, w_out, ffn2_w_down))

    w2 = (0.5 * _pair_block_diag(lru_wa[l], lru_wx[l])).astype(BF16)
    vec = lambda p: p[l].reshape(1, LRU_WIDTH)
    q, k, v, rec, sa, sr = _proj(x2, mod, npre, w_in_b, conv_w[l], vec(conv_b), w2,
                                 0.5 * vec(lru_ba), 0.5 * vec(lru_bx), vec(lru_lambda), seq=seq)
    att = _attention(q, k, v, _bias_table(rel_bias[l]))
    x2, (w_gu2_b,) = _mixout(
        x2, mod, npost, att.reshape(bsz * seq, ATT_WIDTH), rec, sa, sr,
        w_att_o_b, w_rec_o_b, w_out_b, seq=seq, layer=l, casts=(ffn2_w_gu,))

    return _ffn(x2, mod, npre, npost, w_gu2_b, w_down2_b, sub=2, res_w=0.5, seq=seq, layer=l)[0]


def kernel(x, c, w_ada, b_ada, norm_pre, norm_post, ffn1_w_gu, ffn1_w_down, w_in, rel_bias, conv_w, conv_b, lru_wa, lru_ba, lru_wx, lru_bx, lru_lambda, w_att_o, w_rec_o, w_out, ffn2_w_gu, ffn2_w_down):
    bsz, seq, d = x.shape
    tiles = (TOKEN_TILE, MIX_TOKEN_TILE, FFN_TOKEN_TILE, Q_TILE * Q_STEP_TILES)
    assert d == D_MODEL and all(seq % n == 0 for n in tiles)
    assert TOKEN_TILE % LRU_TILE == 0
    x2 = x.reshape(bsz * seq, d)
    for l in range(w_ada.shape[0]):
        x2 = _layer(x2, c, l, bsz, seq, w_ada, b_ada, norm_pre, norm_post, ffn1_w_gu,
                    ffn1_w_down, w_in, rel_bias, conv_w, conv_b, lru_wa, lru_ba, lru_wx, lru_bx,
                    lru_lambda, w_att_o, w_rec_o, w_out, ffn2_w_gu, ffn2_w_down)
    return x2.reshape(bsz, seq, d)
```

```python
import functools
import math

import jax
import jax.numpy as jnp
from jax import lax
from jax.experimental import pallas as pl
from jax.experimental.pallas import tpu as pltpu

D_MODEL = 1024
CHUNK = 64
LEFT_CHUNKS = 8
ATT_HEADS = 8
ATT_HEAD_DIM = 64
ATT_WIDTH = ATT_HEADS * ATT_HEAD_DIM
MAX_REL = 128
LRU_WIDTH = D_MODEL
LRU_BLOCKS = 16
LRU_BLOCK = LRU_WIDTH // LRU_BLOCKS
CONV_WIDTH = 4
LRU_C = 8.0
D_FF = 2816
N_SUB = 3
EPS = 1e-6

LANES = 128
SUBLANES = 8
BF16_ROWS = 16
VMEM_LIMIT_BYTES = 56 * 1024 * 1024

TOKEN_TILE = 512
MIX_TOKEN_TILE = 1024
MIX_SUB_TILE = 256
FFN_TOKEN_TILE = 1024
FFN_SUB_TILE = 256
FF_TILE = 256
PROJ_CHUNK = 256
ADA_TILE = 1152
Q_TILE = 4 * CHUNK
Q_STEP_TILES = 8
K_BLOCKS = LEFT_CHUNKS * CHUNK // Q_TILE + 1
BAND = K_BLOCKS * Q_TILE
LRU_TILE = 256
LRU_STEPS = LRU_TILE // SUBLANES
MASK_VALUE = -1e30
LOG2_E = math.log2(math.e)
BF16 = jnp.bfloat16
F32 = jnp.float32


def _dot(a, b):
    return jnp.dot(a, b, preferred_element_type=F32)


def _rms(x, g):
    return x * lax.rsqrt(jnp.mean(x * x, axis=-1, keepdims=True) + EPS) * g


def _modulated_norm(x, g, shift, scale):
    inv = lax.rsqrt(jnp.mean(x * x, axis=-1, keepdims=True) + EPS)
    return x * inv * (g * (1.0 + scale)) + shift


def _sigmoid(x):
    return 1.0 / (1.0 + jnp.exp(-x))


def _gelu_tanh(x):
    c = math.sqrt(2.0 / math.pi)
    return x * (0.5 * (1.0 + jnp.tanh(c * (x + 0.044715 * (x * x * x)))))


def _mod_rows(mod_ref, sub):
    shift = mod_ref[3 * sub:3 * sub + 1, :]
    scale = mod_ref[3 * sub + 1:3 * sub + 2, :]
    gate = mod_ref[3 * sub + 2:3 * sub + 3, :]
    return shift, scale, gate


def _resident(shape):
    nd = len(shape)
    return pl.BlockSpec(shape, lambda *_: (0,) * nd, pipeline_mode=pl.Buffered(1))


def _params(*semantics):
    return pltpu.CompilerParams(dimension_semantics=semantics,
                                vmem_limit_bytes=VMEM_LIMIT_BYTES)


def _cast_streams(stacked, layer, n_steps):
    for w in stacked:
        assert w.shape[1] % (BF16_ROWS * n_steps) == 0 and w.shape[2] % LANES == 0, w.shape
    in_specs = [pl.BlockSpec((None, w.shape[1] // n_steps, w.shape[2]), lambda i: (layer, i, 0))
                for w in stacked]
    out_specs = [pl.BlockSpec((w.shape[1] // n_steps, w.shape[2]), lambda i: (i, 0))
                 for w in stacked]
    shapes = [jax.ShapeDtypeStruct(w.shape[1:], BF16) for w in stacked]
    return in_specs, out_specs, shapes


def _cast_slabs(in_refs, out_refs):
    for src, dst in zip(in_refs, out_refs):
        dst[...] = src[...].astype(BF16)


def _ada_kernel(c_ref, w_ref, b_ref, *rest):
    n_cast = (len(rest) - 1) // 2
    o_ref = rest[n_cast]
    _cast_slabs(rest[:n_cast], rest[n_cast + 1:])
    c = c_ref[...]
    c_act = (c * _sigmoid(c)).astype(BF16)
    o_ref[...] = _dot(c_act, w_ref[...].astype(BF16)) + b_ref[...]


def _ada(c, w_ada, b_ada, *, layer, casts=()):
    bsz, d = c.shape
    n = w_ada.shape[2]
    n_steps = n // ADA_TILE
    cast_in, cast_specs, cast_shapes = _cast_streams(casts, layer, n_steps)
    out, *cast_out = pl.pallas_call(
        _ada_kernel,
        grid=(n_steps,),
        in_specs=[pl.BlockSpec((bsz, d), lambda j: (0, 0)),
                  pl.BlockSpec((None, d, ADA_TILE), lambda j: (layer, 0, j)),
                  pl.BlockSpec((1, ADA_TILE), lambda j: (0, j))] + cast_in,
        out_specs=[pl.BlockSpec((bsz, ADA_TILE), lambda j: (0, j))] + cast_specs,
        out_shape=[jax.ShapeDtypeStruct((bsz, n), F32)] + cast_shapes,
        compiler_params=_params("arbitrary"),
        name="adaln",
    )(c, w_ada, b_ada[layer].reshape(1, n), *casts)
    return out, cast_out


def _ffn_kernel(x_ref, mod_ref, gpre_ref, gpost_ref, wgu_ref, wdn_ref, *rest, sub, res_w,
                next_sub):
    n_out = 1 if next_sub is None else 2
    n_cast = (len(rest) - 3 - n_out) // 2
    o_ref = rest[n_cast]
    next_ref = None if next_sub is None else rest[n_cast + 1]
    h_ref, act_ref, y_ref = rest[-3:]
    _cast_slabs(rest[:n_cast], rest[n_cast + n_out:-3])
    shift, scale, gate = _mod_rows(mod_ref, sub)
    n_sub = x_ref.shape[0] // FFN_SUB_TILE

    def rows(s):
        return slice(s * FFN_SUB_TILE, (s + 1) * FFN_SUB_TILE)

    def prologue(s):
        x = x_ref[rows(s), :]
        h_ref[s % 2] = _modulated_norm(x, gpre_ref[sub:sub + 1, :], shift, scale).astype(BF16)

    def gate_up(s, j):
        lo = j * FF_TILE
        h = h_ref[s % 2]
        g = _dot(h, wgu_ref[:, lo:lo + FF_TILE])
        u = _dot(h, wgu_ref[:, D_FF + lo:D_FF + lo + FF_TILE])
        act_ref[s % 2, :, lo:lo + FF_TILE] = (g * _sigmoid(g) * u).astype(BF16)

    def down(s):
        y_ref[s % 2] = _dot(act_ref[s % 2], wdn_ref[...])

    def epilogue(s):
        y = _rms(y_ref[s % 2], gpost_ref[sub:sub + 1, :])
        out = x_ref[rows(s), :] + (res_w * gate) * y
        o_ref[rows(s), :] = out
        if next_sub is not None:
            nshift, nscale, _ = _mod_rows(mod_ref, next_sub)
            g_next = gpre_ref[next_sub:next_sub + 1, :]
            next_ref[rows(s), :] = _modulated_norm(out, g_next, nshift, nscale).astype(BF16)

    n_ff = D_FF // FF_TILE
    prologue(0)
    for s in range(n_sub):
        for j in range(n_ff):
            gate_up(s, j)
            if j == 0 and s > 0:
                epilogue(s - 1)
            if j == n_ff // 2 and s + 1 < n_sub:
                prologue(s + 1)
        down(s)
    epilogue(n_sub - 1)


def _ffn(x2, mod, norm_pre, norm_post, w_gu, w_down, *, sub, res_w, seq, layer, next_sub=None,
         casts=()):
    t, d = x2.shape
    tiles_per_seq = seq // FFN_TOKEN_TILE
    n_steps = t // FFN_TOKEN_TILE
    cast_in, cast_specs, cast_shapes = _cast_streams(casts, layer, n_steps)
    tile = pl.BlockSpec((FFN_TOKEN_TILE, d), lambda i: (i, 0))
    next_specs = [] if next_sub is None else [tile]
    next_shapes = [] if next_sub is None else [jax.ShapeDtypeStruct((t, d), BF16)]
    out, *extra = pl.pallas_call(
        functools.partial(_ffn_kernel, sub=sub, res_w=res_w, next_sub=next_sub),
        grid=(n_steps,),
        in_specs=[tile,
                  pl.BlockSpec((None, 3 * N_SUB, d), lambda i: (i // tiles_per_seq, 0, 0)),
                  _resident(norm_pre.shape),
                  _resident(norm_post.shape),
                  _resident(w_gu.shape),
                  _resident(w_down.shape)] + cast_in,
        out_specs=[tile] + next_specs + cast_specs,
        out_shape=[jax.ShapeDtypeStruct((t, d), F32)] + next_shapes + cast_shapes,
        scratch_shapes=[pltpu.VMEM((2, FFN_SUB_TILE, d), BF16),
                        pltpu.VMEM((2, FFN_SUB_TILE, D_FF), BF16),
                        pltpu.VMEM((2, FFN_SUB_TILE, d), F32)],
        compiler_params=_params("arbitrary"),
        name=f"ffn{sub}",
    )(x2, mod, norm_pre, norm_post, w_gu, w_down, *casts)
    h_next = None if next_sub is None else extra.pop(0)
    return out, h_next, extra


def _lru_unit(g, xr_ref, row0, gy_ref, o_ref, fresh, cw_ref, cb_ref, w2_ref, ba_ref, bx_ref, decay,
              tail_ref, h_ref, hs_ref):
    sub = lax.broadcasted_iota(jnp.int32, (SUBLANES, LANES), 0)
    taps = CONV_WIDTH - 1
    sl = slice(g * LANES, (g + 1) * LANES)
    x3 = xr_ref[g, row0:row0 + LRU_TILE, :].reshape(LRU_STEPS, SUBLANES, LANES)
    cur_tail = x3[LRU_STEPS - taps:]
    lead = pltpu.roll(jnp.where(sub == SUBLANES - 1, tail_ref[g] * fresh, cur_tail), 1, 1)
    tail_ref[g] = cur_tail
    xext = jnp.concatenate([lead, x3], axis=0)
    xc3 = cb_ref[:, sl] + cw_ref[0:1, sl] * xext[0:LRU_STEPS]
    for w in range(1, CONV_WIDTH):
        xc3 = xc3 + cw_ref[w:w + 1, sl] * xext[w:w + LRU_STEPS]
    xc = xc3.reshape(LRU_TILE, LANES)
    z = _dot(xc.astype(BF16), w2_ref[g])
    ta = jnp.tanh(z[:, :LANES] + ba_ref[:, sl])
    tx = jnp.tanh(z[:, LANES:] + bx_ref[:, sl])
    log_a = decay[:, sl] * (ta + 1.0)
    a = jnp.exp(log_a)
    th = jnp.tanh(log_a)
    q = (-0.5 * th) / (1.0 - th)
    half_mult = jnp.where(q > 0.0, q * lax.rsqrt(q), 0.0)
    u = half_mult * ((tx + 1.0) * xc)
    a3 = a.reshape(LRU_STEPS, SUBLANES, LANES)
    u3 = u.reshape(LRU_STEPS, SUBLANES, LANES)
    loc = [u3[0]]
    prod = [a3[0]]
    for i in range(1, LRU_STEPS):
        loc.append(a3[i] * loc[-1] + u3[i])
        prod.append(a3[i] * prod[-1])
    blk_a, blk_u = prod[-1], loc[-1]
    for s in (1, 2, 4):
        keep = sub >= s
        a_prev = jnp.where(keep, pltpu.roll(blk_a, s, 0), 1.0)
        u_prev = jnp.where(keep, pltpu.roll(blk_u, s, 0), 0.0)
        blk_u = blk_u + blk_a * u_prev
        blk_a = blk_a * a_prev
    h_in = h_ref[g] * fresh
    end = blk_u + blk_a * h_in
    entry = jnp.where(sub == 0, h_in, pltpu.roll(end, 1, 0))
    h_ref[g] = jnp.broadcast_to(end[SUBLANES - 1:SUBLANES, :], (SUBLANES, LANES))
    for i in range(LRU_STEPS):
        hs_ref[g, i * SUBLANES:(i + 1) * SUBLANES, :] = loc[i] + prod[i] * entry
    rows = []
    for n in range(LRU_TILE // SUBLANES):
        j, i0 = divmod(n * SUBLANES, LRU_STEPS)
        rows.append(hs_ref[g, pl.ds(i0 * SUBLANES + j, SUBLANES, stride=SUBLANES), :])
    hs = jnp.concatenate(rows, axis=0)
    o_ref[row0:row0 + LRU_TILE, sl] = (hs * gy_ref[row0:row0 + LRU_TILE, sl]).astype(BF16)


def _proj_kernel(hin_ref, w_ref, cw_ref, cb_ref, w2_ref, ba_ref, bx_ref, lam_ref,
                 q_ref, k_ref, v_ref, rec_ref, sa_ref, sr_ref,
                 xr_s, gy_s, tail_ref, h_ref, hs_ref, *, tiles_per_seq, n_tiles):
    i = pl.program_id(0)
    rows = hin_ref.shape[0]
    cw = PROJ_CHUNK
    per_chunk = cw // LANES
    lru0 = 3 * ATT_WIDTH
    n_att, n_lru = ATT_WIDTH // cw, LRU_WIDTH // cw

    @pl.when(i == 0)
    def _():
        for ref in (xr_s, gy_s, tail_ref, h_ref):
            ref[...] = jnp.zeros_like(ref)

    def scan_units():
        neg_lam = -lam_ref[...]
        softplus = jnp.maximum(neg_lam, 0.0) + jnp.log1p(jnp.exp(-jnp.abs(neg_lam)))
        decay = (-0.5 * LRU_C) * softplus
        first = jnp.where(lax.rem(i - 1, tiles_per_seq) == 0, 0.0, 1.0)

        def unit(row0, g):
            _lru_unit(g, xr_s, row0, gy_s, rec_ref, first if row0 == 0 else 1.0, cw_ref, cb_ref,
                      w2_ref, ba_ref, bx_ref, decay, tail_ref, h_ref, hs_ref)

        return [functools.partial(unit, row0, g)
                for row0 in range(0, rows, LRU_TILE) for g in range(LRU_WIDTH // LANES)]

    @pl.when(i < n_tiles)
    def _():
        def proj(col):
            return _dot(hin_ref[...], w_ref[:, col:col + cw])

        def q_chunk(c):
            q = (proj(c * cw) * (ATT_HEAD_DIM ** -0.5 * LOG2_E)).astype(BF16)
            for p in range(per_chunk):
                q_ref[c * per_chunk + p] = q[:, p * LANES:(p + 1) * LANES]

        def k_chunk(c):
            k = proj(ATT_WIDTH + c * cw).astype(BF16)
            for p in range(per_chunk):
                k_ref[c * per_chunk + p] = k[:, p * LANES:(p + 1) * LANES]

        def v_chunk(c):
            heads = cw // ATT_HEAD_DIM
            v_t = proj(2 * ATT_WIDTH + c * cw).T.astype(BF16)
            v_ref[c * heads:(c + 1) * heads] = v_t.reshape(heads, ATT_HEAD_DIM, rows)

        def gate_chunk(ref, col0, c):
            ref[:, c * cw:(c + 1) * cw] = _sigmoid(proj(col0 + c * cw)).astype(BF16)

        def gy_chunk(c):
            gy_s[:, c * cw:(c + 1) * cw] = _gelu_tanh(proj(lru0 + LRU_WIDTH + c * cw))

        def xr_chunk(c):
            xr = proj(lru0 + c * cw)
            for src_row in range(0, rows, LRU_STEPS):
                tile0, j = src_row // LRU_TILE * LRU_TILE, src_row % LRU_TILE // LRU_STEPS
                for p in range(per_chunk):
                    xr_s[c * per_chunk + p, pl.ds(tile0 + j, LRU_STEPS, stride=SUBLANES), :] = (
                        xr[src_row:src_row + LRU_STEPS, p * LANES:(p + 1) * LANES])

        chunks = ([functools.partial(q_chunk, c) for c in range(n_att)]
                  + [functools.partial(k_chunk, c) for c in range(n_att)]
                  + [functools.partial(v_chunk, c) for c in range(n_att)]
                  + [functools.partial(gate_chunk, sa_ref, lru0 + 2 * LRU_WIDTH, c) for c in range(n_lru)]
                  + [functools.partial(gate_chunk, sr_ref, lru0 + 3 * LRU_WIDTH, c) for c in range(n_lru)])
        units = scan_units()
        done = 0
        for n, chunk in enumerate(chunks):
            chunk()
            target = -(-(n + 1) * len(units) // len(chunks))
            while done < target:
                units[done]()
                done += 1
        for c in range(n_lru):
            gy_chunk(c)
        for c in range(n_lru):
            xr_chunk(c)

    @pl.when(i == n_tiles)
    def _():
        for unit in scan_units():
            unit()


def _proj(h_in, w_in, conv_w, conv_b, w2, ba, bx, lam, *, seq):
    t, d = h_in.shape
    tiles_per_seq = seq // TOKEN_TILE
    n_tiles = t // TOKEN_TILE
    cur = lambda i: jnp.minimum(i, n_tiles - 1)
    row = lambda n: pl.BlockSpec((TOKEN_TILE, n), lambda i: (cur(i), 0))
    tok = lambda n, dt: jax.ShapeDtypeStruct((t, n), dt)
    bsz, pairs, groups = t // seq, ATT_WIDTH // LANES, LRU_WIDTH // LANES
    qk_spec = pl.BlockSpec((None, pairs, TOKEN_TILE, LANES),
                           lambda i: (cur(i) // tiles_per_seq, 0, cur(i) % tiles_per_seq, 0))
    qk_shape = jax.ShapeDtypeStruct((bsz, pairs, seq, LANES), BF16)
    v_spec = pl.BlockSpec((None, ATT_HEADS, ATT_HEAD_DIM, TOKEN_TILE),
                          lambda i: (cur(i) // tiles_per_seq, 0, 0, cur(i) % tiles_per_seq))
    v_shape = jax.ShapeDtypeStruct((bsz, ATT_HEADS, ATT_HEAD_DIM, seq), BF16)
    rec_spec = pl.BlockSpec((TOKEN_TILE, LRU_WIDTH), lambda i: (jnp.maximum(i - 1, 0), 0))
    return pl.pallas_call(
        functools.partial(_proj_kernel, tiles_per_seq=tiles_per_seq, n_tiles=n_tiles),
        grid=(n_tiles + 1,),
        in_specs=[row(d), _resident(w_in.shape),
                  _resident(conv_w.shape), _resident(conv_b.shape), _resident(w2.shape),
                  _resident(ba.shape), _resident(bx.shape), _resident(lam.shape)],
        out_specs=[qk_spec, qk_spec, v_spec, rec_spec, row(D_MODEL), row(D_MODEL)],
        out_shape=[qk_shape, qk_shape, v_shape, tok(LRU_WIDTH, BF16),
                   tok(D_MODEL, BF16), tok(D_MODEL, BF16)],
        scratch_shapes=[pltpu.VMEM((groups, TOKEN_TILE, LANES), F32),
                        pltpu.VMEM((TOKEN_TILE, LRU_WIDTH), F32),
                        pltpu.VMEM((groups, CONV_WIDTH - 1, SUBLANES, LANES), F32),
                        pltpu.VMEM((groups, SUBLANES, LANES), F32),
                        pltpu.VMEM((groups, LRU_TILE, LANES), F32)],
        compiler_params=_params("arbitrary"),
        name="mixer_proj",
    )(h_in, w_in, conv_w, conv_b, w2, ba, bx, lam)


REL_PAD = 384
TOEPLITZ = 1024


def _bias_kernel(tab_ref, o_ref):
    tab = tab_ref[...]
    hi = tab.astype(BF16)
    r1 = tab - hi.astype(F32)
    mid = r1.astype(BF16)
    lo = (r1 - mid.astype(F32)).astype(BF16)
    d_idx = lax.broadcasted_iota(jnp.int32, (REL_PAD, TOEPLITZ), 0)
    m_idx = lax.broadcasted_iota(jnp.int32, (REL_PAD, TOEPLITZ), 1)
    rel = jnp.clip(m_idx - (Q_TILE - 1), -MAX_REL, MAX_REL) + MAX_REL
    onehot = jnp.where(d_idx == rel, 1.0, 0.0).astype(BF16)
    profile = ((_dot(lo, onehot) + _dot(mid, onehot)) + _dot(hi, onehot)) * LOG2_E

    kk = lax.broadcasted_iota(jnp.int32, (BAND, Q_TILE), 0)
    r = lax.broadcasted_iota(jnp.int32, (BAND, Q_TILE), 1)
    qc = r // CHUNK
    kc = kk // CHUNK
    band = jnp.where((kc >= qc) & (kc <= qc + LEFT_CHUNKS), 0.0, MASK_VALUE)
    for h in range(ATT_HEADS):
        rows = jnp.broadcast_to(profile[h:h + 1, :], (BAND, TOEPLITZ))
        skew = pltpu.roll(rows, TOEPLITZ - (BAND - 1), 1, stride=1, stride_axis=0)
        o_ref[h] = skew[:, :Q_TILE] + band


def _bias_table(rel_bias):
    tab = jnp.pad(rel_bias, ((0, 0), (0, REL_PAD - rel_bias.shape[1])))
    return pl.pallas_call(
        _bias_kernel,
        out_shape=jax.ShapeDtypeStruct((ATT_HEADS, BAND, Q_TILE), F32),
        compiler_params=pltpu.CompilerParams(vmem_limit_bytes=VMEM_LIMIT_BYTES),
        name="rel_bias",
    )(tab)


def _attn_kernel(q_ref, *refs):
    n_kv = K_BLOCKS - 1 + Q_STEP_TILES
    k_refs, v_refs = refs[:n_kv], refs[n_kv:2 * n_kv]
    bias_ref, o_ref, acc_ref = refs[2 * n_kv:2 * n_kv + 3]
    s_refs, p_refs = refs[-4:-2], refs[-2:]
    first_tile = pl.program_id(1) * Q_STEP_TILES
    lane_head = lax.broadcasted_iota(jnp.int32, (1, LANES), 1) // ATT_HEAD_DIM
    zero = jnp.zeros((), BF16)
    live = [(2 * c * CHUNK, (2 * c + 2 + LEFT_CHUNKS) * CHUNK) for c in range(Q_TILE // LANES)]

    @pl.when((pl.program_id(0) == 0) & (pl.program_id(1) == 0))
    def _():
        for p_ref in p_refs:
            p_ref[...] = jnp.zeros_like(p_ref)

    def start_pens(u):
        return [jnp.where(first_tile + u + j < K_BLOCKS - 1, MASK_VALUE, 0.0)
                for j in range(K_BLOCKS - 1)] + [0.0]

    def scores(u, h, slot):
        pair = h // 2
        qh = jnp.where(lane_head == h % 2, q_ref[pair, u * Q_TILE:(u + 1) * Q_TILE, :], zero)
        kcat = jnp.concatenate([r[pair] for r in k_refs[u:u + K_BLOCKS]], axis=0)
        s = lax.dot_general(kcat, qh, (((1,), (1,)), ((), ())), preferred_element_type=F32)
        s_refs[slot][...] = s + bias_ref[h]

    def col_reduce(parts, op):
        while len(parts) > 1:
            parts = [op(parts[i], parts[i + 1]) if i + 1 < len(parts) else parts[i]
                     for i in range(0, len(parts), 2)]
        return parts[0]

    def softmax(u, slot):
        s_ref, p_ref = s_refs[slot], p_refs[slot]
        pens = start_pens(u)
        groups = CHUNK // SUBLANES
        inv = []
        for c, (r0, r1) in enumerate(live):
            cl = slice(c * LANES, (c + 1) * LANES)
            blocks = range(r0, r1, CHUNK)
            part = [s_ref[r:r + CHUNK, cl].reshape(groups, SUBLANES, LANES).max(axis=0)
                    + pens[r // Q_TILE] for r in blocks]
            m = jnp.max(col_reduce(part, jnp.maximum), axis=0, keepdims=True)
            shifted = [m - pen for pen in pens]
            sums = []
            for r in blocks:
                pr = jnp.exp2(s_ref[r:r + CHUNK, cl] - shifted[r // Q_TILE])
                sums.append(pr.reshape(groups, SUBLANES, LANES).sum(axis=0))
                p_ref[r:r + CHUNK, cl] = pr.astype(BF16)
            l = jnp.sum(col_reduce(sums, jnp.add), axis=0, keepdims=True)
            inv.append(1.0 / l)
        return jnp.concatenate(inv, axis=1)

    def weighted_values(u, h, slot, inv_l):
        v_t = jnp.concatenate([r[h] for r in v_refs[u:u + K_BLOCKS]], axis=1)
        acc_ref[u, h] = _dot(v_t, p_refs[slot][...]) * inv_l

    items = [(u, h) for u in range(Q_STEP_TILES) for h in range(ATT_HEADS)]
    scores(*items[0], 0)
    inv_prev = None
    for n, (u, h) in enumerate(items):
        if n + 1 < len(items):
            scores(*items[n + 1], (n + 1) % 2)
        inv_l = softmax(u, n % 2)
        if n > 0:
            weighted_values(*items[n - 1], (n - 1) % 2, inv_prev)
        inv_prev = inv_l
    weighted_values(*items[-1], (len(items) - 1) % 2, inv_prev)
    for u in range(Q_STEP_TILES):
        for p in range(ATT_HEADS // 2):
            pair_t = acc_ref[u, 2 * p:2 * p + 2].reshape(LANES, Q_TILE)
            o_ref[u * Q_TILE:(u + 1) * Q_TILE, p * LANES:(p + 1) * LANES] = pair_t.T.astype(BF16)


def _attention(q, k, v_t, bias):
    bsz, pairs, seq, _ = q.shape
    step_rows = Q_STEP_TILES * Q_TILE
    rel = range(-(K_BLOCKS - 1), Q_STEP_TILES)
    blk = lambda t, n: jnp.maximum(t * Q_STEP_TILES + n, 0)
    key = lambda n: pl.BlockSpec((None, pairs, Q_TILE, LANES), lambda b, t: (b, 0, blk(t, n), 0))
    val = lambda n: pl.BlockSpec((None, ATT_HEADS, ATT_HEAD_DIM, Q_TILE),
                                 lambda b, t: (b, 0, 0, blk(t, n)))
    slot = lambda dt: pltpu.VMEM((BAND, Q_TILE), dt)
    return pl.pallas_call(
        _attn_kernel,
        grid=(bsz, seq // step_rows),
        in_specs=[pl.BlockSpec((None, pairs, step_rows, LANES), lambda b, t: (b, 0, t, 0))]
                 + [key(n) for n in rel] + [val(n) for n in rel] + [_resident(bias.shape)],
        out_specs=pl.BlockSpec((None, step_rows, ATT_WIDTH), lambda b, t: (b, t, 0)),
        out_shape=jax.ShapeDtypeStruct((bsz, seq, ATT_WIDTH), BF16),
        scratch_shapes=[pltpu.VMEM((Q_STEP_TILES, ATT_HEADS, ATT_HEAD_DIM, Q_TILE), F32),
                        slot(F32), slot(F32), slot(BF16), slot(BF16)],
        compiler_params=_params("arbitrary", "arbitrary"),
        name="chunk_attn",
    )(q, *([k] * len(rel)), *([v_t] * len(rel)), bias)


def _pair_block_diag(wa, wx):
    def pairs(w):
        z = jnp.zeros_like(w[0::2])
        top = jnp.concatenate([w[0::2], z], axis=2)
        bot = jnp.concatenate([z, w[1::2]], axis=2)
        return jnp.concatenate([top, bot], axis=1)
    return jnp.concatenate([pairs(wa), pairs(wx)], axis=2)


def _mixout_kernel(x_ref, mod_ref, gpost_ref, att_ref, rec_ref, sa_ref, sr_ref,
                   wao_ref, wro_ref, wout_ref, *rest):
    n_cast = (len(rest) - 2) // 2
    o_ref, y_ref = rest[n_cast], rest[-1]
    _cast_slabs(rest[:n_cast], rest[n_cast + 1:-1])
    _, _, gate = _mod_rows(mod_ref, 1)
    n_sub = x_ref.shape[0] // MIX_SUB_TILE

    def rows(s):
        return slice(s * MIX_SUB_TILE, (s + 1) * MIX_SUB_TILE)

    def project(s):
        att = _dot(att_ref[rows(s), :], wao_ref[...])
        rec = _dot(rec_ref[rows(s), :], wro_ref[...])
        merged = sa_ref[rows(s), :].astype(F32) * att + sr_ref[rows(s), :].astype(F32) * rec
        y_ref[s % 2] = _dot(merged.astype(BF16), wout_ref[...])

    def epilogue(s):
        o_ref[rows(s), :] = x_ref[rows(s), :] + gate * _rms(y_ref[s % 2], gpost_ref[1:2, :])

    for s in range(n_sub):
        project(s)
        if s > 0:
            epilogue(s - 1)
    epilogue(n_sub - 1)


def _mixout(x2, mod, norm_post, att, rec, sa, sr, w_att_o, w_rec_o, w_out, *, seq, layer,
            casts=()):
    t, d = x2.shape
    tiles_per_seq = seq // MIX_TOKEN_TILE
    n_steps = t // MIX_TOKEN_TILE
    row = lambda n: pl.BlockSpec((MIX_TOKEN_TILE, n), lambda i: (i, 0))
    cast_in, cast_specs, cast_shapes = _cast_streams(casts, layer, n_steps)
    out, *cast_out = pl.pallas_call(
        _mixout_kernel,
        grid=(n_steps,),
        in_specs=[row(d),
                  pl.BlockSpec((None, 3 * N_SUB, d), lambda i: (i // tiles_per_seq, 0, 0)),
                  _resident(norm_post.shape),
                  row(ATT_WIDTH), row(LRU_WIDTH), row(d), row(d),
                  _resident(w_att_o.shape), _resident(w_rec_o.shape), _resident(w_out.shape)]
                 + cast_in,
        out_specs=[row(d)] + cast_specs,
        out_shape=[jax.ShapeDtypeStruct((t, d), F32)] + cast_shapes,
        scratch_shapes=[pltpu.VMEM((2, MIX_SUB_TILE, d), F32)],
        compiler_params=_params("arbitrary"),
        name="mixer_out",
    )(x2, mod, norm_post, att, rec, sa, sr, w_att_o, w_rec_o, w_out, *casts)
    return out, cast_out


def _layer(x2, c, l, bsz, seq, w_ada, b_ada, norm_pre, norm_post, ffn1_w_gu, ffn1_w_down, w_in,
           rel_bias, conv_w, conv_b, lru_wa, lru_ba, lru_wx, lru_bx, lru_lambda, w_att_o,
           w_rec_o, w_out, ffn2_w_gu, ffn2_w_down):
    mod, (w_gu1_b, w_down1_b) = _ada(c, w_ada, b_ada, layer=l, casts=(ffn1_w_gu, ffn1_w_down))
    mod = mod.reshape(bsz, 3 * N_SUB, D_MODEL)
    npre, npost = norm_pre[l], norm_post[l]

    x2, h_mix, (w_in_b, w_att_o_b, w_rec_o_b, w_out_b, w_down2_b) = _ffn(
        x2, mod, npre, npost, w_gu1_b, w_down1_b, sub=0, res_w=0.5, seq=seq, layer=l, next_sub=1,
        casts=(w_in, w_att_o, w_rec_o, w_out, ffn2_w_down))

    w2 = (0.5 * _pair_block_diag(lru_wa[l], lru_wx[l])).astype(BF16)
    vec = lambda p: p[l].reshape(1, LRU_WIDTH)
    q, k, v, rec, sa, sr = _proj(h_mix, w_in_b, conv_w[l], vec(conv_b), w2,
                                 0.5 * vec(lru_ba), 0.5 * vec(lru_bx), vec(lru_lambda), seq=seq)
    att = _attention(q, k, v, _bias_table(rel_bias[l]))
    x2, (w_gu2_b,) = _mixout(
        x2, mod, npost, att.reshape(bsz * seq, ATT_WIDTH), rec, sa, sr,
        w_att_o_b, w_rec_o_b, w_out_b, seq=seq, layer=l, casts=(ffn2_w_gu,))

    return _ffn(x2, mod, npre, npost, w_gu2_b, w_down2_b, sub=2, res_w=0.5, seq=seq, layer=l)[0]


def kernel(x, c, w_ada, b_ada, norm_pre, norm_post, ffn1_w_gu, ffn1_w_down, w_in, rel_bias, conv_w, conv_b, lru_wa, lru_ba, lru_wx, lru_bx, lru_lambda, w_att_o, w_rec_o, w_out, ffn2_w_gu, ffn2_w_down):
    bsz, seq, d = x.shape
    tiles = (TOKEN_TILE, MIX_TOKEN_TILE, FFN_TOKEN_TILE, Q_TILE * Q_STEP_TILES)
    assert d == D_MODEL and all(seq % n == 0 for n in tiles)
    assert TOKEN_TILE % LRU_TILE == 0
    x2 = x.reshape(bsz * seq, d)
    for l in range(w_ada.shape[0]):
        x2 = _layer(x2, c, l, bsz, seq, w_ada, b_ada, norm_pre, norm_post, ffn1_w_gu,
                    ffn1_w_down, w_in, rel_bias, conv_w, conv_b, lru_wa, lru_ba, lru_wx, lru_bx,
                    lru_lambda, w_att_o, w_rec_o, w_out, ffn2_w_gu, ffn2_w_down)
    return x2.reshape(bsz, seq, d)
```

```python
import functools
import math

import jax
import jax.numpy as jnp
from jax import lax
from jax.experimental import pallas as pl
from jax.experimental.pallas import tpu as pltpu

D_MODEL = 1024
CHUNK = 64
LEFT_CHUNKS = 8
ATT_HEADS = 8
ATT_HEAD_DIM = 64
ATT_WIDTH = ATT_HEADS * ATT_HEAD_DIM
MAX_REL = 128
LRU_WIDTH = D_MODEL
LRU_BLOCKS = 16
LRU_BLOCK = LRU_WIDTH // LRU_BLOCKS
CONV_WIDTH = 4
LRU_C = 8.0
D_FF = 2816
N_SUB = 3
EPS = 1e-6

LANES = 128
SUBLANES = 8
BF16_ROWS = 16
VMEM_LIMIT_BYTES = 56 * 1024 * 1024

TOKEN_TILE = 512
MIX_TOKEN_TILE = 1024
MIX_SUB_TILE = 256
FFN_TOKEN_TILE = 1024
FFN_SUB_TILE = 256
FF_TILE = 256
PROJ_CHUNK = 256
ADA_TILE = 1152
Q_TILE = 4 * CHUNK
Q_STEP_TILES = 8
K_BLOCKS = LEFT_CHUNKS * CHUNK // Q_TILE + 1
BAND = K_BLOCKS * Q_TILE
LRU_TILE = 256
LRU_STEPS = LRU_TILE // SUBLANES
MASK_VALUE = -1e30
LOG2_E = math.log2(math.e)
BF16 = jnp.bfloat16
F32 = jnp.float32


def _dot(a, b):
    return jnp.dot(a, b, preferred_element_type=F32)


def _rms(x, g):
    return x * lax.rsqrt(jnp.mean(x * x, axis=-1, keepdims=True) + EPS) * g


def _modulated_norm(x, g, shift, scale):
    inv = lax.rsqrt(jnp.mean(x * x, axis=-1, keepdims=True) + EPS)
    return x * inv * (g * (1.0 + scale)) + shift


def _sigmoid(x):
    return 1.0 / (1.0 + jnp.exp(-x))


def _gelu_tanh(x):
    c = math.sqrt(2.0 / math.pi)
    return x * (0.5 * (1.0 + jnp.tanh(c * (x + 0.044715 * (x * x * x)))))


def _mod_rows(mod_ref, sub):
    shift = mod_ref[3 * sub:3 * sub + 1, :]
    scale = mod_ref[3 * sub + 1:3 * sub + 2, :]
    gate = mod_ref[3 * sub + 2:3 * sub + 3, :]
    return shift, scale, gate


def _resident(shape):
    nd = len(shape)
    return pl.BlockSpec(shape, lambda *_: (0,) * nd, pipeline_mode=pl.Buffered(1))


def _params(*semantics):
    return pltpu.CompilerParams(dimension_semantics=semantics,
                                vmem_limit_bytes=VMEM_LIMIT_BYTES)


def _cast_streams(stacked, layer, n_steps):
    for w in stacked:
        assert w.shape[1] % (BF16_ROWS * n_steps) == 0 and w.shape[2] % LANES == 0, w.shape
    in_specs = [pl.BlockSpec((None, w.shape[1] // n_steps, w.shape[2]), lambda i: (layer, i, 0))
                for w in stacked]
    out_specs = [pl.BlockSpec((w.shape[1] // n_steps, w.shape[2]), lambda i: (i, 0))
                 for w in stacked]
    shapes = [jax.ShapeDtypeStruct(w.shape[1:], BF16) for w in stacked]
    return in_specs, out_specs, shapes


def _cast_slabs(in_refs, out_refs):
    for src, dst in zip(in_refs, out_refs):
        dst[...] = src[...].astype(BF16)


def _ada_kernel(c_ref, w_ref, b_ref, *rest):
    n_cast = (len(rest) - 1) // 2
    o_ref = rest[n_cast]
    _cast_slabs(rest[:n_cast], rest[n_cast + 1:])
    c = c_ref[...]
    c_act = (c * _sigmoid(c)).astype(BF16)
    o_ref[...] = _dot(c_act, w_ref[...].astype(BF16)) + b_ref[...]


def _ada(c, w_ada, b_ada, *, layer, casts=()):
    bsz, d = c.shape
    n = w_ada.shape[2]
    n_steps = n // ADA_TILE
    cast_in, cast_specs, cast_shapes = _cast_streams(casts, layer, n_steps)
    out, *cast_out = pl.pallas_call(
        _ada_kernel,
        grid=(n_steps,),
        in_specs=[pl.BlockSpec((bsz, d), lambda j: (0, 0)),
                  pl.BlockSpec((None, d, ADA_TILE), lambda j: (layer, 0, j)),
                  pl.BlockSpec((1, ADA_TILE), lambda j: (0, j))] + cast_in,
        out_specs=[pl.BlockSpec((bsz, ADA_TILE), lambda j: (0, j))] + cast_specs,
        out_shape=[jax.ShapeDtypeStruct((bsz, n), F32)] + cast_shapes,
        compiler_params=_params("arbitrary"),
        name="adaln",
    )(c, w_ada, b_ada[layer].reshape(1, n), *casts)
    return out, cast_out


def _ffn_kernel(x_ref, mod_ref, gpre_ref, gpost_ref, wgu_ref, wdn_ref, *rest, sub, res_w):
    n_cast = (len(rest) - 4) // 2
    o_ref = rest[n_cast]
    h_ref, act_ref, y_ref = rest[-3:]
    _cast_slabs(rest[:n_cast], rest[n_cast + 1:-3])
    shift, scale, gate = _mod_rows(mod_ref, sub)
    n_sub = x_ref.shape[0] // FFN_SUB_TILE

    def rows(s):
        return slice(s * FFN_SUB_TILE, (s + 1) * FFN_SUB_TILE)

    def prologue(s):
        x = x_ref[rows(s), :]
        h_ref[s % 2] = _modulated_norm(x, gpre_ref[sub:sub + 1, :], shift, scale).astype(BF16)

    def gate_up(s, j):
        lo = j * FF_TILE
        h = h_ref[s % 2]
        g = _dot(h, wgu_ref[:, lo:lo + FF_TILE])
        u = _dot(h, wgu_ref[:, D_FF + lo:D_FF + lo + FF_TILE])
        act_ref[s % 2, :, lo:lo + FF_TILE] = (g * _sigmoid(g) * u).astype(BF16)

    def down(s):
        y_ref[s % 2] = _dot(act_ref[s % 2], wdn_ref[...])

    def epilogue(s):
        y = _rms(y_ref[s % 2], gpost_ref[sub:sub + 1, :])
        o_ref[rows(s), :] = x_ref[rows(s), :] + (res_w * gate) * y

    n_ff = D_FF // FF_TILE
    prologue(0)
    for s in range(n_sub):
        for j in range(n_ff):
            gate_up(s, j)
            if j == 0 and s > 0:
                epilogue(s - 1)
            if j == n_ff // 2 and s + 1 < n_sub:
                prologue(s + 1)
        down(s)
    epilogue(n_sub - 1)


def _ffn(x2, mod, norm_pre, norm_post, w_gu, w_down, *, sub, res_w, seq, layer, casts=()):
    t, d = x2.shape
    tiles_per_seq = seq // FFN_TOKEN_TILE
    n_steps = t // FFN_TOKEN_TILE
    cast_in, cast_specs, cast_shapes = _cast_streams(casts, layer, n_steps)
    out, *cast_out = pl.pallas_call(
        functools.partial(_ffn_kernel, sub=sub, res_w=res_w),
        grid=(n_steps,),
        in_specs=[pl.BlockSpec((FFN_TOKEN_TILE, d), lambda i: (i, 0)),
                  pl.BlockSpec((None, 3 * N_SUB, d), lambda i: (i // tiles_per_seq, 0, 0)),
                  _resident(norm_pre.shape),
                  _resident(norm_post.shape),
                  _resident(w_gu.shape),
                  _resident(w_down.shape)] + cast_in,
        out_specs=[pl.BlockSpec((FFN_TOKEN_TILE, d), lambda i: (i, 0))] + cast_specs,
        out_shape=[jax.ShapeDtypeStruct((t, d), F32)] + cast_shapes,
        scratch_shapes=[pltpu.VMEM((2, FFN_SUB_TILE, d), BF16),
                        pltpu.VMEM((2, FFN_SUB_TILE, D_FF), BF16),
                        pltpu.VMEM((2, FFN_SUB_TILE, d), F32)],
        compiler_params=_params("arbitrary"),
        name=f"ffn{sub}",
    )(x2, mod, norm_pre, norm_post, w_gu, w_down, *casts)
    return out, cast_out


def _lru_unit(g, xr_ref, row0, gy_ref, o_ref, fresh, cw_ref, cb_ref, w2_ref, ba_ref, bx_ref, decay,
              tail_ref, h_ref, hs_ref):
    sub = lax.broadcasted_iota(jnp.int32, (SUBLANES, LANES), 0)
    taps = CONV_WIDTH - 1
    sl = slice(g * LANES, (g + 1) * LANES)
    x3 = xr_ref[g, row0:row0 + LRU_TILE, :].reshape(LRU_STEPS, SUBLANES, LANES)
    cur_tail = x3[LRU_STEPS - taps:]
    lead = pltpu.roll(jnp.where(sub == SUBLANES - 1, tail_ref[g] * fresh, cur_tail), 1, 1)
    tail_ref[g] = cur_tail
    xext = jnp.concatenate([lead, x3], axis=0)
    xc3 = cb_ref[:, sl] + cw_ref[0:1, sl] * xext[0:LRU_STEPS]
    for w in range(1, CONV_WIDTH):
        xc3 = xc3 + cw_ref[w:w + 1, sl] * xext[w:w + LRU_STEPS]
    xc = xc3.reshape(LRU_TILE, LANES)
    z = _dot(xc.astype(BF16), w2_ref[g])
    ta = jnp.tanh(z[:, :LANES] + ba_ref[:, sl])
    tx = jnp.tanh(z[:, LANES:] + bx_ref[:, sl])
    log_a = decay[:, sl] * (ta + 1.0)
    a = jnp.exp(log_a)
    th = jnp.tanh(log_a)
    q = (-0.5 * th) / (1.0 - th)
    half_mult = jnp.where(q > 0.0, q * lax.rsqrt(q), 0.0)
    u = half_mult * ((tx + 1.0) * xc)
    a3 = a.reshape(LRU_STEPS, SUBLANES, LANES)
    u3 = u.reshape(LRU_STEPS, SUBLANES, LANES)
    loc = [u3[0]]
    prod = [a3[0]]
    for i in range(1, LRU_STEPS):
        loc.append(a3[i] * loc[-1] + u3[i])
        prod.append(a3[i] * prod[-1])
    blk_a, blk_u = prod[-1], loc[-1]
    for s in (1, 2, 4):
        keep = sub >= s
        a_prev = jnp.where(keep, pltpu.roll(blk_a, s, 0), 1.0)
        u_prev = jnp.where(keep, pltpu.roll(blk_u, s, 0), 0.0)
        blk_u = blk_u + blk_a * u_prev
        blk_a = blk_a * a_prev
    h_in = h_ref[g] * fresh
    end = blk_u + blk_a * h_in
    entry = jnp.where(sub == 0, h_in, pltpu.roll(end, 1, 0))
    h_ref[g] = jnp.broadcast_to(end[SUBLANES - 1:SUBLANES, :], (SUBLANES, LANES))
    for i in range(LRU_STEPS):
        hs_ref[g, i * SUBLANES:(i + 1) * SUBLANES, :] = loc[i] + prod[i] * entry
    rows = []
    for n in range(LRU_TILE // SUBLANES):
        j, i0 = divmod(n * SUBLANES, LRU_STEPS)
        rows.append(hs_ref[g, pl.ds(i0 * SUBLANES + j, SUBLANES, stride=SUBLANES), :])
    hs = jnp.concatenate(rows, axis=0)
    o_ref[row0:row0 + LRU_TILE, sl] = (hs * gy_ref[row0:row0 + LRU_TILE, sl]).astype(BF16)


def _proj_kernel(x_ref, mod_ref, gpre_ref, w_ref, cw_ref, cb_ref, w2_ref, ba_ref, bx_ref, lam_ref,
                 q_ref, k_ref, v_ref, rec_ref, sa_ref, sr_ref,
                 xr_s, gy_s, tail_ref, h_ref, hs_ref, *, tiles_per_seq, n_tiles):
    i = pl.program_id(0)
    rows = x_ref.shape[0]
    cw = PROJ_CHUNK
    per_chunk = cw // LANES
    lru0 = 3 * ATT_WIDTH
    n_att, n_lru = ATT_WIDTH // cw, LRU_WIDTH // cw

    @pl.when(i == 0)
    def _():
        for ref in (xr_s, gy_s, tail_ref, h_ref):
            ref[...] = jnp.zeros_like(ref)

    def scan_units():
        neg_lam = -lam_ref[...]
        softplus = jnp.maximum(neg_lam, 0.0) + jnp.log1p(jnp.exp(-jnp.abs(neg_lam)))
        decay = (-0.5 * LRU_C) * softplus
        first = jnp.where(lax.rem(i - 1, tiles_per_seq) == 0, 0.0, 1.0)

        def unit(row0, g):
            _lru_unit(g, xr_s, row0, gy_s, rec_ref, first if row0 == 0 else 1.0, cw_ref, cb_ref,
                      w2_ref, ba_ref, bx_ref, decay, tail_ref, h_ref, hs_ref)

        return [functools.partial(unit, row0, g)
                for row0 in range(0, rows, LRU_TILE) for g in range(LRU_WIDTH // LANES)]

    @pl.when(i < n_tiles)
    def _():
        x = x_ref[...]
        shift, scale, _ = _mod_rows(mod_ref, 1)
        h = _modulated_norm(x, gpre_ref[1:2, :], shift, scale).astype(BF16)

        def proj(col):
            return _dot(h, w_ref[:, col:col + cw])

        def q_chunk(c):
            q = (proj(c * cw) * (ATT_HEAD_DIM ** -0.5 * LOG2_E)).astype(BF16)
            for p in range(per_chunk):
                q_ref[c * per_chunk + p] = q[:, p * LANES:(p + 1) * LANES]

        def k_chunk(c):
            k = proj(ATT_WIDTH + c * cw).astype(BF16)
            for p in range(per_chunk):
                k_ref[c * per_chunk + p] = k[:, p * LANES:(p + 1) * LANES]

        def v_chunk(c):
            heads = cw // ATT_HEAD_DIM
            v_t = proj(2 * ATT_WIDTH + c * cw).T.astype(BF16)
            v_ref[c * heads:(c + 1) * heads] = v_t.reshape(heads, ATT_HEAD_DIM, rows)

        def gate_chunk(ref, col0, c):
            ref[:, c * cw:(c + 1) * cw] = proj(col0 + c * cw).astype(BF16)

        def gy_chunk(c):
            gy_s[:, c * cw:(c + 1) * cw] = _gelu_tanh(proj(lru0 + LRU_WIDTH + c * cw))

        def xr_chunk(c):
            xr = proj(lru0 + c * cw)
            for src_row in range(0, rows, LRU_STEPS):
                tile0, j = src_row // LRU_TILE * LRU_TILE, src_row % LRU_TILE // LRU_STEPS
                for p in range(per_chunk):
                    xr_s[c * per_chunk + p, pl.ds(tile0 + j, LRU_STEPS, stride=SUBLANES), :] = (
                        xr[src_row:src_row + LRU_STEPS, p * LANES:(p + 1) * LANES])

        chunks = ([functools.partial(q_chunk, c) for c in range(n_att)]
                  + [functools.partial(k_chunk, c) for c in range(n_att)]
                  + [functools.partial(v_chunk, c) for c in range(n_att)]
                  + [functools.partial(gate_chunk, sa_ref, lru0 + 2 * LRU_WIDTH, c) for c in range(n_lru)]
                  + [functools.partial(gate_chunk, sr_ref, lru0 + 3 * LRU_WIDTH, c) for c in range(n_lru)]
                  + [functools.partial(gy_chunk, c) for c in range(n_lru)]
                  + [functools.partial(xr_chunk, c) for c in range(n_lru)])
        units = scan_units()
        for n, chunk in enumerate(chunks):
            chunk()
            if n < len(units):
                units[n]()

    @pl.when(i == n_tiles)
    def _():
        for unit in scan_units():
            unit()


def _proj(x2, mod, norm_pre, w_in, conv_w, conv_b, w2, ba, bx, lam, *, seq):
    t, d = x2.shape
    tiles_per_seq = seq // TOKEN_TILE
    n_tiles = t // TOKEN_TILE
    cur = lambda i: jnp.minimum(i, n_tiles - 1)
    row = lambda n: pl.BlockSpec((TOKEN_TILE, n), lambda i: (cur(i), 0))
    tok = lambda n, dt: jax.ShapeDtypeStruct((t, n), dt)
    bsz, pairs, groups = t // seq, ATT_WIDTH // LANES, LRU_WIDTH // LANES
    qk_spec = pl.BlockSpec((None, pairs, TOKEN_TILE, LANES),
                           lambda i: (cur(i) // tiles_per_seq, 0, cur(i) % tiles_per_seq, 0))
    qk_shape = jax.ShapeDtypeStruct((bsz, pairs, seq, LANES), BF16)
    v_spec = pl.BlockSpec((None, ATT_HEADS, ATT_HEAD_DIM, TOKEN_TILE),
                          lambda i: (cur(i) // tiles_per_seq, 0, 0, cur(i) % tiles_per_seq))
    v_shape = jax.ShapeDtypeStruct((bsz, ATT_HEADS, ATT_HEAD_DIM, seq), BF16)
    rec_spec = pl.BlockSpec((TOKEN_TILE, LRU_WIDTH), lambda i: (jnp.maximum(i - 1, 0), 0))
    return pl.pallas_call(
        functools.partial(_proj_kernel, tiles_per_seq=tiles_per_seq, n_tiles=n_tiles),
        grid=(n_tiles + 1,),
        in_specs=[row(d),
                  pl.BlockSpec((None, 3 * N_SUB, d), lambda i: (cur(i) // tiles_per_seq, 0, 0)),
                  _resident(norm_pre.shape), _resident(w_in.shape),
                  _resident(conv_w.shape), _resident(conv_b.shape), _resident(w2.shape),
                  _resident(ba.shape), _resident(bx.shape), _resident(lam.shape)],
        out_specs=[qk_spec, qk_spec, v_spec, rec_spec, row(D_MODEL), row(D_MODEL)],
        out_shape=[qk_shape, qk_shape, v_shape, tok(LRU_WIDTH, BF16),
                   tok(D_MODEL, BF16), tok(D_MODEL, BF16)],
        scratch_shapes=[pltpu.VMEM((groups, TOKEN_TILE, LANES), F32),
                        pltpu.VMEM((TOKEN_TILE, LRU_WIDTH), F32),
                        pltpu.VMEM((groups, CONV_WIDTH - 1, SUBLANES, LANES), F32),
                        pltpu.VMEM((groups, SUBLANES, LANES), F32),
                        pltpu.VMEM((groups, LRU_TILE, LANES), F32)],
        compiler_params=_params("arbitrary"),
        name="mixer_proj",
    )(x2, mod, norm_pre, w_in, conv_w, conv_b, w2, ba, bx, lam)


REL_PAD = 384
TOEPLITZ = 1024


def _bias_kernel(tab_ref, o_ref):
    tab = tab_ref[...]
    hi = tab.astype(BF16)
    r1 = tab - hi.astype(F32)
    mid = r1.astype(BF16)
    lo = (r1 - mid.astype(F32)).astype(BF16)
    d_idx = lax.broadcasted_iota(jnp.int32, (REL_PAD, TOEPLITZ), 0)
    m_idx = lax.broadcasted_iota(jnp.int32, (REL_PAD, TOEPLITZ), 1)
    rel = jnp.clip(m_idx - (Q_TILE - 1), -MAX_REL, MAX_REL) + MAX_REL
    onehot = jnp.where(d_idx == rel, 1.0, 0.0).astype(BF16)
    profile = ((_dot(lo, onehot) + _dot(mid, onehot)) + _dot(hi, onehot)) * LOG2_E

    kk = lax.broadcasted_iota(jnp.int32, (BAND, Q_TILE), 0)
    r = lax.broadcasted_iota(jnp.int32, (BAND, Q_TILE), 1)
    qc = r // CHUNK
    kc = kk // CHUNK
    band = jnp.where((kc >= qc) & (kc <= qc + LEFT_CHUNKS), 0.0, MASK_VALUE)
    for h in range(ATT_HEADS):
        rows = jnp.broadcast_to(profile[h:h + 1, :], (BAND, TOEPLITZ))
        skew = pltpu.roll(rows, TOEPLITZ - (BAND - 1), 1, stride=1, stride_axis=0)
        o_ref[h] = skew[:, :Q_TILE] + band


def _bias_table(rel_bias):
    tab = jnp.pad(rel_bias, ((0, 0), (0, REL_PAD - rel_bias.shape[1])))
    return pl.pallas_call(
        _bias_kernel,
        out_shape=jax.ShapeDtypeStruct((ATT_HEADS, BAND, Q_TILE), F32),
        compiler_params=pltpu.CompilerParams(vmem_limit_bytes=VMEM_LIMIT_BYTES),
        name="rel_bias",
    )(tab)


def _attn_kernel(q_ref, *refs):
    n_kv = K_BLOCKS - 1 + Q_STEP_TILES
    k_refs, v_refs = refs[:n_kv], refs[n_kv:2 * n_kv]
    bias_ref, o_ref, acc_ref = refs[2 * n_kv:2 * n_kv + 3]
    s_refs, p_refs = refs[-4:-2], refs[-2:]
    first_tile = pl.program_id(1) * Q_STEP_TILES
    lane_head = lax.broadcasted_iota(jnp.int32, (1, LANES), 1) // ATT_HEAD_DIM
    zero = jnp.zeros((), BF16)
    live = [(2 * c * CHUNK, (2 * c + 2 + LEFT_CHUNKS) * CHUNK) for c in range(Q_TILE // LANES)]

    @pl.when((pl.program_id(0) == 0) & (pl.program_id(1) == 0))
    def _():
        for p_ref in p_refs:
            p_ref[...] = jnp.zeros_like(p_ref)

    def start_pens(u):
        return [jnp.where(first_tile + u + j < K_BLOCKS - 1, MASK_VALUE, 0.0)
                for j in range(K_BLOCKS - 1)] + [0.0]

    def scores(u, h, slot):
        pair = h // 2
        qh = jnp.where(lane_head == h % 2, q_ref[pair, u * Q_TILE:(u + 1) * Q_TILE, :], zero)
        kcat = jnp.concatenate([r[pair] for r in k_refs[u:u + K_BLOCKS]], axis=0)
        s = lax.dot_general(kcat, qh, (((1,), (1,)), ((), ())), preferred_element_type=F32)
        s_refs[slot][...] = s + bias_ref[h]

    def col_reduce(parts, op):
        while len(parts) > 1:
            parts = [op(parts[i], parts[i + 1]) if i + 1 < len(parts) else parts[i]
                     for i in range(0, len(parts), 2)]
        return parts[0]

    def softmax(u, slot):
        s_ref, p_ref = s_refs[slot], p_refs[slot]
        pens = start_pens(u)
        groups = CHUNK // SUBLANES
        inv = []
        for c, (r0, r1) in enumerate(live):
            cl = slice(c * LANES, (c + 1) * LANES)
            blocks = range(r0, r1, CHUNK)
            part = [s_ref[r:r + CHUNK, cl].reshape(groups, SUBLANES, LANES).max(axis=0)
                    + pens[r // Q_TILE] for r in blocks]
            m = jnp.max(col_reduce(part, jnp.maximum), axis=0, keepdims=True)
            shifted = [m - pen for pen in pens]
            sums = []
            for r in blocks:
                pr = jnp.exp2(s_ref[r:r + CHUNK, cl] - shifted[r // Q_TILE])
                sums.append(pr.reshape(groups, SUBLANES, LANES).sum(axis=0))
                p_ref[r:r + CHUNK, cl] = pr.astype(BF16)
            l = jnp.sum(col_reduce(sums, jnp.add), axis=0, keepdims=True)
            inv.append(1.0 / l)
        return jnp.concatenate(inv, axis=1)

    def weighted_values(u, h, slot, inv_l):
        v_t = jnp.concatenate([r[h] for r in v_refs[u:u + K_BLOCKS]], axis=1)
        acc_ref[u, h] = _dot(v_t, p_refs[slot][...]) * inv_l

    items = [(u, h) for u in range(Q_STEP_TILES) for h in range(ATT_HEADS)]
    scores(*items[0], 0)
    inv_prev = None
    for n, (u, h) in enumerate(items):
        if n + 1 < len(items):
            scores(*items[n + 1], (n + 1) % 2)
        inv_l = softmax(u, n % 2)
        if n > 0:
            weighted_values(*items[n - 1], (n - 1) % 2, inv_prev)
        inv_prev = inv_l
    weighted_values(*items[-1], (len(items) - 1) % 2, inv_prev)
    for u in range(Q_STEP_TILES):
        for p in range(ATT_HEADS // 2):
            pair_t = acc_ref[u, 2 * p:2 * p + 2].reshape(LANES, Q_TILE)
            o_ref[u * Q_TILE:(u + 1) * Q_TILE, p * LANES:(p + 1) * LANES] = pair_t.T.astype(BF16)


def _attention(q, k, v_t, bias):
    bsz, pairs, seq, _ = q.shape
    step_rows = Q_STEP_TILES * Q_TILE
    rel = range(-(K_BLOCKS - 1), Q_STEP_TILES)
    blk = lambda t, n: jnp.maximum(t * Q_STEP_TILES + n, 0)
    key = lambda n: pl.BlockSpec((None, pairs, Q_TILE, LANES), lambda b, t: (b, 0, blk(t, n), 0))
    val = lambda n: pl.BlockSpec((None, ATT_HEADS, ATT_HEAD_DIM, Q_TILE),
                                 lambda b, t: (b, 0, 0, blk(t, n)))
    slot = lambda dt: pltpu.VMEM((BAND, Q_TILE), dt)
    return pl.pallas_call(
        _attn_kernel,
        grid=(bsz, seq // step_rows),
        in_specs=[pl.BlockSpec((None, pairs, step_rows, LANES), lambda b, t: (b, 0, t, 0))]
                 + [key(n) for n in rel] + [val(n) for n in rel] + [_resident(bias.shape)],
        out_specs=pl.BlockSpec((None, step_rows, ATT_WIDTH), lambda b, t: (b, t, 0)),
        out_shape=jax.ShapeDtypeStruct((bsz, seq, ATT_WIDTH), BF16),
        scratch_shapes=[pltpu.VMEM((Q_STEP_TILES, ATT_HEADS, ATT_HEAD_DIM, Q_TILE), F32),
                        slot(F32), slot(F32), slot(BF16), slot(BF16)],
        compiler_params=_params("arbitrary", "arbitrary"),
        name="chunk_attn",
    )(q, *([k] * len(rel)), *([v_t] * len(rel)), bias)


def _pair_block_diag(wa, wx):
    def pairs(w):
        z = jnp.zeros_like(w[0::2])
        top = jnp.concatenate([w[0::2], z], axis=2)
        bot = jnp.concatenate([z, w[1::2]], axis=2)
        return jnp.concatenate([top, bot], axis=1)
    return jnp.concatenate([pairs(wa), pairs(wx)], axis=2)


def _mixout_kernel(x_ref, mod_ref, gpost_ref, att_ref, rec_ref, sa_ref, sr_ref,
                   wao_ref, wro_ref, wout_ref, *rest):
    n_cast = (len(rest) - 2) // 2
    o_ref, y_ref = rest[n_cast], rest[-1]
    _cast_slabs(rest[:n_cast], rest[n_cast + 1:-1])
    _, _, gate = _mod_rows(mod_ref, 1)
    n_sub = x_ref.shape[0] // MIX_SUB_TILE

    def rows(s):
        return slice(s * MIX_SUB_TILE, (s + 1) * MIX_SUB_TILE)

    def project(s):
        att = _dot(att_ref[rows(s), :], wao_ref[...])
        rec = _dot(rec_ref[rows(s), :], wro_ref[...])
        gate_att = _sigmoid(sa_ref[rows(s), :].astype(F32))
        gate_rec = _sigmoid(sr_ref[rows(s), :].astype(F32))
        merged = gate_att * att + gate_rec * rec
        y_ref[s % 2] = _dot(merged.astype(BF16), wout_ref[...])

    def epilogue(s):
        o_ref[rows(s), :] = x_ref[rows(s), :] + gate * _rms(y_ref[s % 2], gpost_ref[1:2, :])

    for s in range(n_sub):
        project(s)
        if s > 0:
            epilogue(s - 1)
    epilogue(n_sub - 1)


def _mixout(x2, mod, norm_post, att, rec, sa, sr, w_att_o, w_rec_o, w_out, *, seq, layer,
            casts=()):
    t, d = x2.shape
    tiles_per_seq = seq // MIX_TOKEN_TILE
    n_steps = t // MIX_TOKEN_TILE
    row = lambda n: pl.BlockSpec((MIX_TOKEN_TILE, n), lambda i: (i, 0))
    cast_in, cast_specs, cast_shapes = _cast_streams(casts, layer, n_steps)
    out, *cast_out = pl.pallas_call(
        _mixout_kernel,
        grid=(n_steps,),
        in_specs=[row(d),
                  pl.BlockSpec((None, 3 * N_SUB, d), lambda i: (i // tiles_per_seq, 0, 0)),
                  _resident(norm_post.shape),
                  row(ATT_WIDTH), row(LRU_WIDTH), row(d), row(d),
                  _resident(w_att_o.shape), _resident(w_rec_o.shape), _resident(w_out.shape)]
                 + cast_in,
        out_specs=[row(d)] + cast_specs,
        out_shape=[jax.ShapeDtypeStruct((t, d), F32)] + cast_shapes,
        scratch_shapes=[pltpu.VMEM((2, MIX_SUB_TILE, d), F32)],
        compiler_params=_params("arbitrary"),
        name="mixer_out",
    )(x2, mod, norm_post, att, rec, sa, sr, w_att_o, w_rec_o, w_out, *casts)
    return out, cast_out


def _layer(x2, c, l, bsz, seq, w_ada, b_ada, norm_pre, norm_post, ffn1_w_gu, ffn1_w_down, w_in,
           rel_bias, conv_w, conv_b, lru_wa, lru_ba, lru_wx, lru_bx, lru_lambda, w_att_o,
           w_rec_o, w_out, ffn2_w_gu, ffn2_w_down):
    mod, (w_gu1_b, w_down1_b) = _ada(c, w_ada, b_ada, layer=l, casts=(ffn1_w_gu, ffn1_w_down))
    mod = mod.reshape(bsz, 3 * N_SUB, D_MODEL)
    npre, npost = norm_pre[l], norm_post[l]

    x2, (w_in_b, w_att_o_b, w_rec_o_b, w_out_b, w_down2_b) = _ffn(
        x2, mod, npre, npost, w_gu1_b, w_down1_b, sub=0, res_w=0.5, seq=seq, layer=l,
        casts=(w_in, w_att_o, w_rec_o, w_out, ffn2_w_down))

    w2 = (0.5 * _pair_block_diag(lru_wa[l], lru_wx[l])).astype(BF16)
    vec = lambda p: p[l].reshape(1, LRU_WIDTH)
    q, k, v, rec, sa, sr = _proj(x2, mod, npre, w_in_b, conv_w[l], vec(conv_b), w2,
                                 0.5 * vec(lru_ba), 0.5 * vec(lru_bx), vec(lru_lambda), seq=seq)
    att = _attention(q, k, v, _bias_table(rel_bias[l]))
    x2, (w_gu2_b,) = _mixout(
        x2, mod, npost, att.reshape(bsz * seq, ATT_WIDTH), rec, sa, sr,
        w_att_o_b, w_rec_o_b, w_out_b, seq=seq, layer=l, casts=(ffn2_w_gu,))

    return _ffn(x2, mod, npre, npost, w_gu2_b, w_down2_b, sub=2, res_w=0.5, seq=seq, layer=l)[0]


def kernel(x, c, w_ada, b_ada, norm_pre, norm_post, ffn1_w_gu, ffn1_w_down, w_in, rel_bias, conv_w, conv_b, lru_wa, lru_ba, lru_wx, lru_bx, lru_lambda, w_att_o, w_rec_o, w_out, ffn2_w_gu, ffn2_w_down):
    bsz, seq, d = x.shape
    tiles = (TOKEN_TILE, MIX_TOKEN_TILE, FFN_TOKEN_TILE, Q_TILE * Q_STEP_TILES)
    assert d == D_MODEL and all(seq % n == 0 for n in tiles)
    assert TOKEN_TILE % LRU_TILE == 0
    x2 = x.reshape(bsz * seq, d)
    for l in range(w_ada.shape[0]):
        x2 = _layer(x2, c, l, bsz, seq, w_ada, b_ada, norm_pre, norm_post, ffn1_w_gu,
                    ffn1_w_down, w_in, rel_bias, conv_w, conv_b, lru_wa, lru_ba, lru_wx, lru_bx,
                    lru_lambda, w_att_o, w_rec_o, w_out, ffn2_w_gu, ffn2_w_down)
    return x2.reshape(bsz, seq, d)
```

```python
import functools
import math

import jax
import jax.numpy as jnp
from jax import lax
from jax.experimental import pallas as pl
from jax.experimental.pallas import tpu as pltpu

D_MODEL = 1024
CHUNK = 64
LEFT_CHUNKS = 8
ATT_HEADS = 8
ATT_HEAD_DIM = 64
ATT_WIDTH = ATT_HEADS * ATT_HEAD_DIM
MAX_REL = 128
LRU_WIDTH = D_MODEL
LRU_BLOCKS = 16
LRU_BLOCK = LRU_WIDTH // LRU_BLOCKS
CONV_WIDTH = 4
LRU_C = 8.0
D_FF = 2816
N_SUB = 3
EPS = 1e-6

LANES = 128
SUBLANES = 8
BF16_ROWS = 16
VMEM_LIMIT_BYTES = 56 * 1024 * 1024

TOKEN_TILE = 512
MIX_TOKEN_TILE = 1024
MIX_SUB_TILE = 512
FFN_TOKEN_TILE = 1024
FFN_SUB_TILE = 256
FF_TILE = 256
PROJ_CHUNK = 256
ADA_TILE = 1152
Q_TILE = 4 * CHUNK
Q_STEP_TILES = 8
K_BLOCKS = LEFT_CHUNKS * CHUNK // Q_TILE + 1
BAND = K_BLOCKS * Q_TILE
LRU_TILE = 256
LRU_STEPS = LRU_TILE // SUBLANES
MASK_VALUE = -1e30
LOG2_E = math.log2(math.e)
BF16 = jnp.bfloat16
F32 = jnp.float32


def _dot(a, b):
    return jnp.dot(a, b, preferred_element_type=F32)


def _rms(x, g):
    return x * lax.rsqrt(jnp.mean(x * x, axis=-1, keepdims=True) + EPS) * g


def _modulated_norm(x, g, shift, scale):
    inv = lax.rsqrt(jnp.mean(x * x, axis=-1, keepdims=True) + EPS)
    return x * inv * (g * (1.0 + scale)) + shift


def _sigmoid(x):
    return 1.0 / (1.0 + jnp.exp(-x))


def _gelu_tanh(x):
    c = -2.0 * LOG2_E * math.sqrt(2.0 / math.pi)
    return x / (1.0 + jnp.exp2(x * (c + (0.044715 * c) * (x * x))))


def _mod_rows(mod_ref, sub):
    shift = mod_ref[3 * sub:3 * sub + 1, :]
    scale = mod_ref[3 * sub + 1:3 * sub + 2, :]
    gate = mod_ref[3 * sub + 2:3 * sub + 3, :]
    return shift, scale, gate


def _resident(shape):
    nd = len(shape)
    return pl.BlockSpec(shape, lambda *_: (0,) * nd, pipeline_mode=pl.Buffered(1))


def _params(*semantics):
    return pltpu.CompilerParams(dimension_semantics=semantics,
                                vmem_limit_bytes=VMEM_LIMIT_BYTES)


def _cast_streams(stacked, layer, n_steps):
    for w in stacked:
        assert w.shape[1] % (BF16_ROWS * n_steps) == 0 and w.shape[2] % LANES == 0, w.shape
    in_specs = [pl.BlockSpec((None, w.shape[1] // n_steps, w.shape[2]), lambda i: (layer, i, 0))
                for w in stacked]
    out_specs = [pl.BlockSpec((w.shape[1] // n_steps, w.shape[2]), lambda i: (i, 0))
                 for w in stacked]
    shapes = [jax.ShapeDtypeStruct(w.shape[1:], BF16) for w in stacked]
    return in_specs, out_specs, shapes


def _cast_slabs(in_refs, out_refs):
    for src, dst in zip(in_refs, out_refs):
        dst[...] = src[...].astype(BF16)


def _ada_kernel(c_ref, w_ref, b_ref, *rest):
    n_cast = (len(rest) - 1) // 2
    o_ref = rest[n_cast]
    _cast_slabs(rest[:n_cast], rest[n_cast + 1:])
    c = c_ref[...]
    c_act = (c * _sigmoid(c)).astype(BF16)
    o_ref[...] = _dot(c_act, w_ref[...].astype(BF16)) + b_ref[...]


def _ada(c, w_ada, b_ada, *, layer, casts=()):
    bsz, d = c.shape
    n = w_ada.shape[2]
    n_steps = n // ADA_TILE
    cast_in, cast_specs, cast_shapes = _cast_streams(casts, layer, n_steps)
    out, *cast_out = pl.pallas_call(
        _ada_kernel,
        grid=(n_steps,),
        in_specs=[pl.BlockSpec((bsz, d), lambda j: (0, 0)),
                  pl.BlockSpec((None, d, ADA_TILE), lambda j: (layer, 0, j)),
                  pl.BlockSpec((1, ADA_TILE), lambda j: (0, j))] + cast_in,
        out_specs=[pl.BlockSpec((bsz, ADA_TILE), lambda j: (0, j))] + cast_specs,
        out_shape=[jax.ShapeDtypeStruct((bsz, n), F32)] + cast_shapes,
        compiler_params=_params("arbitrary"),
        name="adaln",
    )(c, w_ada, b_ada[layer].reshape(1, n), *casts)
    return out, cast_out


def _ffn_kernel(x_ref, mod_ref, gpre_ref, gpost_ref, wgu_ref, wdn_ref, *rest, sub, res_w):
    n_cast = (len(rest) - 4) // 2
    o_ref = rest[n_cast]
    h_ref, act_ref, y_ref = rest[-3:]
    _cast_slabs(rest[:n_cast], rest[n_cast + 1:-3])
    shift, scale, gate = _mod_rows(mod_ref, sub)
    n_sub = x_ref.shape[0] // FFN_SUB_TILE

    def rows(s):
        return slice(s * FFN_SUB_TILE, (s + 1) * FFN_SUB_TILE)

    def prologue(s):
        x = x_ref[rows(s), :]
        h_ref[s % 2] = _modulated_norm(x, gpre_ref[sub:sub + 1, :], shift, scale).astype(BF16)

    def gate_up(s, j):
        lo = j * FF_TILE
        h = h_ref[s % 2]
        g = _dot(h, wgu_ref[:, lo:lo + FF_TILE])
        u = _dot(h, wgu_ref[:, D_FF + lo:D_FF + lo + FF_TILE])
        act_ref[s % 2, :, lo:lo + FF_TILE] = (g * _sigmoid(g) * u).astype(BF16)

    def down(s):
        y_ref[s % 2] = _dot(act_ref[s % 2], wdn_ref[...])

    def epilogue(s):
        y = _rms(y_ref[s % 2], gpost_ref[sub:sub + 1, :])
        o_ref[rows(s), :] = x_ref[rows(s), :] + (res_w * gate) * y

    n_ff = D_FF // FF_TILE
    prologue(0)
    for s in range(n_sub):
        for j in range(n_ff):
            gate_up(s, j)
            if j == 0 and s > 0:
                epilogue(s - 1)
            if j == n_ff // 2 and s + 1 < n_sub:
                prologue(s + 1)
        down(s)
    epilogue(n_sub - 1)


def _ffn(x2, mod, norm_pre, norm_post, w_gu, w_down, *, sub, res_w, seq, layer, casts=()):
    t, d = x2.shape
    tiles_per_seq = seq // FFN_TOKEN_TILE
    n_steps = t // FFN_TOKEN_TILE
    cast_in, cast_specs, cast_shapes = _cast_streams(casts, layer, n_steps)
    out, *cast_out = pl.pallas_call(
        functools.partial(_ffn_kernel, sub=sub, res_w=res_w),
        grid=(n_steps,),
        in_specs=[pl.BlockSpec((FFN_TOKEN_TILE, d), lambda i: (i, 0)),
                  pl.BlockSpec((None, 3 * N_SUB, d), lambda i: (i // tiles_per_seq, 0, 0)),
                  _resident(norm_pre.shape),
                  _resident(norm_post.shape),
                  _resident(w_gu.shape),
                  _resident(w_down.shape)] + cast_in,
        out_specs=[pl.BlockSpec((FFN_TOKEN_TILE, d), lambda i: (i, 0))] + cast_specs,
        out_shape=[jax.ShapeDtypeStruct((t, d), F32)] + cast_shapes,
        scratch_shapes=[pltpu.VMEM((2, FFN_SUB_TILE, d), BF16),
                        pltpu.VMEM((2, FFN_SUB_TILE, D_FF), BF16),
                        pltpu.VMEM((2, FFN_SUB_TILE, d), F32)],
        compiler_params=_params("arbitrary"),
        name=f"ffn{sub}",
    )(x2, mod, norm_pre, norm_post, w_gu, w_down, *casts)
    return out, cast_out


def _lru_unit(g, xr_ref, row0, gy_ref, o_ref, fresh, cw_ref, cb_ref, w2_ref, ba_ref, bx_ref, decay,
              tail_ref, h_ref, hs_ref):
    sub = lax.broadcasted_iota(jnp.int32, (SUBLANES, LANES), 0)
    taps = CONV_WIDTH - 1
    sl = slice(g * LANES, (g + 1) * LANES)
    x3 = xr_ref[g, row0:row0 + LRU_TILE, :].reshape(LRU_STEPS, SUBLANES, LANES)
    cur_tail = x3[LRU_STEPS - taps:]
    lead = pltpu.roll(jnp.where(sub == SUBLANES - 1, tail_ref[g] * fresh, cur_tail), 1, 1)
    tail_ref[g] = cur_tail
    xext = jnp.concatenate([lead, x3], axis=0)
    xc3 = cb_ref[:, sl] + cw_ref[0:1, sl] * xext[0:LRU_STEPS]
    for w in range(1, CONV_WIDTH):
        xc3 = xc3 + cw_ref[w:w + 1, sl] * xext[w:w + LRU_STEPS]
    xc = xc3.reshape(LRU_TILE, LANES)
    z = _dot(xc.astype(BF16), w2_ref[g])
    ta = jnp.tanh(z[:, :LANES] + ba_ref[:, sl])
    tx = jnp.tanh(z[:, LANES:] + bx_ref[:, sl])
    log_a = decay[:, sl] * (ta + 1.0)
    a = jnp.exp(log_a)
    th = jnp.tanh(log_a)
    q = (-0.5 * th) / (1.0 - th)
    half_mult = jnp.where(q > 0.0, q * lax.rsqrt(q), 0.0)
    u = half_mult * ((tx + 1.0) * xc)
    a3 = a.reshape(LRU_STEPS, SUBLANES, LANES)
    u3 = u.reshape(LRU_STEPS, SUBLANES, LANES)
    loc = [u3[0]]
    prod = [a3[0]]
    for i in range(1, LRU_STEPS):
        loc.append(a3[i] * loc[-1] + u3[i])
        prod.append(a3[i] * prod[-1])
    blk_a, blk_u = prod[-1], loc[-1]
    for s in (1, 2, 4):
        keep = sub >= s
        a_prev = jnp.where(keep, pltpu.roll(blk_a, s, 0), 1.0)
        u_prev = jnp.where(keep, pltpu.roll(blk_u, s, 0), 0.0)
        blk_u = blk_u + blk_a * u_prev
        blk_a = blk_a * a_prev
    h_in = h_ref[g] * fresh
    end = blk_u + blk_a * h_in
    entry = jnp.where(sub == 0, h_in, pltpu.roll(end, 1, 0))
    h_ref[g] = jnp.broadcast_to(end[SUBLANES - 1:SUBLANES, :], (SUBLANES, LANES))
    for i in range(LRU_STEPS):
        hs_ref[g, i * SUBLANES:(i + 1) * SUBLANES, :] = loc[i] + prod[i] * entry
    rows = []
    for n in range(LRU_TILE // SUBLANES):
        j, i0 = divmod(n * SUBLANES, LRU_STEPS)
        rows.append(hs_ref[g, pl.ds(i0 * SUBLANES + j, SUBLANES, stride=SUBLANES), :])
    hs = jnp.concatenate(rows, axis=0)
    o_ref[row0:row0 + LRU_TILE, sl] = (hs * gy_ref[row0:row0 + LRU_TILE, sl]).astype(BF16)


def _proj_kernel(x_ref, mod_ref, gpre_ref, w_ref, cw_ref, cb_ref, w2_ref, ba_ref, bx_ref, lam_ref,
                 q_ref, k_ref, v_ref, rec_ref, sa_ref, sr_ref,
                 xr_s, gy_s, tail_ref, h_ref, hs_ref, *, tiles_per_seq, n_tiles):
    i = pl.program_id(0)
    rows = x_ref.shape[0]
    cw = PROJ_CHUNK
    per_chunk = cw // LANES
    lru0 = 3 * ATT_WIDTH
    n_att, n_lru = ATT_WIDTH // cw, LRU_WIDTH // cw

    @pl.when(i == 0)
    def _():
        for ref in (xr_s, gy_s, tail_ref, h_ref):
            ref[...] = jnp.zeros_like(ref)

    def scan_units():
        neg_lam = -lam_ref[...]
        softplus = jnp.maximum(neg_lam, 0.0) + jnp.log1p(jnp.exp(-jnp.abs(neg_lam)))
        decay = (-0.5 * LRU_C) * softplus
        first = jnp.where(lax.rem(i - 1, tiles_per_seq) == 0, 0.0, 1.0)

        def unit(row0, g):
            _lru_unit(g, xr_s, row0, gy_s, rec_ref, first if row0 == 0 else 1.0, cw_ref, cb_ref,
                      w2_ref, ba_ref, bx_ref, decay, tail_ref, h_ref, hs_ref)

        return [functools.partial(unit, row0, g)
                for row0 in range(0, rows, LRU_TILE) for g in range(LRU_WIDTH // LANES)]

    @pl.when(i < n_tiles)
    def _():
        x = x_ref[...]
        shift, scale, _ = _mod_rows(mod_ref, 1)
        h = _modulated_norm(x, gpre_ref[1:2, :], shift, scale).astype(BF16)

        def proj(col):
            return _dot(h, w_ref[:, col:col + cw])

        def q_chunk(c):
            q = (proj(c * cw) * (ATT_HEAD_DIM ** -0.5 * LOG2_E)).astype(BF16)
            for p in range(per_chunk):
                q_ref[c * per_chunk + p] = q[:, p * LANES:(p + 1) * LANES]

        def k_chunk(c):
            k = proj(ATT_WIDTH + c * cw).astype(BF16)
            for p in range(per_chunk):
                k_ref[c * per_chunk + p] = k[:, p * LANES:(p + 1) * LANES]

        def v_chunk(c):
            heads = cw // ATT_HEAD_DIM
            v_t = proj(2 * ATT_WIDTH + c * cw).T.astype(BF16)
            v_ref[c * heads:(c + 1) * heads] = v_t.reshape(heads, ATT_HEAD_DIM, rows)

        def gate_chunk(ref, col0, c):
            ref[:, c * cw:(c + 1) * cw] = proj(col0 + c * cw).astype(BF16)

        def gy_chunk(c):
            gy_s[:, c * cw:(c + 1) * cw] = _gelu_tanh(proj(lru0 + LRU_WIDTH + c * cw))

        def xr_chunk(c):
            xr = proj(lru0 + c * cw)
            for src_row in range(0, rows, LRU_STEPS):
                tile0, j = src_row // LRU_TILE * LRU_TILE, src_row % LRU_TILE // LRU_STEPS
                for p in range(per_chunk):
                    xr_s[c * per_chunk + p, pl.ds(tile0 + j, LRU_STEPS, stride=SUBLANES), :] = (
                        xr[src_row:src_row + LRU_STEPS, p * LANES:(p + 1) * LANES])

        chunks = ([functools.partial(q_chunk, c) for c in range(n_att)]
                  + [functools.partial(k_chunk, c) for c in range(n_att)]
                  + [functools.partial(v_chunk, c) for c in range(n_att)]
                  + [functools.partial(gate_chunk, sa_ref, lru0 + 2 * LRU_WIDTH, c) for c in range(n_lru)]
                  + [functools.partial(gate_chunk, sr_ref, lru0 + 3 * LRU_WIDTH, c) for c in range(n_lru)]
                  + [functools.partial(gy_chunk, c) for c in range(n_lru)]
                  + [functools.partial(xr_chunk, c) for c in range(n_lru)])
        units = scan_units()
        for n, chunk in enumerate(chunks):
            chunk()
            if n < len(units):
                units[n]()

    @pl.when(i == n_tiles)
    def _():
        for unit in scan_units():
            unit()


def _proj(x2, mod, norm_pre, w_in, conv_w, conv_b, w2, ba, bx, lam, *, seq):
    t, d = x2.shape
    tiles_per_seq = seq // TOKEN_TILE
    n_tiles = t // TOKEN_TILE
    cur = lambda i: jnp.minimum(i, n_tiles - 1)
    row = lambda n: pl.BlockSpec((TOKEN_TILE, n), lambda i: (cur(i), 0))
    tok = lambda n, dt: jax.ShapeDtypeStruct((t, n), dt)
    bsz, pairs, groups = t // seq, ATT_WIDTH // LANES, LRU_WIDTH // LANES
    qk_spec = pl.BlockSpec((None, pairs, TOKEN_TILE, LANES),
                           lambda i: (cur(i) // tiles_per_seq, 0, cur(i) % tiles_per_seq, 0))
    qk_shape = jax.ShapeDtypeStruct((bsz, pairs, seq, LANES), BF16)
    v_spec = pl.BlockSpec((None, ATT_HEADS, ATT_HEAD_DIM, TOKEN_TILE),
                          lambda i: (cur(i) // tiles_per_seq, 0, 0, cur(i) % tiles_per_seq))
    v_shape = jax.ShapeDtypeStruct((bsz, ATT_HEADS, ATT_HEAD_DIM, seq), BF16)
    rec_spec = pl.BlockSpec((TOKEN_TILE, LRU_WIDTH), lambda i: (jnp.maximum(i - 1, 0), 0))
    return pl.pallas_call(
        functools.partial(_proj_kernel, tiles_per_seq=tiles_per_seq, n_tiles=n_tiles),
        grid=(n_tiles + 1,),
        in_specs=[row(d),
                  pl.BlockSpec((None, 3 * N_SUB, d), lambda i: (cur(i) // tiles_per_seq, 0, 0)),
                  _resident(norm_pre.shape), _resident(w_in.shape),
                  _resident(conv_w.shape), _resident(conv_b.shape), _resident(w2.shape),
                  _resident(ba.shape), _resident(bx.shape), _resident(lam.shape)],
        out_specs=[qk_spec, qk_spec, v_spec, rec_spec, row(D_MODEL), row(D_MODEL)],
        out_shape=[qk_shape, qk_shape, v_shape, tok(LRU_WIDTH, BF16),
                   tok(D_MODEL, BF16), tok(D_MODEL, BF16)],
        scratch_shapes=[pltpu.VMEM((groups, TOKEN_TILE, LANES), F32),
                        pltpu.VMEM((TOKEN_TILE, LRU_WIDTH), F32),
                        pltpu.VMEM((groups, CONV_WIDTH - 1, SUBLANES, LANES), F32),
                        pltpu.VMEM((groups, SUBLANES, LANES), F32),
                        pltpu.VMEM((groups, LRU_TILE, LANES), F32)],
        compiler_params=_params("arbitrary"),
        name="mixer_proj",
    )(x2, mod, norm_pre, w_in, conv_w, conv_b, w2, ba, bx, lam)


REL_PAD = 384
TOEPLITZ = 1024


def _bias_kernel(tab_ref, o_ref):
    tab = tab_ref[...]
    hi = tab.astype(BF16)
    r1 = tab - hi.astype(F32)
    mid = r1.astype(BF16)
    lo = (r1 - mid.astype(F32)).astype(BF16)
    d_idx = lax.broadcasted_iota(jnp.int32, (REL_PAD, TOEPLITZ), 0)
    m_idx = lax.broadcasted_iota(jnp.int32, (REL_PAD, TOEPLITZ), 1)
    rel = jnp.clip(m_idx - (Q_TILE - 1), -MAX_REL, MAX_REL) + MAX_REL
    onehot = jnp.where(d_idx == rel, 1.0, 0.0).astype(BF16)
    profile = ((_dot(lo, onehot) + _dot(mid, onehot)) + _dot(hi, onehot)) * LOG2_E

    kk = lax.broadcasted_iota(jnp.int32, (BAND, Q_TILE), 0)
    r = lax.broadcasted_iota(jnp.int32, (BAND, Q_TILE), 1)
    qc = r // CHUNK
    kc = kk // CHUNK
    band = jnp.where((kc >= qc) & (kc <= qc + LEFT_CHUNKS), 0.0, MASK_VALUE)
    for h in range(ATT_HEADS):
        rows = jnp.broadcast_to(profile[h:h + 1, :], (BAND, TOEPLITZ))
        skew = pltpu.roll(rows, TOEPLITZ - (BAND - 1), 1, stride=1, stride_axis=0)
        o_ref[h] = skew[:, :Q_TILE] + band


def _bias_table(rel_bias):
    tab = jnp.pad(rel_bias, ((0, 0), (0, REL_PAD - rel_bias.shape[1])))
    return pl.pallas_call(
        _bias_kernel,
        out_shape=jax.ShapeDtypeStruct((ATT_HEADS, BAND, Q_TILE), F32),
        compiler_params=pltpu.CompilerParams(vmem_limit_bytes=VMEM_LIMIT_BYTES),
        name="rel_bias",
    )(tab)


def _attn_kernel(q_ref, *refs):
    n_kv = K_BLOCKS - 1 + Q_STEP_TILES
    k_refs, v_refs = refs[:n_kv], refs[n_kv:2 * n_kv]
    bias_ref, o_ref, acc_ref = refs[2 * n_kv:2 * n_kv + 3]
    s_refs, p_refs = refs[-4:-2], refs[-2:]
    first_tile = pl.program_id(1) * Q_STEP_TILES
    lane_head = lax.broadcasted_iota(jnp.int32, (1, LANES), 1) // ATT_HEAD_DIM
    zero = jnp.zeros((), BF16)
    live = [(2 * c * CHUNK, (2 * c + 2 + LEFT_CHUNKS) * CHUNK) for c in range(Q_TILE // LANES)]

    @pl.when((pl.program_id(0) == 0) & (pl.program_id(1) == 0))
    def _():
        for p_ref in p_refs:
            p_ref[...] = jnp.zeros_like(p_ref)

    def start_pens(u):
        return [jnp.where(first_tile + u + j < K_BLOCKS - 1, MASK_VALUE, 0.0)
                for j in range(K_BLOCKS - 1)] + [0.0]

    def scores(u, h, slot):
        pair = h // 2
        qh = jnp.where(lane_head == h % 2, q_ref[pair, u * Q_TILE:(u + 1) * Q_TILE, :], zero)
        kcat = jnp.concatenate([r[pair] for r in k_refs[u:u + K_BLOCKS]], axis=0)
        s = lax.dot_general(kcat, qh, (((1,), (1,)), ((), ())), preferred_element_type=F32)
        s_refs[slot][...] = s + bias_ref[h]

    def col_reduce(parts, op):
        while len(parts) > 1:
            parts = [op(parts[i], parts[i + 1]) if i + 1 < len(parts) else parts[i]
                     for i in range(0, len(parts), 2)]
        return parts[0]

    def softmax(u, slot):
        s_ref, p_ref = s_refs[slot], p_refs[slot]
        pens = start_pens(u)
        groups = CHUNK // SUBLANES
        inv = []
        for c, (r0, r1) in enumerate(live):
            cl = slice(c * LANES, (c + 1) * LANES)
            blocks = range(r0, r1, CHUNK)
            part = [s_ref[r:r + CHUNK, cl].reshape(groups, SUBLANES, LANES).max(axis=0)
                    + pens[r // Q_TILE] for r in blocks]
            m = jnp.max(col_reduce(part, jnp.maximum), axis=0, keepdims=True)
            shifted = [m - pen for pen in pens]
            sums = []
            for r in blocks:
                pr = jnp.exp2(s_ref[r:r + CHUNK, cl] - shifted[r // Q_TILE])
                sums.append(pr.reshape(groups, SUBLANES, LANES).sum(axis=0))
                p_ref[r:r + CHUNK, cl] = pr.astype(BF16)
            l = jnp.sum(col_reduce(sums, jnp.add), axis=0, keepdims=True)
            inv.append(1.0 / l)
        return jnp.concatenate(inv, axis=1)

    def weighted_values(u, h, slot, inv_l):
        v_t = jnp.concatenate([r[h] for r in v_refs[u:u + K_BLOCKS]], axis=1)
        acc_ref[u, h] = _dot(v_t, p_refs[slot][...]) * inv_l

    items = [(u, h) for u in range(Q_STEP_TILES) for h in range(ATT_HEADS)]
    scores(*items[0], 0)
    inv_prev = None
    for n, (u, h) in enumerate(items):
        if n + 1 < len(items):
            scores(*items[n + 1], (n + 1) % 2)
        inv_l = softmax(u, n % 2)
        if n > 0:
            weighted_values(*items[n - 1], (n - 1) % 2, inv_prev)
        inv_prev = inv_l
    weighted_values(*items[-1], (len(items) - 1) % 2, inv_prev)
    for u in range(Q_STEP_TILES):
        for p in range(ATT_HEADS // 2):
            pair_t = acc_ref[u, 2 * p:2 * p + 2].reshape(LANES, Q_TILE)
            o_ref[u * Q_TILE:(u + 1) * Q_TILE, p * LANES:(p + 1) * LANES] = pair_t.T.astype(BF16)


def _attention(q, k, v_t, bias):
    bsz, pairs, seq, _ = q.shape
    step_rows = Q_STEP_TILES * Q_TILE
    rel = range(-(K_BLOCKS - 1), Q_STEP_TILES)
    blk = lambda t, n: jnp.maximum(t * Q_STEP_TILES + n, 0)
    key = lambda n: pl.BlockSpec((None, pairs, Q_TILE, LANES), lambda b, t: (b, 0, blk(t, n), 0))
    val = lambda n: pl.BlockSpec((None, ATT_HEADS, ATT_HEAD_DIM, Q_TILE),
                                 lambda b, t: (b, 0, 0, blk(t, n)))
    slot = lambda dt: pltpu.VMEM((BAND, Q_TILE), dt)
    return pl.pallas_call(
        _attn_kernel,
        grid=(bsz, seq // step_rows),
        in_specs=[pl.BlockSpec((None, pairs, step_rows, LANES), lambda b, t: (b, 0, t, 0))]
                 + [key(n) for n in rel] + [val(n) for n in rel] + [_resident(bias.shape)],
        out_specs=pl.BlockSpec((None, step_rows, ATT_WIDTH), lambda b, t: (b, t, 0)),
        out_shape=jax.ShapeDtypeStruct((bsz, seq, ATT_WIDTH), BF16),
        scratch_shapes=[pltpu.VMEM((Q_STEP_TILES, ATT_HEADS, ATT_HEAD_DIM, Q_TILE), F32),
                        slot(F32), slot(F32), slot(BF16), slot(BF16)],
        compiler_params=_params("arbitrary", "arbitrary"),
        name="chunk_attn",
    )(q, *([k] * len(rel)), *([v_t] * len(rel)), bias)


def _pair_block_diag(wa, wx):
    def pairs(w):
        z = jnp.zeros_like(w[0::2])
        top = jnp.concatenate([w[0::2], z], axis=2)
        bot = jnp.concatenate([z, w[1::2]], axis=2)
        return jnp.concatenate([top, bot], axis=1)
    return jnp.concatenate([pairs(wa), pairs(wx)], axis=2)


def _mixout_kernel(x_ref, mod_ref, gpost_ref, att_ref, rec_ref, sa_ref, sr_ref,
                   wao_ref, wro_ref, wout_ref, *rest):
    n_cast = (len(rest) - 2) // 2
    o_ref, y_ref = rest[n_cast], rest[-1]
    _cast_slabs(rest[:n_cast], rest[n_cast + 1:-1])
    _, _, gate = _mod_rows(mod_ref, 1)
    n_sub = x_ref.shape[0] // MIX_SUB_TILE

    def rows(s):
        return slice(s * MIX_SUB_TILE, (s + 1) * MIX_SUB_TILE)

    def project(s):
        att = _dot(att_ref[rows(s), :], wao_ref[...])
        rec = _dot(rec_ref[rows(s), :], wro_ref[...])
        gate_att = _sigmoid(sa_ref[rows(s), :].astype(F32))
        gate_rec = _sigmoid(sr_ref[rows(s), :].astype(F32))
        merged = gate_att * att + gate_rec * rec
        y_ref[s % 2] = _dot(merged.astype(BF16), wout_ref[...])

    def epilogue(s):
        o_ref[rows(s), :] = x_ref[rows(s), :] + gate * _rms(y_ref[s % 2], gpost_ref[1:2, :])

    for s in range(n_sub):
        project(s)
        if s > 0:
            epilogue(s - 1)
    epilogue(n_sub - 1)


def _mixout(x2, mod, norm_post, att, rec, sa, sr, w_att_o, w_rec_o, w_out, *, seq, layer,
            casts=()):
    t, d = x2.shape
    tiles_per_seq = seq // MIX_TOKEN_TILE
    n_steps = t // MIX_TOKEN_TILE
    row = lambda n: pl.BlockSpec((MIX_TOKEN_TILE, n), lambda i: (i, 0))
    cast_in, cast_specs, cast_shapes = _cast_streams(casts, layer, n_steps)
    out, *cast_out = pl.pallas_call(
        _mixout_kernel,
        grid=(n_steps,),
        in_specs=[row(d),
                  pl.BlockSpec((None, 3 * N_SUB, d), lambda i: (i // tiles_per_seq, 0, 0)),
                  _resident(norm_post.shape),
                  row(ATT_WIDTH), row(LRU_WIDTH), row(d), row(d),
                  _resident(w_att_o.shape), _resident(w_rec_o.shape), _resident(w_out.shape)]
                 + cast_in,
        out_specs=[row(d)] + cast_specs,
        out_shape=[jax.ShapeDtypeStruct((t, d), F32)] + cast_shapes,
        scratch_shapes=[pltpu.VMEM((2, MIX_SUB_TILE, d), F32)],
        compiler_params=_params("arbitrary"),
        name="mixer_out",
    )(x2, mod, norm_post, att, rec, sa, sr, w_att_o, w_rec_o, w_out, *casts)
    return out, cast_out


def _layer(x2, c, l, bsz, seq, w_ada, b_ada, norm_pre, norm_post, ffn1_w_gu, ffn1_w_down, w_in,
           rel_bias, conv_w, conv_b, lru_wa, lru_ba, lru_wx, lru_bx, lru_lambda, w_att_o,
           w_rec_o, w_out, ffn2_w_gu, ffn2_w_down):
    mod, (w_gu1_b, w_down1_b) = _ada(c, w_ada, b_ada, layer=l, casts=(ffn1_w_gu, ffn1_w_down))
    mod = mod.reshape(bsz, 3 * N_SUB, D_MODEL)
    npre, npost = norm_pre[l], norm_post[l]

    x2, (w_in_b, w_att_o_b, w_rec_o_b, w_out_b, w_down2_b) = _ffn(
        x2, mod, npre, npost, w_gu1_b, w_down1_b, sub=0, res_w=0.5, seq=seq, layer=l,
        casts=(w_in, w_att_o, w_rec_o, w_out, ffn2_w_down))

    w2 = (0.5 * _pair_block_diag(lru_wa[l], lru_wx[l])).astype(BF16)
    vec = lambda p: p[l].reshape(1, LRU_WIDTH)
    q, k, v, rec, sa, sr = _proj(x2, mod, npre, w_in_b, conv_w[l], vec(conv_b), w2,
                                 0.5 * vec(lru_ba), 0.5 * vec(lru_bx), vec(lru_lambda), seq=seq)
    att = _attention(q, k, v, _bias_table(rel_bias[l]))
    x2, (w_gu2_b,) = _mixout(
        x2, mod, npost, att.reshape(bsz * seq, ATT_WIDTH), rec, sa, sr,
        w_att_o_b, w_rec_o_b, w_out_b, seq=seq, layer=l, casts=(ffn2_w_gu,))

    return _ffn(x2, mod, npre, npost, w_gu2_b, w_down2_b, sub=2, res_w=0.5, seq=seq, layer=l)[0]


def kernel(x, c, w_ada, b_ada, norm_pre, norm_post, ffn1_w_gu, ffn1_w_down, w_in, rel_bias, conv_w, conv_b, lru_wa, lru_ba, lru_wx, lru_bx, lru_lambda, w_att_o, w_rec_o, w_out, ffn2_w_gu, ffn2_w_down):
    bsz, seq, d = x.shape
    tiles = (TOKEN_TILE, MIX_TOKEN_TILE, FFN_TOKEN_TILE, Q_TILE * Q_STEP_TILES)
    assert d == D_MODEL and all(seq % n == 0 for n in tiles)
    assert TOKEN_TILE % LRU_TILE == 0
    x2 = x.reshape(bsz * seq, d)
    for l in range(w_ada.shape[0]):
        x2 = _layer(x2, c, l, bsz, seq, w_ada, b_ada, norm_pre, norm_post, ffn1_w_gu,
                    ffn1_w_down, w_in, rel_bias, conv_w, conv_b, lru_wa, lru_ba, lru_wx, lru_bx,
                    lru_lambda, w_att_o, w_rec_o, w_out, ffn2_w_gu, ffn2_w_down)
    return x2.reshape(bsz, seq, d)
```

```python
import functools
import math

import jax
import jax.numpy as jnp
from jax import lax
from jax.experimental import pallas as pl
from jax.experimental.pallas import tpu as pltpu

D_MODEL = 1024
CHUNK = 64
LEFT_CHUNKS = 8
ATT_HEADS = 8
ATT_HEAD_DIM = 64
ATT_WIDTH = ATT_HEADS * ATT_HEAD_DIM
MAX_REL = 128
LRU_WIDTH = D_MODEL
LRU_BLOCKS = 16
LRU_BLOCK = LRU_WIDTH // LRU_BLOCKS
CONV_WIDTH = 4
LRU_C = 8.0
D_FF = 2816
N_SUB = 3
EPS = 1e-6

LANES = 128
SUBLANES = 8
BF16_ROWS = 16
VMEM_LIMIT_BYTES = 56 * 1024 * 1024

TOKEN_TILE = 512
MIX_TOKEN_TILE = 1024
MIX_SUB_TILE = 512
FFN_TOKEN_TILE = 1024
FFN_SUB_TILE = 512
FF_TILE = 256
PROJ_CHUNK = 256
ADA_TILE = 1152
Q_TILE = 4 * CHUNK
Q_STEP_TILES = 4
K_BLOCKS = LEFT_CHUNKS * CHUNK // Q_TILE + 1
BAND = K_BLOCKS * Q_TILE
LRU_TILE = 256
LRU_STEPS = LRU_TILE // SUBLANES
MASK_VALUE = -1e30
LOG2_E = math.log2(math.e)
BF16 = jnp.bfloat16
F32 = jnp.float32


def _dot(a, b):
    return jnp.dot(a, b, preferred_element_type=F32)


def _rms(x, g):
    return x * lax.rsqrt(jnp.mean(x * x, axis=-1, keepdims=True) + EPS) * g


def _modulated_norm(x, g, shift, scale):
    inv = lax.rsqrt(jnp.mean(x * x, axis=-1, keepdims=True) + EPS)
    return x * inv * (g * (1.0 + scale)) + shift


def _sigmoid(x):
    return 1.0 / (1.0 + jnp.exp(-x))


def _gelu_tanh(x):
    c = -2.0 * LOG2_E * math.sqrt(2.0 / math.pi)
    return x / (1.0 + jnp.exp2(x * (c + (0.044715 * c) * (x * x))))


def _mod_rows(mod_ref, sub):
    shift = mod_ref[3 * sub:3 * sub + 1, :]
    scale = mod_ref[3 * sub + 1:3 * sub + 2, :]
    gate = mod_ref[3 * sub + 2:3 * sub + 3, :]
    return shift, scale, gate


def _resident(shape):
    nd = len(shape)
    return pl.BlockSpec(shape, lambda *_: (0,) * nd, pipeline_mode=pl.Buffered(1))


def _params(*semantics):
    return pltpu.CompilerParams(dimension_semantics=semantics,
                                vmem_limit_bytes=VMEM_LIMIT_BYTES)


def _cast_streams(stacked, layer, n_steps):
    for w in stacked:
        assert w.shape[1] % (BF16_ROWS * n_steps) == 0 and w.shape[2] % LANES == 0, w.shape
    in_specs = [pl.BlockSpec((None, w.shape[1] // n_steps, w.shape[2]), lambda i: (layer, i, 0))
                for w in stacked]
    out_specs = [pl.BlockSpec((w.shape[1] // n_steps, w.shape[2]), lambda i: (i, 0))
                 for w in stacked]
    shapes = [jax.ShapeDtypeStruct(w.shape[1:], BF16) for w in stacked]
    return in_specs, out_specs, shapes


def _cast_slabs(in_refs, out_refs):
    for src, dst in zip(in_refs, out_refs):
        dst[...] = src[...].astype(BF16)


def _ada_kernel(c_ref, w_ref, b_ref, *rest):
    n_cast = (len(rest) - 1) // 2
    o_ref = rest[n_cast]
    _cast_slabs(rest[:n_cast], rest[n_cast + 1:])
    c = c_ref[...]
    c_act = (c * _sigmoid(c)).astype(BF16)
    o_ref[...] = _dot(c_act, w_ref[...].astype(BF16)) + b_ref[...]


def _ada(c, w_ada, b_ada, *, layer, casts=()):
    bsz, d = c.shape
    n = w_ada.shape[2]
    n_steps = n // ADA_TILE
    cast_in, cast_specs, cast_shapes = _cast_streams(casts, layer, n_steps)
    out, *cast_out = pl.pallas_call(
        _ada_kernel,
        grid=(n_steps,),
        in_specs=[pl.BlockSpec((bsz, d), lambda j: (0, 0)),
                  pl.BlockSpec((None, d, ADA_TILE), lambda j: (layer, 0, j)),
                  pl.BlockSpec((1, ADA_TILE), lambda j: (0, j))] + cast_in,
        out_specs=[pl.BlockSpec((bsz, ADA_TILE), lambda j: (0, j))] + cast_specs,
        out_shape=[jax.ShapeDtypeStruct((bsz, n), F32)] + cast_shapes,
        compiler_params=_params("arbitrary"),
        name="adaln",
    )(c, w_ada, b_ada[layer].reshape(1, n), *casts)
    return out, cast_out


def _ffn_kernel(x_ref, mod_ref, gpre_ref, gpost_ref, wgu_ref, wdn_ref, *rest, sub, res_w):
    n_cast = (len(rest) - 4) // 2
    o_ref = rest[n_cast]
    h_ref, act_ref, y_ref = rest[-3:]
    _cast_slabs(rest[:n_cast], rest[n_cast + 1:-3])
    shift, scale, gate = _mod_rows(mod_ref, sub)
    n_sub = x_ref.shape[0] // FFN_SUB_TILE

    def rows(s):
        return slice(s * FFN_SUB_TILE, (s + 1) * FFN_SUB_TILE)

    def prologue(s):
        x = x_ref[rows(s), :]
        h_ref[s % 2] = _modulated_norm(x, gpre_ref[sub:sub + 1, :], shift, scale).astype(BF16)

    def gate_up(s, j):
        lo = j * FF_TILE
        h = h_ref[s % 2]
        g = _dot(h, wgu_ref[:, lo:lo + FF_TILE])
        u = _dot(h, wgu_ref[:, D_FF + lo:D_FF + lo + FF_TILE])
        act_ref[s % 2, :, lo:lo + FF_TILE] = (g * _sigmoid(g) * u).astype(BF16)

    def down(s):
        y_ref[s % 2] = _dot(act_ref[s % 2], wdn_ref[...])

    def epilogue(s):
        y = _rms(y_ref[s % 2], gpost_ref[sub:sub + 1, :])
        o_ref[rows(s), :] = x_ref[rows(s), :] + (res_w * gate) * y

    n_ff = D_FF // FF_TILE
    prologue(0)
    for s in range(n_sub):
        for j in range(n_ff):
            gate_up(s, j)
            if j == 0 and s > 0:
                epilogue(s - 1)
            if j == n_ff // 2 and s + 1 < n_sub:
                prologue(s + 1)
        down(s)
    epilogue(n_sub - 1)


def _ffn(x2, mod, norm_pre, norm_post, w_gu, w_down, *, sub, res_w, seq, layer, casts=()):
    t, d = x2.shape
    tiles_per_seq = seq // FFN_TOKEN_TILE
    n_steps = t // FFN_TOKEN_TILE
    cast_in, cast_specs, cast_shapes = _cast_streams(casts, layer, n_steps)
    out, *cast_out = pl.pallas_call(
        functools.partial(_ffn_kernel, sub=sub, res_w=res_w),
        grid=(n_steps,),
        in_specs=[pl.BlockSpec((FFN_TOKEN_TILE, d), lambda i: (i, 0)),
                  pl.BlockSpec((None, 3 * N_SUB, d), lambda i: (i // tiles_per_seq, 0, 0)),
                  _resident(norm_pre.shape),
                  _resident(norm_post.shape),
                  _resident(w_gu.shape),
                  _resident(w_down.shape)] + cast_in,
        out_specs=[pl.BlockSpec((FFN_TOKEN_TILE, d), lambda i: (i, 0))] + cast_specs,
        out_shape=[jax.ShapeDtypeStruct((t, d), F32)] + cast_shapes,
        scratch_shapes=[pltpu.VMEM((2, FFN_SUB_TILE, d), BF16),
                        pltpu.VMEM((2, FFN_SUB_TILE, D_FF), BF16),
                        pltpu.VMEM((2, FFN_SUB_TILE, d), F32)],
        compiler_params=_params("arbitrary"),
        name=f"ffn{sub}",
    )(x2, mod, norm_pre, norm_post, w_gu, w_down, *casts)
    return out, cast_out


def _lru_unit(g, xr_ref, row0, gy_ref, o_ref, fresh, cw_ref, cb_ref, w2_ref, ba_ref, bx_ref, decay,
              tail_ref, h_ref, hs_ref):
    sub = lax.broadcasted_iota(jnp.int32, (SUBLANES, LANES), 0)
    taps = CONV_WIDTH - 1
    sl = slice(g * LANES, (g + 1) * LANES)
    x3 = xr_ref[g, row0:row0 + LRU_TILE, :].reshape(LRU_STEPS, SUBLANES, LANES)
    cur_tail = x3[LRU_STEPS - taps:]
    lead = pltpu.roll(jnp.where(sub == SUBLANES - 1, tail_ref[g] * fresh, cur_tail), 1, 1)
    tail_ref[g] = cur_tail
    xext = jnp.concatenate([lead, x3], axis=0)
    xc3 = cb_ref[:, sl] + cw_ref[0:1, sl] * xext[0:LRU_STEPS]
    for w in range(1, CONV_WIDTH):
        xc3 = xc3 + cw_ref[w:w + 1, sl] * xext[w:w + LRU_STEPS]
    xc = xc3.reshape(LRU_TILE, LANES)
    z = _dot(xc.astype(BF16), w2_ref[g])
    ta = jnp.tanh(z[:, :LANES] + ba_ref[:, sl])
    tx = jnp.tanh(z[:, LANES:] + bx_ref[:, sl])
    log_a = decay[:, sl] * (ta + 1.0)
    a = jnp.exp(log_a)
    th = jnp.tanh(log_a)
    q = (-0.5 * th) / (1.0 - th)
    half_mult = jnp.where(q > 0.0, q * lax.rsqrt(q), 0.0)
    u = half_mult * ((tx + 1.0) * xc)
    a3 = a.reshape(LRU_STEPS, SUBLANES, LANES)
    u3 = u.reshape(LRU_STEPS, SUBLANES, LANES)
    loc = [u3[0]]
    prod = [a3[0]]
    for i in range(1, LRU_STEPS):
        loc.append(a3[i] * loc[-1] + u3[i])
        prod.append(a3[i] * prod[-1])
    blk_a, blk_u = prod[-1], loc[-1]
    for s in (1, 2, 4):
        keep = sub >= s
        a_prev = jnp.where(keep, pltpu.roll(blk_a, s, 0), 1.0)
        u_prev = jnp.where(keep, pltpu.roll(blk_u, s, 0), 0.0)
        blk_u = blk_u + blk_a * u_prev
        blk_a = blk_a * a_prev
    h_in = h_ref[g] * fresh
    end = blk_u + blk_a * h_in
    entry = jnp.where(sub == 0, h_in, pltpu.roll(end, 1, 0))
    h_ref[g] = jnp.broadcast_to(end[SUBLANES - 1:SUBLANES, :], (SUBLANES, LANES))
    for i in range(LRU_STEPS):
        hs_ref[g, i * SUBLANES:(i + 1) * SUBLANES, :] = loc[i] + prod[i] * entry
    rows = []
    for n in range(LRU_TILE // SUBLANES):
        j, i0 = divmod(n * SUBLANES, LRU_STEPS)
        rows.append(hs_ref[g, pl.ds(i0 * SUBLANES + j, SUBLANES, stride=SUBLANES), :])
    hs = jnp.concatenate(rows, axis=0)
    o_ref[row0:row0 + LRU_TILE, sl] = (hs * gy_ref[row0:row0 + LRU_TILE, sl]).astype(BF16)


def _proj_kernel(x_ref, mod_ref, gpre_ref, w_ref, cw_ref, cb_ref, w2_ref, ba_ref, bx_ref, lam_ref,
                 q_ref, k_ref, v_ref, rec_ref, sa_ref, sr_ref,
                 xr_s, gy_s, tail_ref, h_ref, hs_ref, *, tiles_per_seq, n_tiles):
    i = pl.program_id(0)
    rows = x_ref.shape[0]
    cw = PROJ_CHUNK
    per_chunk = cw // LANES
    lru0 = 3 * ATT_WIDTH
    n_att, n_lru = ATT_WIDTH // cw, LRU_WIDTH // cw

    @pl.when(i == 0)
    def _():
        for ref in (xr_s, gy_s, tail_ref, h_ref):
            ref[...] = jnp.zeros_like(ref)

    def scan_units():
        neg_lam = -lam_ref[...]
        softplus = jnp.maximum(neg_lam, 0.0) + jnp.log1p(jnp.exp(-jnp.abs(neg_lam)))
        decay = (-0.5 * LRU_C) * softplus
        first = jnp.where(lax.rem(i - 1, tiles_per_seq) == 0, 0.0, 1.0)

        def unit(row0, g):
            _lru_unit(g, xr_s, row0, gy_s, rec_ref, first if row0 == 0 else 1.0, cw_ref, cb_ref,
                      w2_ref, ba_ref, bx_ref, decay, tail_ref, h_ref, hs_ref)

        return [functools.partial(unit, row0, g)
                for row0 in range(0, rows, LRU_TILE) for g in range(LRU_WIDTH // LANES)]

    @pl.when(i < n_tiles)
    def _():
        x = x_ref[...]
        shift, scale, _ = _mod_rows(mod_ref, 1)
        h = _modulated_norm(x, gpre_ref[1:2, :], shift, scale).astype(BF16)

        def proj(col):
            return _dot(h, w_ref[:, col:col + cw])

        def q_chunk(c):
            q = (proj(c * cw) * (ATT_HEAD_DIM ** -0.5 * LOG2_E)).astype(BF16)
            for p in range(per_chunk):
                q_ref[c * per_chunk + p] = q[:, p * LANES:(p + 1) * LANES]

        def k_chunk(c):
            k = proj(ATT_WIDTH + c * cw).astype(BF16)
            for p in range(per_chunk):
                k_ref[c * per_chunk + p] = k[:, p * LANES:(p + 1) * LANES]

        def v_chunk(c):
            heads = cw // ATT_HEAD_DIM
            v_t = proj(2 * ATT_WIDTH + c * cw).T.astype(BF16)
            v_ref[c * heads:(c + 1) * heads] = v_t.reshape(heads, ATT_HEAD_DIM, rows)

        def gate_chunk(ref, col0, c):
            ref[:, c * cw:(c + 1) * cw] = proj(col0 + c * cw).astype(BF16)

        def gy_chunk(c):
            gy_s[:, c * cw:(c + 1) * cw] = _gelu_tanh(proj(lru0 + LRU_WIDTH + c * cw))

        def xr_chunk(c):
            xr = proj(lru0 + c * cw)
            for src_row in range(0, rows, LRU_STEPS):
                tile0, j = src_row // LRU_TILE * LRU_TILE, src_row % LRU_TILE // LRU_STEPS
                for p in range(per_chunk):
                    xr_s[c * per_chunk + p, pl.ds(tile0 + j, LRU_STEPS, stride=SUBLANES), :] = (
                        xr[src_row:src_row + LRU_STEPS, p * LANES:(p + 1) * LANES])

        chunks = ([functools.partial(q_chunk, c) for c in range(n_att)]
                  + [functools.partial(k_chunk, c) for c in range(n_att)]
                  + [functools.partial(v_chunk, c) for c in range(n_att)]
                  + [functools.partial(gate_chunk, sa_ref, lru0 + 2 * LRU_WIDTH, c) for c in range(n_lru)]
                  + [functools.partial(gate_chunk, sr_ref, lru0 + 3 * LRU_WIDTH, c) for c in range(n_lru)]
                  + [functools.partial(gy_chunk, c) for c in range(n_lru)]
                  + [functools.partial(xr_chunk, c) for c in range(n_lru)])
        units = scan_units()
        for n, chunk in enumerate(chunks):
            chunk()
            if n < len(units):
                units[n]()

    @pl.when(i == n_tiles)
    def _():
        for unit in scan_units():
            unit()


def _proj(x2, mod, norm_pre, w_in, conv_w, conv_b, w2, ba, bx, lam, *, seq):
    t, d = x2.shape
    tiles_per_seq = seq // TOKEN_TILE
    n_tiles = t // TOKEN_TILE
    cur = lambda i: jnp.minimum(i, n_tiles - 1)
    row = lambda n: pl.BlockSpec((TOKEN_TILE, n), lambda i: (cur(i), 0))
    tok = lambda n, dt: jax.ShapeDtypeStruct((t, n), dt)
    bsz, pairs, groups = t // seq, ATT_WIDTH // LANES, LRU_WIDTH // LANES
    qk_spec = pl.BlockSpec((None, pairs, TOKEN_TILE, LANES),
                           lambda i: (cur(i) // tiles_per_seq, 0, cur(i) % tiles_per_seq, 0))
    qk_shape = jax.ShapeDtypeStruct((bsz, pairs, seq, LANES), BF16)
    v_spec = pl.BlockSpec((None, ATT_HEADS, ATT_HEAD_DIM, TOKEN_TILE),
                          lambda i: (cur(i) // tiles_per_seq, 0, 0, cur(i) % tiles_per_seq))
    v_shape = jax.ShapeDtypeStruct((bsz, ATT_HEADS, ATT_HEAD_DIM, seq), BF16)
    rec_spec = pl.BlockSpec((TOKEN_TILE, LRU_WIDTH), lambda i: (jnp.maximum(i - 1, 0), 0))
    return pl.pallas_call(
        functools.partial(_proj_kernel, tiles_per_seq=tiles_per_seq, n_tiles=n_tiles),
        grid=(n_tiles + 1,),
        in_specs=[row(d),
                  pl.BlockSpec((None, 3 * N_SUB, d), lambda i: (cur(i) // tiles_per_seq, 0, 0)),
                  _resident(norm_pre.shape), _resident(w_in.shape),
                  _resident(conv_w.shape), _resident(conv_b.shape), _resident(w2.shape),
                  _resident(ba.shape), _resident(bx.shape), _resident(lam.shape)],
        out_specs=[qk_spec, qk_spec, v_spec, rec_spec, row(D_MODEL), row(D_MODEL)],
        out_shape=[qk_shape, qk_shape, v_shape, tok(LRU_WIDTH, BF16),
                   tok(D_MODEL, BF16), tok(D_MODEL, BF16)],
        scratch_shapes=[pltpu.VMEM((groups, TOKEN_TILE, LANES), F32),
                        pltpu.VMEM((TOKEN_TILE, LRU_WIDTH), F32),
                        pltpu.VMEM((groups, CONV_WIDTH - 1, SUBLANES, LANES), F32),
                        pltpu.VMEM((groups, SUBLANES, LANES), F32),
                        pltpu.VMEM((groups, LRU_TILE, LANES), F32)],
        compiler_params=_params("arbitrary"),
        name="mixer_proj",
    )(x2, mod, norm_pre, w_in, conv_w, conv_b, w2, ba, bx, lam)


REL_PAD = 384
TOEPLITZ = 1024


def _bias_kernel(tab_ref, o_ref):
    tab = tab_ref[...]
    hi = tab.astype(BF16)
    r1 = tab - hi.astype(F32)
    mid = r1.astype(BF16)
    lo = (r1 - mid.astype(F32)).astype(BF16)
    d_idx = lax.broadcasted_iota(jnp.int32, (REL_PAD, TOEPLITZ), 0)
    m_idx = lax.broadcasted_iota(jnp.int32, (REL_PAD, TOEPLITZ), 1)
    rel = jnp.clip(m_idx - (Q_TILE - 1), -MAX_REL, MAX_REL) + MAX_REL
    onehot = jnp.where(d_idx == rel, 1.0, 0.0).astype(BF16)
    profile = ((_dot(lo, onehot) + _dot(mid, onehot)) + _dot(hi, onehot)) * LOG2_E

    kk = lax.broadcasted_iota(jnp.int32, (BAND, Q_TILE), 0)
    r = lax.broadcasted_iota(jnp.int32, (BAND, Q_TILE), 1)
    qc = r // CHUNK
    kc = kk // CHUNK
    band = jnp.where((kc >= qc) & (kc <= qc + LEFT_CHUNKS), 0.0, MASK_VALUE)
    for h in range(ATT_HEADS):
        rows = jnp.broadcast_to(profile[h:h + 1, :], (BAND, TOEPLITZ))
        skew = pltpu.roll(rows, TOEPLITZ - (BAND - 1), 1, stride=1, stride_axis=0)
        o_ref[h] = skew[:, :Q_TILE] + band


def _bias_table(rel_bias):
    tab = jnp.pad(rel_bias, ((0, 0), (0, REL_PAD - rel_bias.shape[1])))
    return pl.pallas_call(
        _bias_kernel,
        out_shape=jax.ShapeDtypeStruct((ATT_HEADS, BAND, Q_TILE), F32),
        compiler_params=pltpu.CompilerParams(vmem_limit_bytes=VMEM_LIMIT_BYTES),
        name="rel_bias",
    )(tab)


def _attn_kernel(q_ref, *refs):
    n_kv = K_BLOCKS - 1 + Q_STEP_TILES
    k_refs, v_refs = refs[:n_kv], refs[n_kv:2 * n_kv]
    bias_ref, o_ref, acc_ref = refs[2 * n_kv:2 * n_kv + 3]
    s_refs, p_refs = refs[-4:-2], refs[-2:]
    first_tile = pl.program_id(1) * Q_STEP_TILES
    lane_head = lax.broadcasted_iota(jnp.int32, (1, LANES), 1) // ATT_HEAD_DIM
    zero = jnp.zeros((), BF16)
    live = [(2 * c * CHUNK, (2 * c + 2 + LEFT_CHUNKS) * CHUNK) for c in range(Q_TILE // LANES)]

    @pl.when((pl.program_id(0) == 0) & (pl.program_id(1) == 0))
    def _():
        for p_ref in p_refs:
            p_ref[...] = jnp.zeros_like(p_ref)

    def start_pens(u):
        return [jnp.where(first_tile + u + j < K_BLOCKS - 1, MASK_VALUE, 0.0)
                for j in range(K_BLOCKS - 1)] + [0.0]

    def scores(u, h, slot):
        pair = h // 2
        qh = jnp.where(lane_head == h % 2, q_ref[pair, u * Q_TILE:(u + 1) * Q_TILE, :], zero)
        kcat = jnp.concatenate([r[pair] for r in k_refs[u:u + K_BLOCKS]], axis=0)
        s = lax.dot_general(kcat, qh, (((1,), (1,)), ((), ())), preferred_element_type=F32)
        s_refs[slot][...] = s + bias_ref[h]

    def col_reduce(parts, op):
        while len(parts) > 1:
            parts = [op(parts[i], parts[i + 1]) if i + 1 < len(parts) else parts[i]
                     for i in range(0, len(parts), 2)]
        return parts[0]

    def softmax(u, slot):
        s_ref, p_ref = s_refs[slot], p_refs[slot]
        pens = start_pens(u)
        groups = CHUNK // SUBLANES
        inv = []
        for c, (r0, r1) in enumerate(live):
            cl = slice(c * LANES, (c + 1) * LANES)
            blocks = range(r0, r1, CHUNK)
            part = [s_ref[r:r + CHUNK, cl].reshape(groups, SUBLANES, LANES).max(axis=0)
                    + pens[r // Q_TILE] for r in blocks]
            m = jnp.max(col_reduce(part, jnp.maximum), axis=0, keepdims=True)
            shifted = [m - pen for pen in pens]
            sums = []
            for r in blocks:
                pr = jnp.exp2(s_ref[r:r + CHUNK, cl] - shifted[r // Q_TILE])
                sums.append(pr.reshape(groups, SUBLANES, LANES).sum(axis=0))
                p_ref[r:r + CHUNK, cl] = pr.astype(BF16)
            l = jnp.sum(col_reduce(sums, jnp.add), axis=0, keepdims=True)
            inv.append(1.0 / l)
        return jnp.concatenate(inv, axis=1)

    def weighted_values(u, h, slot, inv_l):
        v_t = jnp.concatenate([r[h] for r in v_refs[u:u + K_BLOCKS]], axis=1)
        acc_ref[u, h] = _dot(v_t, p_refs[slot][...]) * inv_l

    items = [(u, h) for u in range(Q_STEP_TILES) for h in range(ATT_HEADS)]
    scores(*items[0], 0)
    inv_prev = None
    for n, (u, h) in enumerate(items):
        if n + 1 < len(items):
            scores(*items[n + 1], (n + 1) % 2)
        inv_l = softmax(u, n % 2)
        if n > 0:
            weighted_values(*items[n - 1], (n - 1) % 2, inv_prev)
        inv_prev = inv_l
    weighted_values(*items[-1], (len(items) - 1) % 2, inv_prev)
    for u in range(Q_STEP_TILES):
        for p in range(ATT_HEADS // 2):
            pair_t = acc_ref[u, 2 * p:2 * p + 2].reshape(LANES, Q_TILE)
            o_ref[u * Q_TILE:(u + 1) * Q_TILE, p * LANES:(p + 1) * LANES] = pair_t.T.astype(BF16)


def _attention(q, k, v_t, bias):
    bsz, pairs, seq, _ = q.shape
    step_rows = Q_STEP_TILES * Q_TILE
    rel = range(-(K_BLOCKS - 1), Q_STEP_TILES)
    blk = lambda t, n: jnp.maximum(t * Q_STEP_TILES + n, 0)
    key = lambda n: pl.BlockSpec((None, pairs, Q_TILE, LANES), lambda b, t: (b, 0, blk(t, n), 0))
    val = lambda n: pl.BlockSpec((None, ATT_HEADS, ATT_HEAD_DIM, Q_TILE),
                                 lambda b, t: (b, 0, 0, blk(t, n)))
    slot = lambda dt: pltpu.VMEM((BAND, Q_TILE), dt)
    return pl.pallas_call(
        _attn_kernel,
        grid=(bsz, seq // step_rows),
        in_specs=[pl.BlockSpec((None, pairs, step_rows, LANES), lambda b, t: (b, 0, t, 0))]
                 + [key(n) for n in rel] + [val(n) for n in rel] + [_resident(bias.shape)],
        out_specs=pl.BlockSpec((None, step_rows, ATT_WIDTH), lambda b, t: (b, t, 0)),
        out_shape=jax.ShapeDtypeStruct((bsz, seq, ATT_WIDTH), BF16),
        scratch_shapes=[pltpu.VMEM((Q_STEP_TILES, ATT_HEADS, ATT_HEAD_DIM, Q_TILE), F32),
                        slot(F32), slot(F32), slot(BF16), slot(BF16)],
        compiler_params=_params("arbitrary", "arbitrary"),
        name="chunk_attn",
    )(q, *([k] * len(rel)), *([v_t] * len(rel)), bias)


def _pair_block_diag(wa, wx):
    def pairs(w):
        z = jnp.zeros_like(w[0::2])
        top = jnp.concatenate([w[0::2], z], axis=2)
        bot = jnp.concatenate([z, w[1::2]], axis=2)
        return jnp.concatenate([top, bot], axis=1)
    return jnp.concatenate([pairs(wa), pairs(wx)], axis=2)


def _mixout_kernel(x_ref, mod_ref, gpost_ref, att_ref, rec_ref, sa_ref, sr_ref,
                   wao_ref, wro_ref, wout_ref, *rest):
    n_cast = (len(rest) - 2) // 2
    o_ref, y_ref = rest[n_cast], rest[-1]
    _cast_slabs(rest[:n_cast], rest[n_cast + 1:-1])
    _, _, gate = _mod_rows(mod_ref, 1)
    n_sub = x_ref.shape[0] // MIX_SUB_TILE

    def rows(s):
        return slice(s * MIX_SUB_TILE, (s + 1) * MIX_SUB_TILE)

    def project(s):
        att = _dot(att_ref[rows(s), :], wao_ref[...])
        rec = _dot(rec_ref[rows(s), :], wro_ref[...])
        gate_att = _sigmoid(sa_ref[rows(s), :].astype(F32))
        gate_rec = _sigmoid(sr_ref[rows(s), :].astype(F32))
        merged = gate_att * att + gate_rec * rec
        y_ref[s % 2] = _dot(merged.astype(BF16), wout_ref[...])

    def epilogue(s):
        o_ref[rows(s), :] = x_ref[rows(s), :] + gate * _rms(y_ref[s % 2], gpost_ref[1:2, :])

    for s in range(n_sub):
        project(s)
        if s > 0:
            epilogue(s - 1)
    epilogue(n_sub - 1)


def _mixout(x2, mod, norm_post, att, rec, sa, sr, w_att_o, w_rec_o, w_out, *, seq, layer,
            casts=()):
    t, d = x2.shape
    tiles_per_seq = seq // MIX_TOKEN_TILE
    n_steps = t // MIX_TOKEN_TILE
    row = lambda n: pl.BlockSpec((MIX_TOKEN_TILE, n), lambda i: (i, 0))
    cast_in, cast_specs, cast_shapes = _cast_streams(casts, layer, n_steps)
    out, *cast_out = pl.pallas_call(
        _mixout_kernel,
        grid=(n_steps,),
        in_specs=[row(d),
                  pl.BlockSpec((None, 3 * N_SUB, d), lambda i: (i // tiles_per_seq, 0, 0)),
                  _resident(norm_post.shape),
                  row(ATT_WIDTH), row(LRU_WIDTH), row(d), row(d),
                  _resident(w_att_o.shape), _resident(w_rec_o.shape), _resident(w_out.shape)]
                 + cast_in,
        out_specs=[row(d)] + cast_specs,
        out_shape=[jax.ShapeDtypeStruct((t, d), F32)] + cast_shapes,
        scratch_shapes=[pltpu.VMEM((2, MIX_SUB_TILE, d), F32)],
        compiler_params=_params("arbitrary"),
        name="mixer_out",
    )(x2, mod, norm_post, att, rec, sa, sr, w_att_o, w_rec_o, w_out, *casts)
    return out, cast_out


def _layer(x2, c, l, bsz, seq, w_ada, b_ada, norm_pre, norm_post, ffn1_w_gu, ffn1_w_down, w_in,
           rel_bias, conv_w, conv_b, lru_wa, lru_ba, lru_wx, lru_bx, lru_lambda, w_att_o,
           w_rec_o, w_out, ffn2_w_gu, ffn2_w_down):
    mod, (w_gu1_b, w_down1_b) = _ada(c, w_ada, b_ada, layer=l, casts=(ffn1_w_gu, ffn1_w_down))
    mod = mod.reshape(bsz, 3 * N_SUB, D_MODEL)
    npre, npost = norm_pre[l], norm_post[l]

    x2, (w_in_b, w_att_o_b, w_rec_o_b, w_out_b, w_down2_b) = _ffn(
        x2, mod, npre, npost, w_gu1_b, w_down1_b, sub=0, res_w=0.5, seq=seq, layer=l,
        casts=(w_in, w_att_o, w_rec_o, w_out, ffn2_w_down))

    w2 = (0.5 * _pair_block_diag(lru_wa[l], lru_wx[l])).astype(BF16)
    vec = lambda p: p[l].reshape(1, LRU_WIDTH)
    q, k, v, rec, sa, sr = _proj(x2, mod, npre, w_in_b, conv_w[l], vec(conv_b), w2,
                                 0.5 * vec(lru_ba), 0.5 * vec(lru_bx), vec(lru_lambda), seq=seq)
    att = _attention(q, k, v, _bias_table(rel_bias[l]))
    x2, (w_gu2_b,) = _mixout(
        x2, mod, npost, att.reshape(bsz * seq, ATT_WIDTH), rec, sa, sr,
        w_att_o_b, w_rec_o_b, w_out_b, seq=seq, layer=l, casts=(ffn2_w_gu,))

    return _ffn(x2, mod, npre, npost, w_gu2_b, w_down2_b, sub=2, res_w=0.5, seq=seq, layer=l)[0]


def kernel(x, c, w_ada, b_ada, norm_pre, norm_post, ffn1_w_gu, ffn1_w_down, w_in, rel_bias, conv_w, conv_b, lru_wa, lru_ba, lru_wx, lru_bx, lru_lambda, w_att_o, w_rec_o, w_out, ffn2_w_gu, ffn2_w_down):
    bsz, seq, d = x.shape
    tiles = (TOKEN_TILE, MIX_TOKEN_TILE, FFN_TOKEN_TILE, Q_TILE * Q_STEP_TILES)
    assert d == D_MODEL and all(seq % n == 0 for n in tiles)
    assert TOKEN_TILE % LRU_TILE == 0
    x2 = x.reshape(bsz * seq, d)
    for l in range(w_ada.shape[0]):
        x2 = _layer(x2, c, l, bsz, seq, w_ada, b_ada, norm_pre, norm_post, ffn1_w_gu,
                    ffn1_w_down, w_in, rel_bias, conv_w, conv_b, lru_wa, lru_ba, lru_wx, lru_bx,
                    lru_lambda, w_att_o, w_rec_o, w_out, ffn2_w_gu, ffn2_w_down)
    return x2.reshape(bsz, seq, d)
```

```python
import functools
import math

import jax
import jax.numpy as jnp
from jax import lax
from jax.experimental import pallas as pl
from jax.experimental.pallas import tpu as pltpu

D_MODEL = 1024
CHUNK = 64
LEFT_CHUNKS = 8
ATT_HEADS = 8
ATT_HEAD_DIM = 64
ATT_WIDTH = ATT_HEADS * ATT_HEAD_DIM
MAX_REL = 128
LRU_WIDTH = D_MODEL
LRU_BLOCKS = 16
LRU_BLOCK = LRU_WIDTH // LRU_BLOCKS
CONV_WIDTH = 4
LRU_C = 8.0
D_FF = 2816
N_SUB = 3
EPS = 1e-6

LANES = 128
SUBLANES = 8
BF16_ROWS = 16
VMEM_LIMIT_BYTES = 56 * 1024 * 1024

TOKEN_TILE = 512
MIX_TOKEN_TILE = 1024
MIX_SUB_TILE = 512
FFN_TOKEN_TILE = 1024
FFN_SUB_TILE = 512
FF_TILE = 256
PROJ_CHUNK = 256
ADA_TILE = 1152
Q_TILE = 4 * CHUNK
Q_STEP_TILES = 4
K_BLOCKS = LEFT_CHUNKS * CHUNK // Q_TILE + 1
BAND = K_BLOCKS * Q_TILE
LRU_TILE = 256
LRU_STEPS = LRU_TILE // SUBLANES
MASK_VALUE = -1e30
LOG2_E = math.log2(math.e)
BF16 = jnp.bfloat16
F32 = jnp.float32


def _dot(a, b):
    return jnp.dot(a, b, preferred_element_type=F32)


def _rms(x, g):
    return x * lax.rsqrt(jnp.mean(x * x, axis=-1, keepdims=True) + EPS) * g


def _modulated_norm(x, g, shift, scale):
    inv = lax.rsqrt(jnp.mean(x * x, axis=-1, keepdims=True) + EPS)
    return x * inv * (g * (1.0 + scale)) + shift


def _sigmoid(x):
    return 1.0 / (1.0 + jnp.exp(-x))


def _gelu_tanh(x):
    c = -2.0 * LOG2_E * math.sqrt(2.0 / math.pi)
    return x / (1.0 + jnp.exp2(x * (c + (0.044715 * c) * (x * x))))


def _mod_rows(mod_ref, sub):
    shift = mod_ref[3 * sub:3 * sub + 1, :]
    scale = mod_ref[3 * sub + 1:3 * sub + 2, :]
    gate = mod_ref[3 * sub + 2:3 * sub + 3, :]
    return shift, scale, gate


def _resident(shape):
    nd = len(shape)
    return pl.BlockSpec(shape, lambda *_: (0,) * nd, pipeline_mode=pl.Buffered(1))


def _params(*semantics):
    return pltpu.CompilerParams(dimension_semantics=semantics,
                                vmem_limit_bytes=VMEM_LIMIT_BYTES)


def _cast_streams(stacked, layer, n_steps):
    for w in stacked:
        assert w.shape[1] % (BF16_ROWS * n_steps) == 0 and w.shape[2] % LANES == 0, w.shape
    in_specs = [pl.BlockSpec((None, w.shape[1] // n_steps, w.shape[2]), lambda i: (layer, i, 0))
                for w in stacked]
    out_specs = [pl.BlockSpec((w.shape[1] // n_steps, w.shape[2]), lambda i: (i, 0))
                 for w in stacked]
    shapes = [jax.ShapeDtypeStruct(w.shape[1:], BF16) for w in stacked]
    return in_specs, out_specs, shapes


def _cast_slabs(in_refs, out_refs):
    for src, dst in zip(in_refs, out_refs):
        dst[...] = src[...].astype(BF16)


def _ada_kernel(c_ref, w_ref, b_ref, *rest):
    n_cast = (len(rest) - 1) // 2
    o_ref = rest[n_cast]
    _cast_slabs(rest[:n_cast], rest[n_cast + 1:])
    c = c_ref[...]
    c_act = (c * _sigmoid(c)).astype(BF16)
    o_ref[...] = _dot(c_act, w_ref[...].astype(BF16)) + b_ref[...]


def _ada(c, w_ada, b_ada, *, layer, casts=()):
    bsz, d = c.shape
    n = w_ada.shape[2]
    n_steps = n // ADA_TILE
    cast_in, cast_specs, cast_shapes = _cast_streams(casts, layer, n_steps)
    out, *cast_out = pl.pallas_call(
        _ada_kernel,
        grid=(n_steps,),
        in_specs=[pl.BlockSpec((bsz, d), lambda j: (0, 0)),
                  pl.BlockSpec((None, d, ADA_TILE), lambda j: (layer, 0, j)),
                  pl.BlockSpec((1, ADA_TILE), lambda j: (0, j))] + cast_in,
        out_specs=[pl.BlockSpec((bsz, ADA_TILE), lambda j: (0, j))] + cast_specs,
        out_shape=[jax.ShapeDtypeStruct((bsz, n), F32)] + cast_shapes,
        compiler_params=_params("arbitrary"),
        name="adaln",
    )(c, w_ada, b_ada[layer].reshape(1, n), *casts)
    return out, cast_out


def _ffn_kernel(x_ref, mod_ref, gpre_ref, gpost_ref, wgu_ref, wdn_ref, *rest, sub, res_w):
    n_cast = (len(rest) - 4) // 2
    o_ref = rest[n_cast]
    h_ref, act_ref, y_ref = rest[-3:]
    _cast_slabs(rest[:n_cast], rest[n_cast + 1:-3])
    shift, scale, gate = _mod_rows(mod_ref, sub)
    n_sub = x_ref.shape[0] // FFN_SUB_TILE

    def rows(s):
        return slice(s * FFN_SUB_TILE, (s + 1) * FFN_SUB_TILE)

    def prologue(s):
        x = x_ref[rows(s), :]
        h_ref[s % 2] = _modulated_norm(x, gpre_ref[sub:sub + 1, :], shift, scale).astype(BF16)

    def gate_up(s, j):
        lo = j * FF_TILE
        h = h_ref[s % 2]
        g = _dot(h, wgu_ref[:, lo:lo + FF_TILE])
        u = _dot(h, wgu_ref[:, D_FF + lo:D_FF + lo + FF_TILE])
        act_ref[s % 2, :, lo:lo + FF_TILE] = (g * _sigmoid(g) * u).astype(BF16)

    def down(s):
        y_ref[s % 2] = _dot(act_ref[s % 2], wdn_ref[...])

    def epilogue(s):
        y = _rms(y_ref[s % 2], gpost_ref[sub:sub + 1, :])
        o_ref[rows(s), :] = x_ref[rows(s), :] + (res_w * gate) * y

    n_ff = D_FF // FF_TILE
    prologue(0)
    for s in range(n_sub):
        for j in range(n_ff):
            gate_up(s, j)
            if j == 0 and s > 0:
                epilogue(s - 1)
            if j == n_ff // 2 and s + 1 < n_sub:
                prologue(s + 1)
        down(s)
    epilogue(n_sub - 1)


def _ffn(x2, mod, norm_pre, norm_post, w_gu, w_down, *, sub, res_w, seq, layer, casts=()):
    t, d = x2.shape
    tiles_per_seq = seq // FFN_TOKEN_TILE
    n_steps = t // FFN_TOKEN_TILE
    cast_in, cast_specs, cast_shapes = _cast_streams(casts, layer, n_steps)
    out, *cast_out = pl.pallas_call(
        functools.partial(_ffn_kernel, sub=sub, res_w=res_w),
        grid=(n_steps,),
        in_specs=[pl.BlockSpec((FFN_TOKEN_TILE, d), lambda i: (i, 0)),
                  pl.BlockSpec((None, 3 * N_SUB, d), lambda i: (i // tiles_per_seq, 0, 0)),
                  _resident(norm_pre.shape),
                  _resident(norm_post.shape),
                  _resident(w_gu.shape),
                  _resident(w_down.shape)] + cast_in,
        out_specs=[pl.BlockSpec((FFN_TOKEN_TILE, d), lambda i: (i, 0))] + cast_specs,
        out_shape=[jax.ShapeDtypeStruct((t, d), F32)] + cast_shapes,
        scratch_shapes=[pltpu.VMEM((2, FFN_SUB_TILE, d), BF16),
                        pltpu.VMEM((2, FFN_SUB_TILE, D_FF), BF16),
                        pltpu.VMEM((2, FFN_SUB_TILE, d), F32)],
        compiler_params=_params("arbitrary"),
        name=f"ffn{sub}",
    )(x2, mod, norm_pre, norm_post, w_gu, w_down, *casts)
    return out, cast_out


def _lru_unit(g, xr_ref, row0, gy_ref, o_ref, fresh, cw_ref, cb_ref, w2_ref, ba_ref, bx_ref, decay,
              tail_ref, h_ref, hs_ref):
    sub = lax.broadcasted_iota(jnp.int32, (SUBLANES, LANES), 0)
    taps = CONV_WIDTH - 1
    sl = slice(g * LANES, (g + 1) * LANES)
    x3 = xr_ref[g, row0:row0 + LRU_TILE, :].reshape(LRU_STEPS, SUBLANES, LANES)
    cur_tail = x3[LRU_STEPS - taps:]
    lead = pltpu.roll(jnp.where(sub == SUBLANES - 1, tail_ref[g] * fresh, cur_tail), 1, 1)
    tail_ref[g] = cur_tail
    xext = jnp.concatenate([lead, x3], axis=0)
    xc3 = cb_ref[:, sl] + cw_ref[0:1, sl] * xext[0:LRU_STEPS]
    for w in range(1, CONV_WIDTH):
        xc3 = xc3 + cw_ref[w:w + 1, sl] * xext[w:w + LRU_STEPS]
    xc = xc3.reshape(LRU_TILE, LANES)
    z = _dot(xc.astype(BF16), w2_ref[g])
    ta = jnp.tanh(z[:, :LANES] + ba_ref[:, sl])
    tx = jnp.tanh(z[:, LANES:] + bx_ref[:, sl])
    log_a = decay[:, sl] * (ta + 1.0)
    a = jnp.exp(log_a)
    th = jnp.tanh(log_a)
    q = (-0.5 * th) / (1.0 - th)
    half_mult = jnp.where(q > 0.0, q * lax.rsqrt(q), 0.0)
    u = half_mult * ((tx + 1.0) * xc)
    a3 = a.reshape(LRU_STEPS, SUBLANES, LANES)
    u3 = u.reshape(LRU_STEPS, SUBLANES, LANES)
    loc = [u3[0]]
    prod = [a3[0]]
    for i in range(1, LRU_STEPS):
        loc.append(a3[i] * loc[-1] + u3[i])
        prod.append(a3[i] * prod[-1])
    blk_a, blk_u = prod[-1], loc[-1]
    for s in (1, 2, 4):
        keep = sub >= s
        a_prev = jnp.where(keep, pltpu.roll(blk_a, s, 0), 1.0)
        u_prev = jnp.where(keep, pltpu.roll(blk_u, s, 0), 0.0)
        blk_u = blk_u + blk_a * u_prev
        blk_a = blk_a * a_prev
    h_in = h_ref[g] * fresh
    end = blk_u + blk_a * h_in
    entry = jnp.where(sub == 0, h_in, pltpu.roll(end, 1, 0))
    h_ref[g] = jnp.broadcast_to(end[SUBLANES - 1:SUBLANES, :], (SUBLANES, LANES))
    for i in range(LRU_STEPS):
        hs_ref[g, i * SUBLANES:(i + 1) * SUBLANES, :] = loc[i] + prod[i] * entry
    rows = []
    for n in range(LRU_TILE // SUBLANES):
        j, i0 = divmod(n * SUBLANES, LRU_STEPS)
        rows.append(hs_ref[g, pl.ds(i0 * SUBLANES + j, SUBLANES, stride=SUBLANES), :])
    hs = jnp.concatenate(rows, axis=0)
    o_ref[row0:row0 + LRU_TILE, sl] = (hs * gy_ref[row0:row0 + LRU_TILE, sl]).astype(BF16)


def _proj_kernel(x_ref, mod_ref, gpre_ref, w_ref, cw_ref, cb_ref, w2_ref, ba_ref, bx_ref, lam_ref,
                 q_ref, k_ref, v_ref, rec_ref, sa_ref, sr_ref,
                 xr_s, gy_s, tail_ref, h_ref, hs_ref, *, tiles_per_seq, n_tiles):
    i = pl.program_id(0)
    rows = x_ref.shape[0]
    cw = PROJ_CHUNK
    per_chunk = cw // LANES
    lru0 = 3 * ATT_WIDTH
    n_att, n_lru = ATT_WIDTH // cw, LRU_WIDTH // cw

    @pl.when(i == 0)
    def _():
        for ref in (xr_s, gy_s, tail_ref, h_ref):
            ref[...] = jnp.zeros_like(ref)

    def scan_units():
        neg_lam = -lam_ref[...]
        softplus = jnp.maximum(neg_lam, 0.0) + jnp.log1p(jnp.exp(-jnp.abs(neg_lam)))
        decay = (-0.5 * LRU_C) * softplus
        first = jnp.where(lax.rem(i - 1, tiles_per_seq) == 0, 0.0, 1.0)

        def unit(row0, g):
            _lru_unit(g, xr_s, row0, gy_s, rec_ref, first if row0 == 0 else 1.0, cw_ref, cb_ref,
                      w2_ref, ba_ref, bx_ref, decay, tail_ref, h_ref, hs_ref)

        return [functools.partial(unit, row0, g)
                for row0 in range(0, rows, LRU_TILE) for g in range(LRU_WIDTH // LANES)]

    @pl.when(i < n_tiles)
    def _():
        x = x_ref[...]
        shift, scale, _ = _mod_rows(mod_ref, 1)
        h = _modulated_norm(x, gpre_ref[1:2, :], shift, scale).astype(BF16)

        def proj(col):
            return _dot(h, w_ref[:, col:col + cw])

        def q_chunk(c):
            q = (proj(c * cw) * (ATT_HEAD_DIM ** -0.5 * LOG2_E)).astype(BF16)
            for p in range(per_chunk):
                q_ref[c * per_chunk + p] = q[:, p * LANES:(p + 1) * LANES]

        def k_chunk(c):
            k = proj(ATT_WIDTH + c * cw).astype(BF16)
            for p in range(per_chunk):
                k_ref[c * per_chunk + p] = k[:, p * LANES:(p + 1) * LANES]

        def v_chunk(c):
            heads = cw // ATT_HEAD_DIM
            v_t = proj(2 * ATT_WIDTH + c * cw).T.astype(BF16)
            v_ref[c * heads:(c + 1) * heads] = v_t.reshape(heads, ATT_HEAD_DIM, rows)

        def gate_chunk(ref, col0, c):
            ref[:, c * cw:(c + 1) * cw] = proj(col0 + c * cw).astype(BF16)

        def gy_chunk(c):
            gy_s[:, c * cw:(c + 1) * cw] = _gelu_tanh(proj(lru0 + LRU_WIDTH + c * cw))

        def xr_chunk(c):
            xr = proj(lru0 + c * cw)
            for src_row in range(0, rows, LRU_STEPS):
                tile0, j = src_row // LRU_TILE * LRU_TILE, src_row % LRU_TILE // LRU_STEPS
                for p in range(per_chunk):
                    xr_s[c * per_chunk + p, pl.ds(tile0 + j, LRU_STEPS, stride=SUBLANES), :] = (
                        xr[src_row:src_row + LRU_STEPS, p * LANES:(p + 1) * LANES])

        chunks = ([functools.partial(q_chunk, c) for c in range(n_att)]
                  + [functools.partial(k_chunk, c) for c in range(n_att)]
                  + [functools.partial(v_chunk, c) for c in range(n_att)]
                  + [functools.partial(gate_chunk, sa_ref, lru0 + 2 * LRU_WIDTH, c) for c in range(n_lru)]
                  + [functools.partial(gate_chunk, sr_ref, lru0 + 3 * LRU_WIDTH, c) for c in range(n_lru)]
                  + [functools.partial(gy_chunk, c) for c in range(n_lru)]
                  + [functools.partial(xr_chunk, c) for c in range(n_lru)])
        units = scan_units()
        for n, chunk in enumerate(chunks):
            chunk()
            if n < len(units):
                units[n]()

    @pl.when(i == n_tiles)
    def _():
        for unit in scan_units():
            unit()


def _proj(x2, mod, norm_pre, w_in, conv_w, conv_b, w2, ba, bx, lam, *, seq):
    t, d = x2.shape
    tiles_per_seq = seq // TOKEN_TILE
    n_tiles = t // TOKEN_TILE
    cur = lambda i: jnp.minimum(i, n_tiles - 1)
    row = lambda n: pl.BlockSpec((TOKEN_TILE, n), lambda i: (cur(i), 0))
    tok = lambda n, dt: jax.ShapeDtypeStruct((t, n), dt)
    bsz, pairs, groups = t // seq, ATT_WIDTH // LANES, LRU_WIDTH // LANES
    qk_spec = pl.BlockSpec((None, pairs, TOKEN_TILE, LANES),
                           lambda i: (cur(i) // tiles_per_seq, 0, cur(i) % tiles_per_seq, 0))
    qk_shape = jax.ShapeDtypeStruct((bsz, pairs, seq, LANES), BF16)
    v_spec = pl.BlockSpec((None, ATT_HEADS, ATT_HEAD_DIM, TOKEN_TILE),
                          lambda i: (cur(i) // tiles_per_seq, 0, 0, cur(i) % tiles_per_seq))
    v_shape = jax.ShapeDtypeStruct((bsz, ATT_HEADS, ATT_HEAD_DIM, seq), BF16)
    rec_spec = pl.BlockSpec((TOKEN_TILE, LRU_WIDTH), lambda i: (jnp.maximum(i - 1, 0), 0))
    return pl.pallas_call(
        functools.partial(_proj_kernel, tiles_per_seq=tiles_per_seq, n_tiles=n_tiles),
        grid=(n_tiles + 1,),
        in_specs=[row(d),
                  pl.BlockSpec((None, 3 * N_SUB, d), lambda i: (cur(i) // tiles_per_seq, 0, 0)),
                  _resident(norm_pre.shape), _resident(w_in.shape),
                  _resident(conv_w.shape), _resident(conv_b.shape), _resident(w2.shape),
                  _resident(ba.shape), _resident(bx.shape), _resident(lam.shape)],
        out_specs=[qk_spec, qk_spec, v_spec, rec_spec, row(D_MODEL), row(D_MODEL)],
        out_shape=[qk_shape, qk_shape, v_shape, tok(LRU_WIDTH, BF16),
                   tok(D_MODEL, BF16), tok(D_MODEL, BF16)],
        scratch_shapes=[pltpu.VMEM((groups, TOKEN_TILE, LANES), F32),
                        pltpu.VMEM((TOKEN_TILE, LRU_WIDTH), F32),
                        pltpu.VMEM((groups, CONV_WIDTH - 1, SUBLANES, LANES), F32),
                        pltpu.VMEM((groups, SUBLANES, LANES), F32),
                        pltpu.VMEM((groups, LRU_TILE, LANES), F32)],
        compiler_params=_params("arbitrary"),
        name="mixer_proj",
    )(x2, mod, norm_pre, w_in, conv_w, conv_b, w2, ba, bx, lam)


REL_PAD = 384
TOEPLITZ = 1024


def _bias_kernel(tab_ref, o_ref):
    tab = tab_ref[...]
    hi = tab.astype(BF16)
    r1 = tab - hi.astype(F32)
    mid = r1.astype(BF16)
    lo = (r1 - mid.astype(F32)).astype(BF16)
    d_idx = lax.broadcasted_iota(jnp.int32, (REL_PAD, TOEPLITZ), 0)
    m_idx = lax.broadcasted_iota(jnp.int32, (REL_PAD, TOEPLITZ), 1)
    rel = jnp.clip(m_idx - (Q_TILE - 1), -MAX_REL, MAX_REL) + MAX_REL
    onehot = jnp.where(d_idx == rel, 1.0, 0.0).astype(BF16)
    profile = ((_dot(lo, onehot) + _dot(mid, onehot)) + _dot(hi, onehot)) * LOG2_E

    kk = lax.broadcasted_iota(jnp.int32, (BAND, Q_TILE), 0)
    r = lax.broadcasted_iota(jnp.int32, (BAND, Q_TILE), 1)
    qc = r // CHUNK
    kc = kk // CHUNK
    band = jnp.where((kc >= qc) & (kc <= qc + LEFT_CHUNKS), 0.0, MASK_VALUE)
    for h in range(ATT_HEADS):
        rows = jnp.broadcast_to(profile[h:h + 1, :], (BAND, TOEPLITZ))
        skew = pltpu.roll(rows, TOEPLITZ - (BAND - 1), 1, stride=1, stride_axis=0)
        o_ref[h] = skew[:, :Q_TILE] + band


def _bias_table(rel_bias):
    tab = jnp.pad(rel_bias, ((0, 0), (0, REL_PAD - rel_bias.shape[1])))
    return pl.pallas_call(
        _bias_kernel,
        out_shape=jax.ShapeDtypeStruct((ATT_HEADS, BAND, Q_TILE), F32),
        compiler_params=pltpu.CompilerParams(vmem_limit_bytes=VMEM_LIMIT_BYTES),
        name="rel_bias",
    )(tab)


def _attn_kernel(q_ref, *refs):
    n_kv = K_BLOCKS - 1 + Q_STEP_TILES
    k_refs, v_refs = refs[:n_kv], refs[n_kv:2 * n_kv]
    bias_ref, o_ref, acc_ref = refs[2 * n_kv:2 * n_kv + 3]
    s_refs, p_refs = refs[-4:-2], refs[-2:]
    first_tile = pl.program_id(1) * Q_STEP_TILES
    lane_head = lax.broadcasted_iota(jnp.int32, (1, LANES), 1) // ATT_HEAD_DIM
    zero = jnp.zeros((), BF16)
    live = [(2 * c * CHUNK, (2 * c + 2 + LEFT_CHUNKS) * CHUNK) for c in range(Q_TILE // LANES)]

    @pl.when((pl.program_id(0) == 0) & (pl.program_id(1) == 0))
    def _():
        for p_ref in p_refs:
            p_ref[...] = jnp.zeros_like(p_ref)

    def start_pens(u):
        return [jnp.where(first_tile + u + j < K_BLOCKS - 1, MASK_VALUE, 0.0)
                for j in range(K_BLOCKS - 1)] + [0.0]

    def scores(u, h, slot):
        pair = h // 2
        qh = jnp.where(lane_head == h % 2, q_ref[pair, u * Q_TILE:(u + 1) * Q_TILE, :], zero)
        kcat = jnp.concatenate([r[pair] for r in k_refs[u:u + K_BLOCKS]], axis=0)
        s = lax.dot_general(kcat, qh, (((1,), (1,)), ((), ())), preferred_element_type=F32)
        s_refs[slot][...] = s + bias_ref[h]

    def col_reduce(parts, op):
        while len(parts) > 1:
            parts = [op(parts[i], parts[i + 1]) if i + 1 < len(parts) else parts[i]
                     for i in range(0, len(parts), 2)]
        return parts[0]

    def softmax(u, slot):
        s_ref, p_ref = s_refs[slot], p_refs[slot]
        pens = start_pens(u)
        groups = CHUNK // SUBLANES
        inv = []
        for c, (r0, r1) in enumerate(live):
            cl = slice(c * LANES, (c + 1) * LANES)
            blocks = range(r0, r1, CHUNK)
            part = [s_ref[r:r + CHUNK, cl].reshape(groups, SUBLANES, LANES).max(axis=0)
                    + pens[r // Q_TILE] for r in blocks]
            m = jnp.max(col_reduce(part, jnp.maximum), axis=0, keepdims=True)
            shifted = [m - pen for pen in pens]
            sums = []
            for r in blocks:
                pr = jnp.exp2(s_ref[r:r + CHUNK, cl] - shifted[r // Q_TILE])
                sums.append(pr.reshape(groups, SUBLANES, LANES).sum(axis=0))
                p_ref[r:r + CHUNK, cl] = pr.astype(BF16)
            l = jnp.sum(col_reduce(sums, jnp.add), axis=0, keepdims=True)
            inv.append(1.0 / l)
        return jnp.concatenate(inv, axis=1)

    def weighted_values(u, h, slot, inv_l):
        v_t = jnp.concatenate([r[h] for r in v_refs[u:u + K_BLOCKS]], axis=1)
        acc_ref[u, h] = _dot(v_t, p_refs[slot][...]) * inv_l

    items = [(u, h) for u in range(Q_STEP_TILES) for h in range(ATT_HEADS)]
    scores(*items[0], 0)
    inv_prev = None
    for n, (u, h) in enumerate(items):
        if n + 1 < len(items):
            scores(*items[n + 1], (n + 1) % 2)
        inv_l = softmax(u, n % 2)
        if n > 0:
            weighted_values(*items[n - 1], (n - 1) % 2, inv_prev)
        inv_prev = inv_l
    weighted_values(*items[-1], (len(items) - 1) % 2, inv_prev)
    for u in range(Q_STEP_TILES):
        for p in range(ATT_HEADS // 2):
            pair_t = acc_ref[u, 2 * p:2 * p + 2].reshape(LANES, Q_TILE)
            o_ref[u * Q_TILE:(u + 1) * Q_TILE, p * LANES:(p + 1) * LANES] = pair_t.T.astype(BF16)


def _attention(q, k, v_t, bias):
    bsz, pairs, seq, _ = q.shape
    step_rows = Q_STEP_TILES * Q_TILE
    rel = range(-(K_BLOCKS - 1), Q_STEP_TILES)
    blk = lambda t, n: jnp.maximum(t * Q_STEP_TILES + n, 0)
    key = lambda n: pl.BlockSpec((None, pairs, Q_TILE, LANES), lambda b, t: (b, 0, blk(t, n), 0))
    val = lambda n: pl.BlockSpec((None, ATT_HEADS, ATT_HEAD_DIM, Q_TILE),
                                 lambda b, t: (b, 0, 0, blk(t, n)))
    slot = lambda dt: pltpu.VMEM((BAND, Q_TILE), dt)
    return pl.pallas_call(
        _attn_kernel,
        grid=(bsz, seq // step_rows),
        in_specs=[pl.BlockSpec((None, pairs, step_rows, LANES), lambda b, t: (b, 0, t, 0))]
                 + [key(n) for n in rel] + [val(n) for n in rel] + [_resident(bias.shape)],
        out_specs=pl.BlockSpec((None, step_rows, ATT_WIDTH), lambda b, t: (b, t, 0)),
        out_shape=jax.ShapeDtypeStruct((bsz, seq, ATT_WIDTH), BF16),
        scratch_shapes=[pltpu.VMEM((Q_STEP_TILES, ATT_HEADS, ATT_HEAD_DIM, Q_TILE), F32),
                        slot(F32), slot(F32), slot(BF16), slot(BF16)],
        compiler_params=_params("arbitrary", "arbitrary"),
        name="chunk_attn",
    )(q, *([k] * len(rel)), *([v_t] * len(rel)), bias)


def _pair_block_diag(wa, wx):
    def pairs(w):
        z = jnp.zeros_like(w[0::2])
        top = jnp.concatenate([w[0::2], z], axis=2)
        bot = jnp.concatenate([z, w[1::2]], axis=2)
        return jnp.concatenate([top, bot], axis=1)
    return jnp.concatenate([pairs(wa), pairs(wx)], axis=2)


def _mixout_kernel(x_ref, mod_ref, gpost_ref, att_ref, rec_ref, sa_ref, sr_ref,
                   wao_ref, wro_ref, wout_ref, *rest):
    n_cast = (len(rest) - 3) // 2
    o_ref, m_ref, y_ref = rest[n_cast], rest[-2], rest[-1]
    _cast_slabs(rest[:n_cast], rest[n_cast + 1:-2])
    _, _, gate = _mod_rows(mod_ref, 1)
    n_sub = x_ref.shape[0] // MIX_SUB_TILE

    def rows(s):
        return slice(s * MIX_SUB_TILE, (s + 1) * MIX_SUB_TILE)

    def merge(s):
        att = _dot(att_ref[rows(s), :], wao_ref[...])
        rec = _dot(rec_ref[rows(s), :], wro_ref[...])
        gate_att = _sigmoid(sa_ref[rows(s), :].astype(F32))
        gate_rec = _sigmoid(sr_ref[rows(s), :].astype(F32))
        m_ref[s % 2] = (gate_att * att + gate_rec * rec).astype(BF16)

    def project(s):
        y_ref[s % 2] = _dot(m_ref[s % 2], wout_ref[...])

    def epilogue(s):
        o_ref[rows(s), :] = x_ref[rows(s), :] + gate * _rms(y_ref[s % 2], gpost_ref[1:2, :])

    merge(0)
    for s in range(n_sub):
        if s + 1 < n_sub:
            merge(s + 1)
        project(s)
        if s > 0:
            epilogue(s - 1)
    epilogue(n_sub - 1)


def _mixout(x2, mod, norm_post, att, rec, sa, sr, w_att_o, w_rec_o, w_out, *, seq, layer,
            casts=()):
    t, d = x2.shape
    tiles_per_seq = seq // MIX_TOKEN_TILE
    n_steps = t // MIX_TOKEN_TILE
    row = lambda n: pl.BlockSpec((MIX_TOKEN_TILE, n), lambda i: (i, 0))
    cast_in, cast_specs, cast_shapes = _cast_streams(casts, layer, n_steps)
    out, *cast_out = pl.pallas_call(
        _mixout_kernel,
        grid=(n_steps,),
        in_specs=[row(d),
                  pl.BlockSpec((None, 3 * N_SUB, d), lambda i: (i // tiles_per_seq, 0, 0)),
                  _resident(norm_post.shape),
                  row(ATT_WIDTH), row(LRU_WIDTH), row(d), row(d),
                  _resident(w_att_o.shape), _resident(w_rec_o.shape), _resident(w_out.shape)]
                 + cast_in,
        out_specs=[row(d)] + cast_specs,
        out_shape=[jax.ShapeDtypeStruct((t, d), F32)] + cast_shapes,
        scratch_shapes=[pltpu.VMEM((2, MIX_SUB_TILE, d), BF16),
                        pltpu.VMEM((2, MIX_SUB_TILE, d), F32)],
        compiler_params=_params("arbitrary"),
        name="mixer_out",
    )(x2, mod, norm_post, att, rec, sa, sr, w_att_o, w_rec_o, w_out, *casts)
    return out, cast_out


def _layer(x2, c, l, bsz, seq, w_ada, b_ada, norm_pre, norm_post, ffn1_w_gu, ffn1_w_down, w_in,
           rel_bias, conv_w, conv_b, lru_wa, lru_ba, lru_wx, lru_bx, lru_lambda, w_att_o,
           w_rec_o, w_out, ffn2_w_gu, ffn2_w_down):
    mod, (w_gu1_b, w_down1_b) = _ada(c, w_ada, b_ada, layer=l, casts=(ffn1_w_gu, ffn1_w_down))
    mod = mod.reshape(bsz, 3 * N_SUB, D_MODEL)
    npre, npost = norm_pre[l], norm_post[l]

    x2, (w_in_b, w_att_o_b, w_rec_o_b, w_out_b, w_down2_b) = _ffn(
        x2, mod, npre, npost, w_gu1_b, w_down1_b, sub=0, res_w=0.5, seq=seq, layer=l,
        casts=(w_in, w_att_o, w_rec_o, w_out, ffn2_w_down))

    w2 = (0.5 * _pair_block_diag(lru_wa[l], lru_wx[l])).astype(BF16)
    vec = lambda p: p[l].reshape(1, LRU_WIDTH)
    q, k, v, rec, sa, sr = _proj(x2, mod, npre, w_in_b, conv_w[l], vec(conv_b), w2,
                                 0.5 * vec(lru_ba), 0.5 * vec(lru_bx), vec(lru_lambda), seq=seq)
    att = _attention(q, k, v, _bias_table(rel_bias[l]))
    x2, (w_gu2_b,) = _mixout(
        x2, mod, npost, att.reshape(bsz * seq, ATT_WIDTH), rec, sa, sr,
        w_att_o_b, w_rec_o_b, w_out_b, seq=seq, layer=l, casts=(ffn2_w_gu,))

    return _ffn(x2, mod, npre, npost, w_gu2_b, w_down2_b, sub=2, res_w=0.5, seq=seq, layer=l)[0]


def kernel(x, c, w_ada, b_ada, norm_pre, norm_post, ffn1_w_gu, ffn1_w_down, w_in, rel_bias, conv_w, conv_b, lru_wa, lru_ba, lru_wx, lru_bx, lru_lambda, w_att_o, w_rec_o, w_out, ffn2_w_gu, ffn2_w_down):
    bsz, seq, d = x.shape
    tiles = (TOKEN_TILE, MIX_TOKEN_TILE, FFN_TOKEN_TILE, Q_TILE * Q_STEP_TILES)
    assert d == D_MODEL and all(seq % n == 0 for n in tiles)
    assert TOKEN_TILE % LRU_TILE == 0
    x2 = x.reshape(bsz * seq, d)
    for l in range(w_ada.shape[0]):
        x2 = _layer(x2, c, l, bsz, seq, w_ada, b_ada, norm_pre, norm_post, ffn1_w_gu,
                    ffn1_w_down, w_in, rel_bias, conv_w, conv_b, lru_wa, lru_ba, lru_wx, lru_bx,
                    lru_lambda, w_att_o, w_rec_o, w_out, ffn2_w_gu, ffn2_w_down)
    return x2.reshape(bsz, seq, d)
```

```python
import functools
import math

import jax
import jax.numpy as jnp
from jax import lax
from jax.experimental import pallas as pl
from jax.experimental.pallas import tpu as pltpu

D_MODEL = 1024
CHUNK = 64
LEFT_CHUNKS = 8
ATT_HEADS = 8
ATT_HEAD_DIM = 64
ATT_WIDTH = ATT_HEADS * ATT_HEAD_DIM
MAX_REL = 128
LRU_WIDTH = D_MODEL
LRU_BLOCKS = 16
LRU_BLOCK = LRU_WIDTH // LRU_BLOCKS
CONV_WIDTH = 4
LRU_C = 8.0
D_FF = 2816
N_SUB = 3
EPS = 1e-6

LANES = 128
SUBLANES = 8
BF16_ROWS = 16
VMEM_LIMIT_BYTES = 56 * 1024 * 1024

TOKEN_TILE = 512
MIX_TOKEN_TILE = 1024
MIX_SUB_TILE = 512
FFN_TOKEN_TILE = 1024
FFN_SUB_TILE = 512
FF_TILE = 256
PROJ_CHUNK = 256
ADA_TILE = 1152
Q_TILE = 4 * CHUNK
Q_STEP_TILES = 4
K_BLOCKS = LEFT_CHUNKS * CHUNK // Q_TILE + 1
BAND = K_BLOCKS * Q_TILE
LRU_TILE = 256
LRU_STEPS = LRU_TILE // SUBLANES
MASK_VALUE = -1e30
SMALLEST_NORMAL = float(jnp.finfo(jnp.float32).tiny)
LOG2_E = math.log2(math.e)
BF16 = jnp.bfloat16
F32 = jnp.float32


def _dot(a, b):
    return jnp.dot(a, b, preferred_element_type=F32)


def _rms(x, g):
    return x * lax.rsqrt(jnp.mean(x * x, axis=-1, keepdims=True) + EPS) * g


def _modulated_norm(x, g, shift, scale):
    inv = lax.rsqrt(jnp.mean(x * x, axis=-1, keepdims=True) + EPS)
    return x * inv * (g * (1.0 + scale)) + shift


def _sigmoid(x):
    return 1.0 / (1.0 + jnp.exp(-x))


def _gelu_tanh(x):
    c = -2.0 * LOG2_E * math.sqrt(2.0 / math.pi)
    return x / (1.0 + jnp.exp2(x * (c + (0.044715 * c) * (x * x))))


def _mod_rows(mod_ref, sub):
    shift = mod_ref[3 * sub:3 * sub + 1, :]
    scale = mod_ref[3 * sub + 1:3 * sub + 2, :]
    gate = mod_ref[3 * sub + 2:3 * sub + 3, :]
    return shift, scale, gate


def _resident(shape):
    nd = len(shape)
    return pl.BlockSpec(shape, lambda *_: (0,) * nd, pipeline_mode=pl.Buffered(1))


def _params(*semantics):
    return pltpu.CompilerParams(dimension_semantics=semantics,
                                vmem_limit_bytes=VMEM_LIMIT_BYTES)


def _cast_streams(stacked, layer, n_steps):
    for w in stacked:
        assert w.shape[1] % (BF16_ROWS * n_steps) == 0 and w.shape[2] % LANES == 0, w.shape
    in_specs = [pl.BlockSpec((None, w.shape[1] // n_steps, w.shape[2]), lambda i: (layer, i, 0))
                for w in stacked]
    out_specs = [pl.BlockSpec((w.shape[1] // n_steps, w.shape[2]), lambda i: (i, 0))
                 for w in stacked]
    shapes = [jax.ShapeDtypeStruct(w.shape[1:], BF16) for w in stacked]
    return in_specs, out_specs, shapes


def _cast_slabs(in_refs, out_refs):
    for src, dst in zip(in_refs, out_refs):
        dst[...] = src[...].astype(BF16)


def _ada_kernel(c_ref, w_ref, b_ref, *rest):
    n_cast = (len(rest) - 1) // 2
    o_ref = rest[n_cast]
    _cast_slabs(rest[:n_cast], rest[n_cast + 1:])
    c = c_ref[...]
    c_act = (c * _sigmoid(c)).astype(BF16)
    o_ref[...] = _dot(c_act, w_ref[...].astype(BF16)) + b_ref[...]


def _ada(c, w_ada, b_ada, *, layer, casts=()):
    bsz, d = c.shape
    n = w_ada.shape[2]
    n_steps = n // ADA_TILE
    cast_in, cast_specs, cast_shapes = _cast_streams(casts, layer, n_steps)
    out, *cast_out = pl.pallas_call(
        _ada_kernel,
        grid=(n_steps,),
        in_specs=[pl.BlockSpec((bsz, d), lambda j: (0, 0)),
                  pl.BlockSpec((None, d, ADA_TILE), lambda j: (layer, 0, j)),
                  pl.BlockSpec((1, ADA_TILE), lambda j: (0, j))] + cast_in,
        out_specs=[pl.BlockSpec((bsz, ADA_TILE), lambda j: (0, j))] + cast_specs,
        out_shape=[jax.ShapeDtypeStruct((bsz, n), F32)] + cast_shapes,
        compiler_params=_params("arbitrary"),
        name="adaln",
    )(c, w_ada, b_ada[layer].reshape(1, n), *casts)
    return out, cast_out


def _ffn_kernel(x_ref, mod_ref, gpre_ref, gpost_ref, wgu_ref, wdn_ref, *rest, sub, res_w):
    n_cast = (len(rest) - 4) // 2
    o_ref = rest[n_cast]
    h_ref, act_ref, y_ref = rest[-3:]
    _cast_slabs(rest[:n_cast], rest[n_cast + 1:-3])
    shift, scale, gate = _mod_rows(mod_ref, sub)
    n_sub = x_ref.shape[0] // FFN_SUB_TILE

    def rows(s):
        return slice(s * FFN_SUB_TILE, (s + 1) * FFN_SUB_TILE)

    def prologue(s):
        x = x_ref[rows(s), :]
        h_ref[s % 2] = _modulated_norm(x, gpre_ref[sub:sub + 1, :], shift, scale).astype(BF16)

    def gate_up(s, j):
        lo = j * FF_TILE
        h = h_ref[s % 2]
        g = _dot(h, wgu_ref[:, lo:lo + FF_TILE])
        u = _dot(h, wgu_ref[:, D_FF + lo:D_FF + lo + FF_TILE])
        act_ref[s % 2, :, lo:lo + FF_TILE] = (g * _sigmoid(g) * u).astype(BF16)

    def down(s):
        y_ref[s % 2] = _dot(act_ref[s % 2], wdn_ref[...])

    def epilogue(s):
        y = _rms(y_ref[s % 2], gpost_ref[sub:sub + 1, :])
        o_ref[rows(s), :] = x_ref[rows(s), :] + (res_w * gate) * y

    n_ff = D_FF // FF_TILE
    prologue(0)
    for s in range(n_sub):
        for j in range(n_ff):
            gate_up(s, j)
            if j == 0 and s > 0:
                epilogue(s - 1)
            if j == n_ff // 2 and s + 1 < n_sub:
                prologue(s + 1)
        down(s)
    epilogue(n_sub - 1)


def _ffn(x2, mod, norm_pre, norm_post, w_gu, w_down, *, sub, res_w, seq, layer, casts=()):
    t, d = x2.shape
    tiles_per_seq = seq // FFN_TOKEN_TILE
    n_steps = t // FFN_TOKEN_TILE
    cast_in, cast_specs, cast_shapes = _cast_streams(casts, layer, n_steps)
    out, *cast_out = pl.pallas_call(
        functools.partial(_ffn_kernel, sub=sub, res_w=res_w),
        grid=(n_steps,),
        in_specs=[pl.BlockSpec((FFN_TOKEN_TILE, d), lambda i: (i, 0)),
                  pl.BlockSpec((None, 3 * N_SUB, d), lambda i: (i // tiles_per_seq, 0, 0)),
                  _resident(norm_pre.shape),
                  _resident(norm_post.shape),
                  _resident(w_gu.shape),
                  _resident(w_down.shape)] + cast_in,
        out_specs=[pl.BlockSpec((FFN_TOKEN_TILE, d), lambda i: (i, 0))] + cast_specs,
        out_shape=[jax.ShapeDtypeStruct((t, d), F32)] + cast_shapes,
        scratch_shapes=[pltpu.VMEM((2, FFN_SUB_TILE, d), BF16),
                        pltpu.VMEM((2, FFN_SUB_TILE, D_FF), BF16),
                        pltpu.VMEM((2, FFN_SUB_TILE, d), F32)],
        compiler_params=_params("arbitrary"),
        name=f"ffn{sub}",
    )(x2, mod, norm_pre, norm_post, w_gu, w_down, *casts)
    return out, cast_out


def _lru_unit(g, xr_ref, row0, gy_ref, o_ref, fresh, cw_ref, cb_ref, w2_ref, ba_ref, bx_ref, decay,
              tail_ref, h_ref, hs_ref):
    sub = lax.broadcasted_iota(jnp.int32, (SUBLANES, LANES), 0)
    taps = CONV_WIDTH - 1
    sl = slice(g * LANES, (g + 1) * LANES)
    x3 = xr_ref[g, row0:row0 + LRU_TILE, :].reshape(LRU_STEPS, SUBLANES, LANES)
    cur_tail = x3[LRU_STEPS - taps:]
    lead = pltpu.roll(jnp.where(sub == SUBLANES - 1, tail_ref[g] * fresh, cur_tail), 1, 1)
    tail_ref[g] = cur_tail
    xext = jnp.concatenate([lead, x3], axis=0)
    xc3 = cb_ref[:, sl] + cw_ref[0:1, sl] * xext[0:LRU_STEPS]
    for w in range(1, CONV_WIDTH):
        xc3 = xc3 + cw_ref[w:w + 1, sl] * xext[w:w + LRU_STEPS]
    xc = xc3.reshape(LRU_TILE, LANES)
    z = _dot(xc.astype(BF16), w2_ref[g])
    ta = jnp.tanh(z[:, :LANES] + ba_ref[:, sl])
    tx = jnp.tanh(z[:, LANES:] + bx_ref[:, sl])
    log_a = decay[:, sl] * (ta + 1.0)
    a = jnp.exp(log_a)
    th = jnp.tanh(log_a)
    q = (-0.5 * th) / (1.0 - th)
    half_mult = q * lax.rsqrt(jnp.maximum(q, SMALLEST_NORMAL))
    u = half_mult * ((tx + 1.0) * xc)
    a3 = a.reshape(LRU_STEPS, SUBLANES, LANES)
    u3 = u.reshape(LRU_STEPS, SUBLANES, LANES)
    loc = [u3[0]]
    prod = [a3[0]]
    for i in range(1, LRU_STEPS):
        loc.append(a3[i] * loc[-1] + u3[i])
        prod.append(a3[i] * prod[-1])
    blk_a, blk_u = prod[-1], loc[-1]
    for s in (1, 2, 4):
        keep = sub >= s
        a_prev = jnp.where(keep, pltpu.roll(blk_a, s, 0), 1.0)
        u_prev = jnp.where(keep, pltpu.roll(blk_u, s, 0), 0.0)
        blk_u = blk_u + blk_a * u_prev
        blk_a = blk_a * a_prev
    h_in = h_ref[g] * fresh
    end = blk_u + blk_a * h_in
    entry = jnp.where(sub == 0, h_in, pltpu.roll(end, 1, 0))
    h_ref[g] = jnp.broadcast_to(end[SUBLANES - 1:SUBLANES, :], (SUBLANES, LANES))
    for i in range(LRU_STEPS):
        hs_ref[g, i * SUBLANES:(i + 1) * SUBLANES, :] = loc[i] + prod[i] * entry
    rows = []
    for n in range(LRU_TILE // SUBLANES):
        j, i0 = divmod(n * SUBLANES, LRU_STEPS)
        rows.append(hs_ref[g, pl.ds(i0 * SUBLANES + j, SUBLANES, stride=SUBLANES), :])
    hs = jnp.concatenate(rows, axis=0)
    o_ref[row0:row0 + LRU_TILE, sl] = (hs * gy_ref[row0:row0 + LRU_TILE, sl]).astype(BF16)


def _proj_kernel(x_ref, mod_ref, gpre_ref, w_ref, cw_ref, cb_ref, w2_ref, ba_ref, bx_ref, lam_ref,
                 q_ref, k_ref, v_ref, rec_ref, sa_ref, sr_ref,
                 xr_s, gy_s, tail_ref, h_ref, hs_ref, *, tiles_per_seq, n_tiles):
    i = pl.program_id(0)
    rows = x_ref.shape[0]
    cw = PROJ_CHUNK
    per_chunk = cw // LANES
    lru0 = 3 * ATT_WIDTH
    n_att, n_lru = ATT_WIDTH // cw, LRU_WIDTH // cw

    @pl.when(i == 0)
    def _():
        for ref in (xr_s, gy_s, tail_ref, h_ref):
            ref[...] = jnp.zeros_like(ref)

    def scan_units():
        neg_lam = -lam_ref[...]
        softplus = jnp.maximum(neg_lam, 0.0) + jnp.log1p(jnp.exp(-jnp.abs(neg_lam)))
        decay = (-0.5 * LRU_C) * softplus
        first = jnp.where(lax.rem(i - 1, tiles_per_seq) == 0, 0.0, 1.0)

        def unit(row0, g):
            _lru_unit(g, xr_s, row0, gy_s, rec_ref, first if row0 == 0 else 1.0, cw_ref, cb_ref,
                      w2_ref, ba_ref, bx_ref, decay, tail_ref, h_ref, hs_ref)

        return [functools.partial(unit, row0, g)
                for row0 in range(0, rows, LRU_TILE) for g in range(LRU_WIDTH // LANES)]

    @pl.when(i < n_tiles)
    def _():
        x = x_ref[...]
        shift, scale, _ = _mod_rows(mod_ref, 1)
        h = _modulated_norm(x, gpre_ref[1:2, :], shift, scale).astype(BF16)

        def proj(col):
            return _dot(h, w_ref[:, col:col + cw])

        def q_chunk(c):
            q = (proj(c * cw) * (ATT_HEAD_DIM ** -0.5 * LOG2_E)).astype(BF16)
            for p in range(per_chunk):
                q_ref[c * per_chunk + p] = q[:, p * LANES:(p + 1) * LANES]

        def k_chunk(c):
            k = proj(ATT_WIDTH + c * cw).astype(BF16)
            for p in range(per_chunk):
                k_ref[c * per_chunk + p] = k[:, p * LANES:(p + 1) * LANES]

        def v_chunk(c):
            heads = cw // ATT_HEAD_DIM
            v_t = proj(2 * ATT_WIDTH + c * cw).T.astype(BF16)
            v_ref[c * heads:(c + 1) * heads] = v_t.reshape(heads, ATT_HEAD_DIM, rows)

        def gate_chunk(ref, col0, c):
            ref[:, c * cw:(c + 1) * cw] = proj(col0 + c * cw).astype(BF16)

        def gy_chunk(c):
            gy_s[:, c * cw:(c + 1) * cw] = _gelu_tanh(proj(lru0 + LRU_WIDTH + c * cw))

        def xr_chunk(c):
            xr = proj(lru0 + c * cw)
            for src_row in range(0, rows, LRU_STEPS):
                tile0, j = src_row // LRU_TILE * LRU_TILE, src_row % LRU_TILE // LRU_STEPS
                for p in range(per_chunk):
                    xr_s[c * per_chunk + p, pl.ds(tile0 + j, LRU_STEPS, stride=SUBLANES), :] = (
                        xr[src_row:src_row + LRU_STEPS, p * LANES:(p + 1) * LANES])

        chunks = ([functools.partial(q_chunk, c) for c in range(n_att)]
                  + [functools.partial(k_chunk, c) for c in range(n_att)]
                  + [functools.partial(v_chunk, c) for c in range(n_att)]
                  + [functools.partial(gate_chunk, sa_ref, lru0 + 2 * LRU_WIDTH, c) for c in range(n_lru)]
                  + [functools.partial(gate_chunk, sr_ref, lru0 + 3 * LRU_WIDTH, c) for c in range(n_lru)]
                  + [functools.partial(gy_chunk, c) for c in range(n_lru)]
                  + [functools.partial(xr_chunk, c) for c in range(n_lru)])
        units = scan_units()
        for n, chunk in enumerate(chunks):
            chunk()
            if n < len(units):
                units[n]()

    @pl.when(i == n_tiles)
    def _():
        for unit in scan_units():
            unit()


def _proj(x2, mod, norm_pre, w_in, conv_w, conv_b, w2, ba, bx, lam, *, seq):
    t, d = x2.shape
    tiles_per_seq = seq // TOKEN_TILE
    n_tiles = t // TOKEN_TILE
    cur = lambda i: jnp.minimum(i, n_tiles - 1)
    row = lambda n: pl.BlockSpec((TOKEN_TILE, n), lambda i: (cur(i), 0))
    tok = lambda n, dt: jax.ShapeDtypeStruct((t, n), dt)
    bsz, pairs, groups = t // seq, ATT_WIDTH // LANES, LRU_WIDTH // LANES
    qk_spec = pl.BlockSpec((None, pairs, TOKEN_TILE, LANES),
                           lambda i: (cur(i) // tiles_per_seq, 0, cur(i) % tiles_per_seq, 0))
    qk_shape = jax.ShapeDtypeStruct((bsz, pairs, seq, LANES), BF16)
    v_spec = pl.BlockSpec((None, ATT_HEADS, ATT_HEAD_DIM, TOKEN_TILE),
                          lambda i: (cur(i) // tiles_per_seq, 0, 0, cur(i) % tiles_per_seq))
    v_shape = jax.ShapeDtypeStruct((bsz, ATT_HEADS, ATT_HEAD_DIM, seq), BF16)
    rec_spec = pl.BlockSpec((TOKEN_TILE, LRU_WIDTH), lambda i: (jnp.maximum(i - 1, 0), 0))
    return pl.pallas_call(
        functools.partial(_proj_kernel, tiles_per_seq=tiles_per_seq, n_tiles=n_tiles),
        grid=(n_tiles + 1,),
        in_specs=[row(d),
                  pl.BlockSpec((None, 3 * N_SUB, d), lambda i: (cur(i) // tiles_per_seq, 0, 0)),
                  _resident(norm_pre.shape), _resident(w_in.shape),
                  _resident(conv_w.shape), _resident(conv_b.shape), _resident(w2.shape),
                  _resident(ba.shape), _resident(bx.shape), _resident(lam.shape)],
        out_specs=[qk_spec, qk_spec, v_spec, rec_spec, row(D_MODEL), row(D_MODEL)],
        out_shape=[qk_shape, qk_shape, v_shape, tok(LRU_WIDTH, BF16),
                   tok(D_MODEL, BF16), tok(D_MODEL, BF16)],
        scratch_shapes=[pltpu.VMEM((groups, TOKEN_TILE, LANES), F32),
                        pltpu.VMEM((TOKEN_TILE, LRU_WIDTH), F32),
                        pltpu.VMEM((groups, CONV_WIDTH - 1, SUBLANES, LANES), F32),
                        pltpu.VMEM((groups, SUBLANES, LANES), F32),
                        pltpu.VMEM((groups, LRU_TILE, LANES), F32)],
        compiler_params=_params("arbitrary"),
        name="mixer_proj",
    )(x2, mod, norm_pre, w_in, conv_w, conv_b, w2, ba, bx, lam)


REL_PAD = 384
TOEPLITZ = 1024


def _bias_kernel(tab_ref, o_ref):
    tab = tab_ref[...]
    hi = tab.astype(BF16)
    r1 = tab - hi.astype(F32)
    mid = r1.astype(BF16)
    lo = (r1 - mid.astype(F32)).astype(BF16)
    d_idx = lax.broadcasted_iota(jnp.int32, (REL_PAD, TOEPLITZ), 0)
    m_idx = lax.broadcasted_iota(jnp.int32, (REL_PAD, TOEPLITZ), 1)
    rel = jnp.clip(m_idx - (Q_TILE - 1), -MAX_REL, MAX_REL) + MAX_REL
    onehot = jnp.where(d_idx == rel, 1.0, 0.0).astype(BF16)
    profile = ((_dot(lo, onehot) + _dot(mid, onehot)) + _dot(hi, onehot)) * LOG2_E

    kk = lax.broadcasted_iota(jnp.int32, (BAND, Q_TILE), 0)
    r = lax.broadcasted_iota(jnp.int32, (BAND, Q_TILE), 1)
    qc = r // CHUNK
    kc = kk // CHUNK
    band = jnp.where((kc >= qc) & (kc <= qc + LEFT_CHUNKS), 0.0, MASK_VALUE)
    for h in range(ATT_HEADS):
        rows = jnp.broadcast_to(profile[h:h + 1, :], (BAND, TOEPLITZ))
        skew = pltpu.roll(rows, TOEPLITZ - (BAND - 1), 1, stride=1, stride_axis=0)
        o_ref[h] = skew[:, :Q_TILE] + band


def _bias_table(rel_bias):
    tab = jnp.pad(rel_bias, ((0, 0), (0, REL_PAD - rel_bias.shape[1])))
    return pl.pallas_call(
        _bias_kernel,
        out_shape=jax.ShapeDtypeStruct((ATT_HEADS, BAND, Q_TILE), F32),
        compiler_params=pltpu.CompilerParams(vmem_limit_bytes=VMEM_LIMIT_BYTES),
        name="rel_bias",
    )(tab)


def _attn_kernel(q_ref, *refs):
    n_kv = K_BLOCKS - 1 + Q_STEP_TILES
    k_refs, v_refs = refs[:n_kv], refs[n_kv:2 * n_kv]
    bias_ref, o_ref, acc_ref = refs[2 * n_kv:2 * n_kv + 3]
    s_refs, p_refs = refs[-4:-2], refs[-2:]
    first_tile = pl.program_id(1) * Q_STEP_TILES
    lane_head = lax.broadcasted_iota(jnp.int32, (1, LANES), 1) // ATT_HEAD_DIM
    zero = jnp.zeros((), BF16)
    live = [(2 * c * CHUNK, (2 * c + 2 + LEFT_CHUNKS) * CHUNK) for c in range(Q_TILE // LANES)]

    @pl.when((pl.program_id(0) == 0) & (pl.program_id(1) == 0))
    def _():
        for p_ref in p_refs:
            p_ref[...] = jnp.zeros_like(p_ref)

    def start_pens(u):
        return [jnp.where(first_tile + u + j < K_BLOCKS - 1, MASK_VALUE, 0.0)
                for j in range(K_BLOCKS - 1)] + [0.0]

    def scores(u, h, slot):
        pair = h // 2
        qh = jnp.where(lane_head == h % 2, q_ref[pair, u * Q_TILE:(u + 1) * Q_TILE, :], zero)
        kcat = jnp.concatenate([r[pair] for r in k_refs[u:u + K_BLOCKS]], axis=0)
        s = lax.dot_general(kcat, qh, (((1,), (1,)), ((), ())), preferred_element_type=F32)
        s_refs[slot][...] = s + bias_ref[h]

    def col_reduce(parts, op):
        while len(parts) > 1:
            parts = [op(parts[i], parts[i + 1]) if i + 1 < len(parts) else parts[i]
                     for i in range(0, len(parts), 2)]
        return parts[0]

    def softmax(u, slot):
        s_ref, p_ref = s_refs[slot], p_refs[slot]
        pens = start_pens(u)
        groups = CHUNK // SUBLANES
        inv = []
        for c, (r0, r1) in enumerate(live):
            cl = slice(c * LANES, (c + 1) * LANES)
            blocks = range(r0, r1, CHUNK)
            part = [s_ref[r:r + CHUNK, cl].reshape(groups, SUBLANES, LANES).max(axis=0)
                    + pens[r // Q_TILE] for r in blocks]
            m = jnp.max(col_reduce(part, jnp.maximum), axis=0, keepdims=True)
            shifted = [m - pen for pen in pens]
            sums = []
            for r in blocks:
                pr = jnp.exp2(s_ref[r:r + CHUNK, cl] - shifted[r // Q_TILE])
                sums.append(pr.reshape(groups, SUBLANES, LANES).sum(axis=0))
                p_ref[r:r + CHUNK, cl] = pr.astype(BF16)
            l = jnp.sum(col_reduce(sums, jnp.add), axis=0, keepdims=True)
            inv.append(1.0 / l)
        return jnp.concatenate(inv, axis=1)

    def weighted_values(u, h, slot, inv_l):
        v_t = jnp.concatenate([r[h] for r in v_refs[u:u + K_BLOCKS]], axis=1)
        acc_ref[u, h] = _dot(v_t, p_refs[slot][...]) * inv_l

    items = [(u, h) for u in range(Q_STEP_TILES) for h in range(ATT_HEADS)]
    scores(*items[0], 0)
    inv_prev = None
    for n, (u, h) in enumerate(items):
        if n + 1 < len(items):
            scores(*items[n + 1], (n + 1) % 2)
        inv_l = softmax(u, n % 2)
        if n > 0:
            weighted_values(*items[n - 1], (n - 1) % 2, inv_prev)
        inv_prev = inv_l
    weighted_values(*items[-1], (len(items) - 1) % 2, inv_prev)
    for u in range(Q_STEP_TILES):
        for p in range(ATT_HEADS // 2):
            pair_t = acc_ref[u, 2 * p:2 * p + 2].reshape(LANES, Q_TILE)
            o_ref[u * Q_TILE:(u + 1) * Q_TILE, p * LANES:(p + 1) * LANES] = pair_t.T.astype(BF16)


def _attention(q, k, v_t, bias):
    bsz, pairs, seq, _ = q.shape
    step_rows = Q_STEP_TILES * Q_TILE
    rel = range(-(K_BLOCKS - 1), Q_STEP_TILES)
    blk = lambda t, n: jnp.maximum(t * Q_STEP_TILES + n, 0)
    key = lambda n: pl.BlockSpec((None, pairs, Q_TILE, LANES), lambda b, t: (b, 0, blk(t, n), 0))
    val = lambda n: pl.BlockSpec((None, ATT_HEADS, ATT_HEAD_DIM, Q_TILE),
                                 lambda b, t: (b, 0, 0, blk(t, n)))
    slot = lambda dt: pltpu.VMEM((BAND, Q_TILE), dt)
    return pl.pallas_call(
        _attn_kernel,
        grid=(bsz, seq // step_rows),
        in_specs=[pl.BlockSpec((None, pairs, step_rows, LANES), lambda b, t: (b, 0, t, 0))]
                 + [key(n) for n in rel] + [val(n) for n in rel] + [_resident(bias.shape)],
        out_specs=pl.BlockSpec((None, step_rows, ATT_WIDTH), lambda b, t: (b, t, 0)),
        out_shape=jax.ShapeDtypeStruct((bsz, seq, ATT_WIDTH), BF16),
        scratch_shapes=[pltpu.VMEM((Q_STEP_TILES, ATT_HEADS, ATT_HEAD_DIM, Q_TILE), F32),
                        slot(F32), slot(F32), slot(BF16), slot(BF16)],
        compiler_params=_params("arbitrary", "arbitrary"),
        name="chunk_attn",
    )(q, *([k] * len(rel)), *([v_t] * len(rel)), bias)


def _pair_block_diag(wa, wx):
    def pairs(w):
        z = jnp.zeros_like(w[0::2])
        top = jnp.concatenate([w[0::2], z], axis=2)
        bot = jnp.concatenate([z, w[1::2]], axis=2)
        return jnp.concatenate([top, bot], axis=1)
    return jnp.concatenate([pairs(wa), pairs(wx)], axis=2)


def _mixout_kernel(x_ref, mod_ref, gpost_ref, att_ref, rec_ref, sa_ref, sr_ref,
                   wao_ref, wro_ref, wout_ref, *rest):
    n_cast = (len(rest) - 2) // 2
    o_ref, y_ref = rest[n_cast], rest[-1]
    _cast_slabs(rest[:n_cast], rest[n_cast + 1:-1])
    _, _, gate = _mod_rows(mod_ref, 1)
    n_sub = x_ref.shape[0] // MIX_SUB_TILE

    def rows(s):
        return slice(s * MIX_SUB_TILE, (s + 1) * MIX_SUB_TILE)

    def project(s):
        att = _dot(att_ref[rows(s), :], wao_ref[...])
        rec = _dot(rec_ref[rows(s), :], wro_ref[...])
        gate_att = _sigmoid(sa_ref[rows(s), :].astype(F32))
        gate_rec = _sigmoid(sr_ref[rows(s), :].astype(F32))
        merged = gate_att * att + gate_rec * rec
        y_ref[s % 2] = _dot(merged.astype(BF16), wout_ref[...])

    def epilogue(s):
        o_ref[rows(s), :] = x_ref[rows(s), :] + gate * _rms(y_ref[s % 2], gpost_ref[1:2, :])

    for s in range(n_sub):
        project(s)
        if s > 0:
            epilogue(s - 1)
    epilogue(n_sub - 1)


def _mixout(x2, mod, norm_post, att, rec, sa, sr, w_att_o, w_rec_o, w_out, *, seq, layer,
            casts=()):
    t, d = x2.shape
    tiles_per_seq = seq // MIX_TOKEN_TILE
    n_steps = t // MIX_TOKEN_TILE
    row = lambda n: pl.BlockSpec((MIX_TOKEN_TILE, n), lambda i: (i, 0))
    cast_in, cast_specs, cast_shapes = _cast_streams(casts, layer, n_steps)
    out, *cast_out = pl.pallas_call(
        _mixout_kernel,
        grid=(n_steps,),
        in_specs=[row(d),
                  pl.BlockSpec((None, 3 * N_SUB, d), lambda i: (i // tiles_per_seq, 0, 0)),
                  _resident(norm_post.shape),
                  row(ATT_WIDTH), row(LRU_WIDTH), row(d), row(d),
                  _resident(w_att_o.shape), _resident(w_rec_o.shape), _resident(w_out.shape)]
                 + cast_in,
        out_specs=[row(d)] + cast_specs,
        out_shape=[jax.ShapeDtypeStruct((t, d), F32)] + cast_shapes,
        scratch_shapes=[pltpu.VMEM((2, MIX_SUB_TILE, d), F32)],
        compiler_params=_params("arbitrary"),
        name="mixer_out",
    )(x2, mod, norm_post, att, rec, sa, sr, w_att_o, w_rec_o, w_out, *casts)
    return out, cast_out


def _layer(x2, c, l, bsz, seq, w_ada, b_ada, norm_pre, norm_post, ffn1_w_gu, ffn1_w_down, w_in,
           rel_bias, conv_w, conv_b, lru_wa, lru_ba, lru_wx, lru_bx, lru_lambda, w_att_o,
           w_rec_o, w_out, ffn2_w_gu, ffn2_w_down):
    mod, (w_gu1_b, w_down1_b) = _ada(c, w_ada, b_ada, layer=l, casts=(ffn1_w_gu, ffn1_w_down))
    mod = mod.reshape(bsz, 3 * N_SUB, D_MODEL)
    npre, npost = norm_pre[l], norm_post[l]

    x2, (w_in_b, w_att_o_b, w_rec_o_b, w_out_b, w_down2_b) = _ffn(
        x2, mod, npre, npost, w_gu1_b, w_down1_b, sub=0, res_w=0.5, seq=seq, layer=l,
        casts=(w_in, w_att_o, w_rec_o, w_out, ffn2_w_down))

    w2 = (0.5 * _pair_block_diag(lru_wa[l], lru_wx[l])).astype(BF16)
    vec = lambda p: p[l].reshape(1, LRU_WIDTH)
    q, k, v, rec, sa, sr = _proj(x2, mod, npre, w_in_b, conv_w[l], vec(conv_b), w2,
                                 0.5 * vec(lru_ba), 0.5 * vec(lru_bx), vec(lru_lambda), seq=seq)
    att = _attention(q, k, v, _bias_table(rel_bias[l]))
    x2, (w_gu2_b,) = _mixout(
        x2, mod, npost, att.reshape(bsz * seq, ATT_WIDTH), rec, sa, sr,
        w_att_o_b, w_rec_o_b, w_out_b, seq=seq, layer=l, casts=(ffn2_w_gu,))

    return _ffn(x2, mod, npre, npost, w_gu2_b, w_down2_b, sub=2, res_w=0.5, seq=seq, layer=l)[0]


def kernel(x, c, w_ada, b_ada, norm_pre, norm_post, ffn1_w_gu, ffn1_w_down, w_in, rel_bias, conv_w, conv_b, lru_wa, lru_ba, lru_wx, lru_bx, lru_lambda, w_att_o, w_rec_o, w_out, ffn2_w_gu, ffn2_w_down):
    bsz, seq, d = x.shape
    tiles = (TOKEN_TILE, MIX_TOKEN_TILE, FFN_TOKEN_TILE, Q_TILE * Q_STEP_TILES)
    assert d == D_MODEL and all(seq % n == 0 for n in tiles)
    assert TOKEN_TILE % LRU_TILE == 0
    x2 = x.reshape(bsz * seq, d)
    for l in range(w_ada.shape[0]):
        x2 = _layer(x2, c, l, bsz, seq, w_ada, b_ada, norm_pre, norm_post, ffn1_w_gu,
                    ffn1_w_down, w_in, rel_bias, conv_w, conv_b, lru_wa, lru_ba, lru_wx, lru_bx,
                    lru_lambda, w_att_o, w_rec_o, w_out, ffn2_w_gu, ffn2_w_down)
    return x2.reshape(bsz, seq, d)
```

```python
import functools
import math

import jax
import jax.numpy as jnp
from jax import lax
from jax.experimental import pallas as pl
from jax.experimental.pallas import tpu as pltpu

D_MODEL = 1024
CHUNK = 64
LEFT_CHUNKS = 8
ATT_HEADS = 8
ATT_HEAD_DIM = 64
ATT_WIDTH = ATT_HEADS * ATT_HEAD_DIM
MAX_REL = 128
LRU_WIDTH = D_MODEL
LRU_BLOCKS = 16
LRU_BLOCK = LRU_WIDTH // LRU_BLOCKS
CONV_WIDTH = 4
LRU_C = 8.0
D_FF = 2816
N_SUB = 3
EPS = 1e-6

LANES = 128
SUBLANES = 8
BF16_ROWS = 16
VMEM_LIMIT_BYTES = 56 * 1024 * 1024

TOKEN_TILE = 512
MIX_TOKEN_TILE = 1024
MIX_SUB_TILE = 512
FFN_TOKEN_TILE = 1024
FFN_SUB_TILE = 512
FF_TILE = 256
PROJ_CHUNK = 256
ADA_TILE = 1152
Q_TILE = 4 * CHUNK
Q_STEP_TILES = 4
K_BLOCKS = LEFT_CHUNKS * CHUNK // Q_TILE + 1
BAND = K_BLOCKS * Q_TILE
LRU_TILE = 256
LRU_STEPS = LRU_TILE // SUBLANES
MASK_VALUE = -1e30
SMALLEST_NORMAL = float(jnp.finfo(jnp.float32).tiny)
LOG2_E = math.log2(math.e)
BF16 = jnp.bfloat16
F32 = jnp.float32


def _dot(a, b):
    return jnp.dot(a, b, preferred_element_type=F32)


def _rms(x, g):
    return x * lax.rsqrt(jnp.mean(x * x, axis=-1, keepdims=True) + EPS) * g


def _modulated_norm(x, g, shift, scale):
    inv = lax.rsqrt(jnp.mean(x * x, axis=-1, keepdims=True) + EPS)
    return x * inv * (g * (1.0 + scale)) + shift


def _sigmoid(x):
    return 1.0 / (1.0 + jnp.exp(-x))


def _gelu_tanh(x):
    c = -2.0 * LOG2_E * math.sqrt(2.0 / math.pi)
    return x / (1.0 + jnp.exp2(x * (c + (0.044715 * c) * (x * x))))


def _mod_rows(mod_ref, sub):
    shift = mod_ref[3 * sub:3 * sub + 1, :]
    scale = mod_ref[3 * sub + 1:3 * sub + 2, :]
    gate = mod_ref[3 * sub + 2:3 * sub + 3, :]
    return shift, scale, gate


def _resident(shape):
    nd = len(shape)
    return pl.BlockSpec(shape, lambda *_: (0,) * nd, pipeline_mode=pl.Buffered(1))


def _params(*semantics):
    return pltpu.CompilerParams(dimension_semantics=semantics,
                                vmem_limit_bytes=VMEM_LIMIT_BYTES)


def _cast_streams(stacked, layer, n_steps):
    for w in stacked:
        assert w.shape[1] % (BF16_ROWS * n_steps) == 0 and w.shape[2] % LANES == 0, w.shape
    in_specs = [pl.BlockSpec((None, w.shape[1] // n_steps, w.shape[2]), lambda i: (layer, i, 0))
                for w in stacked]
    out_specs = [pl.BlockSpec((w.shape[1] // n_steps, w.shape[2]), lambda i: (i, 0))
                 for w in stacked]
    shapes = [jax.ShapeDtypeStruct(w.shape[1:], BF16) for w in stacked]
    return in_specs, out_specs, shapes


def _cast_slabs(in_refs, out_refs):
    for src, dst in zip(in_refs, out_refs):
        dst[...] = src[...].astype(BF16)


def _ada_kernel(c_ref, w_ref, b_ref, *rest):
    n_cast = (len(rest) - 1) // 2
    o_ref = rest[n_cast]
    _cast_slabs(rest[:n_cast], rest[n_cast + 1:])
    c = c_ref[...]
    c_act = (c * _sigmoid(c)).astype(BF16)
    o_ref[...] = _dot(c_act, w_ref[...].astype(BF16)) + b_ref[...]


def _ada(c, w_ada, b_ada, *, layer, casts=()):
    bsz, d = c.shape
    n = w_ada.shape[2]
    n_steps = n // ADA_TILE
    cast_in, cast_specs, cast_shapes = _cast_streams(casts, layer, n_steps)
    out, *cast_out = pl.pallas_call(
        _ada_kernel,
        grid=(n_steps,),
        in_specs=[pl.BlockSpec((bsz, d), lambda j: (0, 0)),
                  pl.BlockSpec((None, d, ADA_TILE), lambda j: (layer, 0, j)),
                  pl.BlockSpec((1, ADA_TILE), lambda j: (0, j))] + cast_in,
        out_specs=[pl.BlockSpec((bsz, ADA_TILE), lambda j: (0, j))] + cast_specs,
        out_shape=[jax.ShapeDtypeStruct((bsz, n), F32)] + cast_shapes,
        compiler_params=_params("arbitrary"),
        name="adaln",
    )(c, w_ada, b_ada[layer].reshape(1, n), *casts)
    return out, cast_out


def _ffn_kernel(x_ref, mod_ref, gpre_ref, gpost_ref, wgu_ref, wdn_ref, *rest, sub, res_w):
    n_cast = (len(rest) - 4) // 2
    o_ref = rest[n_cast]
    h_ref, act_ref, y_ref = rest[-3:]
    shift, scale, gate = _mod_rows(mod_ref, sub)
    n_sub = x_ref.shape[0] // FFN_SUB_TILE

    def rows(s):
        return slice(s * FFN_SUB_TILE, (s + 1) * FFN_SUB_TILE)

    def prologue(s):
        x = x_ref[rows(s), :]
        h_ref[s % 2] = _modulated_norm(x, gpre_ref[sub:sub + 1, :], shift, scale).astype(BF16)

    def gate_up(s, j):
        lo = j * FF_TILE
        h = h_ref[s % 2]
        g = _dot(h, wgu_ref[:, lo:lo + FF_TILE])
        u = _dot(h, wgu_ref[:, D_FF + lo:D_FF + lo + FF_TILE])
        act_ref[s % 2, :, lo:lo + FF_TILE] = (g * _sigmoid(g) * u).astype(BF16)

    def down(s):
        y_ref[s % 2] = _dot(act_ref[s % 2], wdn_ref[...])

    def epilogue(s):
        y = _rms(y_ref[s % 2], gpost_ref[sub:sub + 1, :])
        o_ref[rows(s), :] = x_ref[rows(s), :] + (res_w * gate) * y

    n_ff = D_FF // FF_TILE
    prologue(0)
    for s in range(n_sub):
        for j in range(n_ff):
            gate_up(s, j)
            if j == 0 and s == 0:
                _cast_slabs(rest[:n_cast], rest[n_cast + 1:-3])
            if j == 0 and s > 0:
                epilogue(s - 1)
            if j == n_ff // 2 and s + 1 < n_sub:
                prologue(s + 1)
        down(s)
    epilogue(n_sub - 1)


def _ffn(x2, mod, norm_pre, norm_post, w_gu, w_down, *, sub, res_w, seq, layer, casts=()):
    t, d = x2.shape
    tiles_per_seq = seq // FFN_TOKEN_TILE
    n_steps = t // FFN_TOKEN_TILE
    cast_in, cast_specs, cast_shapes = _cast_streams(casts, layer, n_steps)
    out, *cast_out = pl.pallas_call(
        functools.partial(_ffn_kernel, sub=sub, res_w=res_w),
        grid=(n_steps,),
        in_specs=[pl.BlockSpec((FFN_TOKEN_TILE, d), lambda i: (i, 0)),
                  pl.BlockSpec((None, 3 * N_SUB, d), lambda i: (i // tiles_per_seq, 0, 0)),
                  _resident(norm_pre.shape),
                  _resident(norm_post.shape),
                  _resident(w_gu.shape),
                  _resident(w_down.shape)] + cast_in,
        out_specs=[pl.BlockSpec((FFN_TOKEN_TILE, d), lambda i: (i, 0))] + cast_specs,
        out_shape=[jax.ShapeDtypeStruct((t, d), F32)] + cast_shapes,
        scratch_shapes=[pltpu.VMEM((2, FFN_SUB_TILE, d), BF16),
                        pltpu.VMEM((2, FFN_SUB_TILE, D_FF), BF16),
                        pltpu.VMEM((2, FFN_SUB_TILE, d), F32)],
        compiler_params=_params("arbitrary"),
        name=f"ffn{sub}",
    )(x2, mod, norm_pre, norm_post, w_gu, w_down, *casts)
    return out, cast_out


def _lru_unit(g, xr_ref, row0, gy_ref, o_ref, fresh, cw_ref, cb_ref, w2_ref, ba_ref, bx_ref, decay,
              tail_ref, h_ref, hs_ref):
    sub = lax.broadcasted_iota(jnp.int32, (SUBLANES, LANES), 0)
    taps = CONV_WIDTH - 1
    sl = slice(g * LANES, (g + 1) * LANES)
    x3 = xr_ref[g, row0:row0 + LRU_TILE, :].reshape(LRU_STEPS, SUBLANES, LANES)
    cur_tail = x3[LRU_STEPS - taps:]
    lead = pltpu.roll(jnp.where(sub == SUBLANES - 1, tail_ref[g] * fresh, cur_tail), 1, 1)
    tail_ref[g] = cur_tail
    xext = jnp.concatenate([lead, x3], axis=0)
    xc3 = cb_ref[:, sl] + cw_ref[0:1, sl] * xext[0:LRU_STEPS]
    for w in range(1, CONV_WIDTH):
        xc3 = xc3 + cw_ref[w:w + 1, sl] * xext[w:w + LRU_STEPS]
    xc = xc3.reshape(LRU_TILE, LANES)
    z = _dot(xc.astype(BF16), w2_ref[g])
    ta = jnp.tanh(z[:, :LANES] + ba_ref[:, sl])
    tx = jnp.tanh(z[:, LANES:] + bx_ref[:, sl])
    log_a = decay[:, sl] * (ta + 1.0)
    a = jnp.exp(log_a)
    th = jnp.tanh(log_a)
    q = (-0.5 * th) / (1.0 - th)
    half_mult = q * lax.rsqrt(jnp.maximum(q, SMALLEST_NORMAL))
    u = half_mult * ((tx + 1.0) * xc)
    a3 = a.reshape(LRU_STEPS, SUBLANES, LANES)
    u3 = u.reshape(LRU_STEPS, SUBLANES, LANES)
    loc = [u3[0]]
    prod = [a3[0]]
    for i in range(1, LRU_STEPS):
        loc.append(a3[i] * loc[-1] + u3[i])
        prod.append(a3[i] * prod[-1])
    blk_a, blk_u = prod[-1], loc[-1]
    for s in (1, 2, 4):
        keep = sub >= s
        a_prev = jnp.where(keep, pltpu.roll(blk_a, s, 0), 1.0)
        u_prev = jnp.where(keep, pltpu.roll(blk_u, s, 0), 0.0)
        blk_u = blk_u + blk_a * u_prev
        blk_a = blk_a * a_prev
    h_in = h_ref[g] * fresh
    end = blk_u + blk_a * h_in
    entry = jnp.where(sub == 0, h_in, pltpu.roll(end, 1, 0))
    h_ref[g] = jnp.broadcast_to(end[SUBLANES - 1:SUBLANES, :], (SUBLANES, LANES))
    for i in range(LRU_STEPS):
        hs_ref[g, i * SUBLANES:(i + 1) * SUBLANES, :] = loc[i] + prod[i] * entry
    rows = []
    for n in range(LRU_TILE // SUBLANES):
        j, i0 = divmod(n * SUBLANES, LRU_STEPS)
        rows.append(hs_ref[g, pl.ds(i0 * SUBLANES + j, SUBLANES, stride=SUBLANES), :])
    hs = jnp.concatenate(rows, axis=0)
    o_ref[row0:row0 + LRU_TILE, sl] = (hs * gy_ref[row0:row0 + LRU_TILE, sl]).astype(BF16)


def _proj_kernel(x_ref, mod_ref, gpre_ref, w_ref, cw_ref, cb_ref, w2_ref, ba_ref, bx_ref, lam_ref,
                 q_ref, k_ref, v_ref, rec_ref, sa_ref, sr_ref,
                 xr_s, gy_s, tail_ref, h_ref, hs_ref, *, tiles_per_seq, n_tiles):
    i = pl.program_id(0)
    rows = x_ref.shape[0]
    cw = PROJ_CHUNK
    per_chunk = cw // LANES
    lru0 = 3 * ATT_WIDTH
    n_att, n_lru = ATT_WIDTH // cw, LRU_WIDTH // cw

    @pl.when(i == 0)
    def _():
        for ref in (xr_s, gy_s, tail_ref, h_ref):
            ref[...] = jnp.zeros_like(ref)

    def scan_units():
        neg_lam = -lam_ref[...]
        softplus = jnp.maximum(neg_lam, 0.0) + jnp.log1p(jnp.exp(-jnp.abs(neg_lam)))
        decay = (-0.5 * LRU_C) * softplus
        first = jnp.where(lax.rem(i - 1, tiles_per_seq) == 0, 0.0, 1.0)

        def unit(row0, g):
            _lru_unit(g, xr_s, row0, gy_s, rec_ref, first if row0 == 0 else 1.0, cw_ref, cb_ref,
                      w2_ref, ba_ref, bx_ref, decay, tail_ref, h_ref, hs_ref)

        return [functools.partial(unit, row0, g)
                for row0 in range(0, rows, LRU_TILE) for g in range(LRU_WIDTH // LANES)]

    @pl.when(i < n_tiles)
    def _():
        x = x_ref[...]
        shift, scale, _ = _mod_rows(mod_ref, 1)
        h = _modulated_norm(x, gpre_ref[1:2, :], shift, scale).astype(BF16)

        def proj(col):
            return _dot(h, w_ref[:, col:col + cw])

        def q_chunk(c):
            q = (proj(c * cw) * (ATT_HEAD_DIM ** -0.5 * LOG2_E)).astype(BF16)
            for p in range(per_chunk):
                q_ref[c * per_chunk + p] = q[:, p * LANES:(p + 1) * LANES]

        def k_chunk(c):
            k = proj(ATT_WIDTH + c * cw).astype(BF16)
            for p in range(per_chunk):
                k_ref[c * per_chunk + p] = k[:, p * LANES:(p + 1) * LANES]

        def v_chunk(c):
            heads = cw // ATT_HEAD_DIM
            v_t = proj(2 * ATT_WIDTH + c * cw).T.astype(BF16)
            v_ref[c * heads:(c + 1) * heads] = v_t.reshape(heads, ATT_HEAD_DIM, rows)

        def gate_chunk(ref, col0, c):
            ref[:, c * cw:(c + 1) * cw] = proj(col0 + c * cw).astype(BF16)

        def gy_chunk(c):
            gy_s[:, c * cw:(c + 1) * cw] = _gelu_tanh(proj(lru0 + LRU_WIDTH + c * cw))

        def xr_chunk(c):
            xr = proj(lru0 + c * cw)
            for src_row in range(0, rows, LRU_STEPS):
                tile0, j = src_row // LRU_TILE * LRU_TILE, src_row % LRU_TILE // LRU_STEPS
                for p in range(per_chunk):
                    xr_s[c * per_chunk + p, pl.ds(tile0 + j, LRU_STEPS, stride=SUBLANES), :] = (
                        xr[src_row:src_row + LRU_STEPS, p * LANES:(p + 1) * LANES])

        chunks = ([functools.partial(q_chunk, c) for c in range(n_att)]
                  + [functools.partial(k_chunk, c) for c in range(n_att)]
                  + [functools.partial(v_chunk, c) for c in range(n_att)]
                  + [functools.partial(gate_chunk, sa_ref, lru0 + 2 * LRU_WIDTH, c) for c in range(n_lru)]
                  + [functools.partial(gate_chunk, sr_ref, lru0 + 3 * LRU_WIDTH, c) for c in range(n_lru)]
                  + [functools.partial(gy_chunk, c) for c in range(n_lru)]
                  + [functools.partial(xr_chunk, c) for c in range(n_lru)])
        units = scan_units()
        for n, chunk in enumerate(chunks):
            chunk()
            if n < len(units):
                units[n]()

    @pl.when(i == n_tiles)
    def _():
        for unit in scan_units():
            unit()


def _proj(x2, mod, norm_pre, w_in, conv_w, conv_b, w2, ba, bx, lam, *, seq):
    t, d = x2.shape
    tiles_per_seq = seq // TOKEN_TILE
    n_tiles = t // TOKEN_TILE
    cur = lambda i: jnp.minimum(i, n_tiles - 1)
    row = lambda n: pl.BlockSpec((TOKEN_TILE, n), lambda i: (cur(i), 0))
    tok = lambda n, dt: jax.ShapeDtypeStruct((t, n), dt)
    bsz, pairs, groups = t // seq, ATT_WIDTH // LANES, LRU_WIDTH // LANES
    qk_spec = pl.BlockSpec((None, pairs, TOKEN_TILE, LANES),
                           lambda i: (cur(i) // tiles_per_seq, 0, cur(i) % tiles_per_seq, 0))
    qk_shape = jax.ShapeDtypeStruct((bsz, pairs, seq, LANES), BF16)
    v_spec = pl.BlockSpec((None, ATT_HEADS, ATT_HEAD_DIM, TOKEN_TILE),
                          lambda i: (cur(i) // tiles_per_seq, 0, 0, cur(i) % tiles_per_seq))
    v_shape = jax.ShapeDtypeStruct((bsz, ATT_HEADS, ATT_HEAD_DIM, seq), BF16)
    rec_spec = pl.BlockSpec((TOKEN_TILE, LRU_WIDTH), lambda i: (jnp.maximum(i - 1, 0), 0))
    return pl.pallas_call(
        functools.partial(_proj_kernel, tiles_per_seq=tiles_per_seq, n_tiles=n_tiles),
        grid=(n_tiles + 1,),
        in_specs=[row(d),
                  pl.BlockSpec((None, 3 * N_SUB, d), lambda i: (cur(i) // tiles_per_seq, 0, 0)),
                  _resident(norm_pre.shape), _resident(w_in.shape),
                  _resident(conv_w.shape), _resident(conv_b.shape), _resident(w2.shape),
                  _resident(ba.shape), _resident(bx.shape), _resident(lam.shape)],
        out_specs=[qk_spec, qk_spec, v_spec, rec_spec, row(D_MODEL), row(D_MODEL)],
        out_shape=[qk_shape, qk_shape, v_shape, tok(LRU_WIDTH, BF16),
                   tok(D_MODEL, BF16), tok(D_MODEL, BF16)],
        scratch_shapes=[pltpu.VMEM((groups, TOKEN_TILE, LANES), F32),
                        pltpu.VMEM((TOKEN_TILE, LRU_WIDTH), F32),
                        pltpu.VMEM((groups, CONV_WIDTH - 1, SUBLANES, LANES), F32),
                        pltpu.VMEM((groups, SUBLANES, LANES), F32),
                        pltpu.VMEM((groups, LRU_TILE, LANES), F32)],
        compiler_params=_params("arbitrary"),
        name="mixer_proj",
    )(x2, mod, norm_pre, w_in, conv_w, conv_b, w2, ba, bx, lam)


REL_PAD = 384
TOEPLITZ = 1024


def _bias_kernel(tab_ref, o_ref):
    tab = tab_ref[...]
    hi = tab.astype(BF16)
    r1 = tab - hi.astype(F32)
    mid = r1.astype(BF16)
    lo = (r1 - mid.astype(F32)).astype(BF16)
    d_idx = lax.broadcasted_iota(jnp.int32, (REL_PAD, TOEPLITZ), 0)
    m_idx = lax.broadcasted_iota(jnp.int32, (REL_PAD, TOEPLITZ), 1)
    rel = jnp.clip(m_idx - (Q_TILE - 1), -MAX_REL, MAX_REL) + MAX_REL
    onehot = jnp.where(d_idx == rel, 1.0, 0.0).astype(BF16)
    profile = ((_dot(lo, onehot) + _dot(mid, onehot)) + _dot(hi, onehot)) * LOG2_E

    kk = lax.broadcasted_iota(jnp.int32, (BAND, Q_TILE), 0)
    r = lax.broadcasted_iota(jnp.int32, (BAND, Q_TILE), 1)
    qc = r // CHUNK
    kc = kk // CHUNK
    band = jnp.where((kc >= qc) & (kc <= qc + LEFT_CHUNKS), 0.0, MASK_VALUE)
    for h in range(ATT_HEADS):
        rows = jnp.broadcast_to(profile[h:h + 1, :], (BAND, TOEPLITZ))
        skew = pltpu.roll(rows, TOEPLITZ - (BAND - 1), 1, stride=1, stride_axis=0)
        o_ref[h] = skew[:, :Q_TILE] + band


def _bias_table(rel_bias):
    tab = jnp.pad(rel_bias, ((0, 0), (0, REL_PAD - rel_bias.shape[1])))
    return pl.pallas_call(
        _bias_kernel,
        out_shape=jax.ShapeDtypeStruct((ATT_HEADS, BAND, Q_TILE), F32),
        compiler_params=pltpu.CompilerParams(vmem_limit_bytes=VMEM_LIMIT_BYTES),
        name="rel_bias",
    )(tab)


def _attn_kernel(q_ref, *refs):
    n_kv = K_BLOCKS - 1 + Q_STEP_TILES
    k_refs, v_refs = refs[:n_kv], refs[n_kv:2 * n_kv]
    bias_ref, o_ref, acc_ref = refs[2 * n_kv:2 * n_kv + 3]
    s_refs, p_refs = refs[-4:-2], refs[-2:]
    first_tile = pl.program_id(1) * Q_STEP_TILES
    lane_head = lax.broadcasted_iota(jnp.int32, (1, LANES), 1) // ATT_HEAD_DIM
    zero = jnp.zeros((), BF16)
    live = [(2 * c * CHUNK, (2 * c + 2 + LEFT_CHUNKS) * CHUNK) for c in range(Q_TILE // LANES)]

    @pl.when((pl.program_id(0) == 0) & (pl.program_id(1) == 0))
    def _():
        for p_ref in p_refs:
            p_ref[...] = jnp.zeros_like(p_ref)

    def start_pens(u):
        return [jnp.where(first_tile + u + j < K_BLOCKS - 1, MASK_VALUE, 0.0)
                for j in range(K_BLOCKS - 1)] + [0.0]

    def scores(u, h, slot):
        pair = h // 2
        qh = jnp.where(lane_head == h % 2, q_ref[pair, u * Q_TILE:(u + 1) * Q_TILE, :], zero)
        kcat = jnp.concatenate([r[pair] for r in k_refs[u:u + K_BLOCKS]], axis=0)
        s = lax.dot_general(kcat, qh, (((1,), (1,)), ((), ())), preferred_element_type=F32)
        s_refs[slot][...] = s + bias_ref[h]

    def col_reduce(parts, op):
        while len(parts) > 1:
            parts = [op(parts[i], parts[i + 1]) if i + 1 < len(parts) else parts[i]
                     for i in range(0, len(parts), 2)]
        return parts[0]

    def softmax(u, slot):
        s_ref, p_ref = s_refs[slot], p_refs[slot]
        pens = start_pens(u)
        groups = CHUNK // SUBLANES
        inv = []
        for c, (r0, r1) in enumerate(live):
            cl = slice(c * LANES, (c + 1) * LANES)
            blocks = range(r0, r1, CHUNK)
            part = [s_ref[r:r + CHUNK, cl].reshape(groups, SUBLANES, LANES).max(axis=0)
                    + pens[r // Q_TILE] for r in blocks]
            m = jnp.max(col_reduce(part, jnp.maximum), axis=0, keepdims=True)
            shifted = [m - pen for pen in pens]
            sums = []
            for r in blocks:
                pr = jnp.exp2(s_ref[r:r + CHUNK, cl] - shifted[r // Q_TILE])
                sums.append(pr.reshape(groups, SUBLANES, LANES).sum(axis=0))
                p_ref[r:r + CHUNK, cl] = pr.astype(BF16)
            l = jnp.sum(col_reduce(sums, jnp.add), axis=0, keepdims=True)
            inv.append(1.0 / l)
        return jnp.concatenate(inv, axis=1)

    def weighted_values(u, h, slot, inv_l):
        v_t = jnp.concatenate([r[h] for r in v_refs[u:u + K_BLOCKS]], axis=1)
        acc_ref[u, h] = _dot(v_t, p_refs[slot][...]) * inv_l

    items = [(u, h) for u in range(Q_STEP_TILES) for h in range(ATT_HEADS)]
    scores(*items[0], 0)
    inv_prev = None
    for n, (u, h) in enumerate(items):
        if n + 1 < len(items):
            scores(*items[n + 1], (n + 1) % 2)
        inv_l = softmax(u, n % 2)
        if n > 0:
            weighted_values(*items[n - 1], (n - 1) % 2, inv_prev)
        inv_prev = inv_l
    weighted_values(*items[-1], (len(items) - 1) % 2, inv_prev)
    for u in range(Q_STEP_TILES):
        for p in range(ATT_HEADS // 2):
            pair_t = acc_ref[u, 2 * p:2 * p + 2].reshape(LANES, Q_TILE)
            o_ref[u * Q_TILE:(u + 1) * Q_TILE, p * LANES:(p + 1) * LANES] = pair_t.T.astype(BF16)


def _attention(q, k, v_t, bias):
    bsz, pairs, seq, _ = q.shape
    step_rows = Q_STEP_TILES * Q_TILE
    rel = range(-(K_BLOCKS - 1), Q_STEP_TILES)
    blk = lambda t, n: jnp.maximum(t * Q_STEP_TILES + n, 0)
    key = lambda n: pl.BlockSpec((None, pairs, Q_TILE, LANES), lambda b, t: (b, 0, blk(t, n), 0))
    val = lambda n: pl.BlockSpec((None, ATT_HEADS, ATT_HEAD_DIM, Q_TILE),
                                 lambda b, t: (b, 0, 0, blk(t, n)))
    slot = lambda dt: pltpu.VMEM((BAND, Q_TILE), dt)
    return pl.pallas_call(
        _attn_kernel,
        grid=(bsz, seq // step_rows),
        in_specs=[pl.BlockSpec((None, pairs, step_rows, LANES), lambda b, t: (b, 0, t, 0))]
                 + [key(n) for n in rel] + [val(n) for n in rel] + [_resident(bias.shape)],
        out_specs=pl.BlockSpec((None, step_rows, ATT_WIDTH), lambda b, t: (b, t, 0)),
        out_shape=jax.ShapeDtypeStruct((bsz, seq, ATT_WIDTH), BF16),
        scratch_shapes=[pltpu.VMEM((Q_STEP_TILES, ATT_HEADS, ATT_HEAD_DIM, Q_TILE), F32),
                        slot(F32), slot(F32), slot(BF16), slot(BF16)],
        compiler_params=_params("arbitrary", "arbitrary"),
        name="chunk_attn",
    )(q, *([k] * len(rel)), *([v_t] * len(rel)), bias)


def _pair_block_diag(wa, wx):
    def pairs(w):
        z = jnp.zeros_like(w[0::2])
        top = jnp.concatenate([w[0::2], z], axis=2)
        bot = jnp.concatenate([z, w[1::2]], axis=2)
        return jnp.concatenate([top, bot], axis=1)
    return jnp.concatenate([pairs(wa), pairs(wx)], axis=2)


def _mixout_kernel(x_ref, mod_ref, gpost_ref, att_ref, rec_ref, sa_ref, sr_ref,
                   wao_ref, wro_ref, wout_ref, *rest):
    n_cast = (len(rest) - 2) // 2
    o_ref, y_ref = rest[n_cast], rest[-1]
    _, _, gate = _mod_rows(mod_ref, 1)
    n_sub = x_ref.shape[0] // MIX_SUB_TILE

    def rows(s):
        return slice(s * MIX_SUB_TILE, (s + 1) * MIX_SUB_TILE)

    def project(s):
        att = _dot(att_ref[rows(s), :], wao_ref[...])
        rec = _dot(rec_ref[rows(s), :], wro_ref[...])
        gate_att = _sigmoid(sa_ref[rows(s), :].astype(F32))
        gate_rec = _sigmoid(sr_ref[rows(s), :].astype(F32))
        merged = gate_att * att + gate_rec * rec
        y_ref[s % 2] = _dot(merged.astype(BF16), wout_ref[...])

    def epilogue(s):
        o_ref[rows(s), :] = x_ref[rows(s), :] + gate * _rms(y_ref[s % 2], gpost_ref[1:2, :])

    for s in range(n_sub):
        project(s)
        if s == 0:
            _cast_slabs(rest[:n_cast], rest[n_cast + 1:-1])
        if s > 0:
            epilogue(s - 1)
    epilogue(n_sub - 1)


def _mixout(x2, mod, norm_post, att, rec, sa, sr, w_att_o, w_rec_o, w_out, *, seq, layer,
            casts=()):
    t, d = x2.shape
    tiles_per_seq = seq // MIX_TOKEN_TILE
    n_steps = t // MIX_TOKEN_TILE
    row = lambda n: pl.BlockSpec((MIX_TOKEN_TILE, n), lambda i: (i, 0))
    cast_in, cast_specs, cast_shapes = _cast_streams(casts, layer, n_steps)
    out, *cast_out = pl.pallas_call(
        _mixout_kernel,
        grid=(n_steps,),
        in_specs=[row(d),
                  pl.BlockSpec((None, 3 * N_SUB, d), lambda i: (i // tiles_per_seq, 0, 0)),
                  _resident(norm_post.shape),
                  row(ATT_WIDTH), row(LRU_WIDTH), row(d), row(d),
                  _resident(w_att_o.shape), _resident(w_rec_o.shape), _resident(w_out.shape)]
                 + cast_in,
        out_specs=[row(d)] + cast_specs,
        out_shape=[jax.ShapeDtypeStruct((t, d), F32)] + cast_shapes,
        scratch_shapes=[pltpu.VMEM((2, MIX_SUB_TILE, d), F32)],
        compiler_params=_params("arbitrary"),
        name="mixer_out",
    )(x2, mod, norm_post, att, rec, sa, sr, w_att_o, w_rec_o, w_out, *casts)
    return out, cast_out


def _layer(x2, c, l, bsz, seq, w_ada, b_ada, norm_pre, norm_post, ffn1_w_gu, ffn1_w_down, w_in,
           rel_bias, conv_w, conv_b, lru_wa, lru_ba, lru_wx, lru_bx, lru_lambda, w_att_o,
           w_rec_o, w_out, ffn2_w_gu, ffn2_w_down):
    mod, (w_gu1_b, w_down1_b) = _ada(c, w_ada, b_ada, layer=l, casts=(ffn1_w_gu, ffn1_w_down))
    mod = mod.reshape(bsz, 3 * N_SUB, D_MODEL)
    npre, npost = norm_pre[l], norm_post[l]

    x2, (w_in_b, w_att_o_b, w_rec_o_b, w_out_b, w_down2_b) = _ffn(
        x2, mod, npre, npost, w_gu1_b, w_down1_b, sub=0, res_w=0.5, seq=seq, layer=l,
        casts=(w_in, w_att_o, w_rec_o, w_out, ffn2_w_down))

    w2 = (0.5 * _pair_block_diag(lru_wa[l], lru_wx[l])).astype(BF16)
    vec = lambda p: p[l].reshape(1, LRU_WIDTH)
    q, k, v, rec, sa, sr = _proj(x2, mod, npre, w_in_b, conv_w[l], vec(conv_b), w2,
                                 0.5 * vec(lru_ba), 0.5 * vec(lru_bx), vec(lru_lambda), seq=seq)
    att = _attention(q, k, v, _bias_table(rel_bias[l]))
    x2, (w_gu2_b,) = _mixout(
        x2, mod, npost, att.reshape(bsz * seq, ATT_WIDTH), rec, sa, sr,
        w_att_o_b, w_rec_o_b, w_out_b, seq=seq, layer=l, casts=(ffn2_w_gu,))

    return _ffn(x2, mod, npre, npost, w_gu2_b, w_down2_b, sub=2, res_w=0.5, seq=seq, layer=l)[0]


def kernel(x, c, w_ada, b_ada, norm_pre, norm_post, ffn1_w_gu, ffn1_w_down, w_in, rel_bias, conv_w, conv_b, lru_wa, lru_ba, lru_wx, lru_bx, lru_lambda, w_att_o, w_rec_o, w_out, ffn2_w_gu, ffn2_w_down):
    bsz, seq, d = x.shape
    tiles = (TOKEN_TILE, MIX_TOKEN_TILE, FFN_TOKEN_TILE, Q_TILE * Q_STEP_TILES)
    assert d == D_MODEL and all(seq % n == 0 for n in tiles)
    assert TOKEN_TILE % LRU_TILE == 0
    x2 = x.reshape(bsz * seq, d)
    for l in range(w_ada.shape[0]):
        x2 = _layer(x2, c, l, bsz, seq, w_ada, b_ada, norm_pre, norm_post, ffn1_w_gu,
                    ffn1_w_down, w_in, rel_bias, conv_w, conv_b, lru_wa, lru_ba, lru_wx, lru_bx,
                    lru_lambda, w_att_o, w_rec_o, w_out, ffn2_w_gu, ffn2_w_down)
    return x2.reshape(bsz, seq, d)
```

```python
import functools
import math

import jax
import jax.numpy as jnp
from jax import lax
from jax.experimental import pallas as pl
from jax.experimental.pallas import tpu as pltpu

D_MODEL = 1024
CHUNK = 64
LEFT_CHUNKS = 8
ATT_HEADS = 8
ATT_HEAD_DIM = 64
ATT_WIDTH = ATT_HEADS * ATT_HEAD_DIM
MAX_REL = 128
LRU_WIDTH = D_MODEL
LRU_BLOCKS = 16
LRU_BLOCK = LRU_WIDTH // LRU_BLOCKS
CONV_WIDTH = 4
LRU_C = 8.0
D_FF = 2816
N_SUB = 3
EPS = 1e-6

LANES = 128
SUBLANES = 8
BF16_ROWS = 16
VMEM_LIMIT_BYTES = 56 * 1024 * 1024

TOKEN_TILE = 512
MIX_TOKEN_TILE = 1024
MIX_SUB_TILE = 512
FFN_TOKEN_TILE = 1024
FFN_SUB_TILE = 512
FF_TILE = 256
PROJ_CHUNK = 256
ADA_TILE = 1152
Q_TILE = 4 * CHUNK
Q_STEP_TILES = 4
K_BLOCKS = LEFT_CHUNKS * CHUNK // Q_TILE + 1
BAND = K_BLOCKS * Q_TILE
LRU_TILE = 256
LRU_STEPS = LRU_TILE // SUBLANES
MASK_VALUE = -1e30
SMALLEST_NORMAL = float(jnp.finfo(jnp.float32).tiny)
LOG2_E = math.log2(math.e)
BF16 = jnp.bfloat16
F32 = jnp.float32


def _dot(a, b):
    return jnp.dot(a, b, preferred_element_type=F32)


def _rms(x, g):
    return x * lax.rsqrt(jnp.mean(x * x, axis=-1, keepdims=True) + EPS) * g


def _modulated_norm(x, g, shift, scale):
    inv = lax.rsqrt(jnp.mean(x * x, axis=-1, keepdims=True) + EPS)
    return x * inv * (g * (1.0 + scale)) + shift


def _sigmoid(x):
    return 1.0 / (1.0 + jnp.exp(-x))


def _gelu_tanh(x):
    c = -2.0 * LOG2_E * math.sqrt(2.0 / math.pi)
    return x / (1.0 + jnp.exp2(x * (c + (0.044715 * c) * (x * x))))


def _mod_rows(mod_ref, sub):
    shift = mod_ref[3 * sub:3 * sub + 1, :]
    scale = mod_ref[3 * sub + 1:3 * sub + 2, :]
    gate = mod_ref[3 * sub + 2:3 * sub + 3, :]
    return shift, scale, gate


def _resident(shape):
    nd = len(shape)
    return pl.BlockSpec(shape, lambda *_: (0,) * nd, pipeline_mode=pl.Buffered(1))


def _params(*semantics):
    return pltpu.CompilerParams(dimension_semantics=semantics,
                                vmem_limit_bytes=VMEM_LIMIT_BYTES)


def _cast_streams(stacked, layer, n_steps):
    for w in stacked:
        assert w.shape[1] % (BF16_ROWS * n_steps) == 0 and w.shape[2] % LANES == 0, w.shape
    in_specs = [pl.BlockSpec((None, w.shape[1] // n_steps, w.shape[2]), lambda i: (layer, i, 0))
                for w in stacked]
    out_specs = [pl.BlockSpec((w.shape[1] // n_steps, w.shape[2]), lambda i: (i, 0))
                 for w in stacked]
    shapes = [jax.ShapeDtypeStruct(w.shape[1:], BF16) for w in stacked]
    return in_specs, out_specs, shapes


def _cast_slabs(in_refs, out_refs):
    for src, dst in zip(in_refs, out_refs):
        dst[...] = src[...].astype(BF16)


def _ada_kernel(c_ref, w_ref, b_ref, *rest):
    n_cast = (len(rest) - 1) // 2
    o_ref = rest[n_cast]
    _cast_slabs(rest[:n_cast], rest[n_cast + 1:])
    c = c_ref[...]
    c_act = (c * _sigmoid(c)).astype(BF16)
    o_ref[...] = _dot(c_act, w_ref[...].astype(BF16)) + b_ref[...]


def _ada(c, w_ada, b_ada, *, layer, casts=()):
    bsz, d = c.shape
    n = w_ada.shape[2]
    n_steps = n // ADA_TILE
    cast_in, cast_specs, cast_shapes = _cast_streams(casts, layer, n_steps)
    out, *cast_out = pl.pallas_call(
        _ada_kernel,
        grid=(n_steps,),
        in_specs=[pl.BlockSpec((bsz, d), lambda j: (0, 0)),
                  pl.BlockSpec((None, d, ADA_TILE), lambda j: (layer, 0, j)),
                  pl.BlockSpec((1, ADA_TILE), lambda j: (0, j))] + cast_in,
        out_specs=[pl.BlockSpec((bsz, ADA_TILE), lambda j: (0, j))] + cast_specs,
        out_shape=[jax.ShapeDtypeStruct((bsz, n), F32)] + cast_shapes,
        compiler_params=_params("arbitrary"),
        name="adaln",
    )(c, w_ada, b_ada[layer].reshape(1, n), *casts)
    return out, cast_out


def _ffn_kernel(x_ref, mod_ref, gpre_ref, gpost_ref, wgu_ref, wdn_ref, *rest, sub, res_w):
    n_cast = (len(rest) - 4) // 2
    o_ref = rest[n_cast]
    h_ref, act_ref, y_ref = rest[-3:]
    _cast_slabs(rest[:n_cast], rest[n_cast + 1:-3])
    shift, scale, gate = _mod_rows(mod_ref, sub)
    n_sub = x_ref.shape[0] // FFN_SUB_TILE

    def rows(s):
        return slice(s * FFN_SUB_TILE, (s + 1) * FFN_SUB_TILE)

    def prologue(s):
        x = x_ref[rows(s), :]
        h_ref[s % 2] = _modulated_norm(x, gpre_ref[sub:sub + 1, :], shift, scale).astype(BF16)

    def gate_up(s, j):
        lo = j * FF_TILE
        h = h_ref[s % 2]
        g = _dot(h, wgu_ref[:, lo:lo + FF_TILE])
        u = _dot(h, wgu_ref[:, D_FF + lo:D_FF + lo + FF_TILE])
        act_ref[s % 2, :, lo:lo + FF_TILE] = (g * _sigmoid(g) * u).astype(BF16)

    def down(s):
        y_ref[s % 2] = _dot(act_ref[s % 2], wdn_ref[...])

    def epilogue(s):
        y = _rms(y_ref[s % 2], gpost_ref[sub:sub + 1, :])
        o_ref[rows(s), :] = x_ref[rows(s), :] + (res_w * gate) * y

    n_ff = D_FF // FF_TILE
    prologue(0)
    for s in range(n_sub):
        for j in range(n_ff):
            gate_up(s, j)
            if j == 0 and s > 0:
                epilogue(s - 1)
            if j == n_ff // 2 and s + 1 < n_sub:
                prologue(s + 1)
        down(s)
    epilogue(n_sub - 1)


def _ffn(x2, mod, norm_pre, norm_post, w_gu, w_down, *, sub, res_w, seq, layer, casts=()):
    t, d = x2.shape
    tiles_per_seq = seq // FFN_TOKEN_TILE
    n_steps = t // FFN_TOKEN_TILE
    cast_in, cast_specs, cast_shapes = _cast_streams(casts, layer, n_steps)
    out, *cast_out = pl.pallas_call(
        functools.partial(_ffn_kernel, sub=sub, res_w=res_w),
        grid=(n_steps,),
        in_specs=[pl.BlockSpec((FFN_TOKEN_TILE, d), lambda i: (i, 0)),
                  pl.BlockSpec((None, 3 * N_SUB, d), lambda i: (i // tiles_per_seq, 0, 0)),
                  _resident(norm_pre.shape),
                  _resident(norm_post.shape),
                  _resident(w_gu.shape),
                  _resident(w_down.shape)] + cast_in,
        out_specs=[pl.BlockSpec((FFN_TOKEN_TILE, d), lambda i: (i, 0))] + cast_specs,
        out_shape=[jax.ShapeDtypeStruct((t, d), F32)] + cast_shapes,
        scratch_shapes=[pltpu.VMEM((2, FFN_SUB_TILE, d), BF16),
                        pltpu.VMEM((2, FFN_SUB_TILE, D_FF), BF16),
                        pltpu.VMEM((2, FFN_SUB_TILE, d), F32)],
        compiler_params=_params("arbitrary"),
        name=f"ffn{sub}",
    )(x2, mod, norm_pre, norm_post, w_gu, w_down, *casts)
    return out, cast_out


def _lru_unit(g, xr_ref, row0, gy_ref, o_ref, fresh, cw_ref, cb_ref, w2_ref, ba_ref, bx_ref, decay,
              tail_ref, h_ref, hs_ref):
    sub = lax.broadcasted_iota(jnp.int32, (SUBLANES, LANES), 0)
    taps = CONV_WIDTH - 1
    sl = slice(g * LANES, (g + 1) * LANES)
    x3 = xr_ref[g, row0:row0 + LRU_TILE, :].reshape(LRU_STEPS, SUBLANES, LANES)
    cur_tail = x3[LRU_STEPS - taps:]
    lead = pltpu.roll(jnp.where(sub == SUBLANES - 1, tail_ref[g] * fresh, cur_tail), 1, 1)
    tail_ref[g] = cur_tail
    xext = jnp.concatenate([lead, x3], axis=0)
    xc3 = 0.5 * cb_ref[:, sl] + (0.5 * cw_ref[0:1, sl]) * xext[0:LRU_STEPS]
    for w in range(1, CONV_WIDTH):
        xc3 = xc3 + (0.5 * cw_ref[w:w + 1, sl]) * xext[w:w + LRU_STEPS]
    xc = xc3.reshape(LRU_TILE, LANES)
    z = _dot(xc.astype(BF16), w2_ref[g])
    ta = jnp.tanh(z[:, :LANES] + 0.5 * ba_ref[:, sl])
    tx = jnp.tanh(z[:, LANES:] + 0.5 * bx_ref[:, sl])
    half_rate = decay[:, sl] * (ta + 1.0)
    a = jnp.exp2(half_rate * (-2.0 * LOG2_E))
    tp = jnp.tanh(half_rate)
    mult = (1.0 + a) * (tp * lax.rsqrt(jnp.maximum(tp, SMALLEST_NORMAL)))
    u = mult * ((tx + 1.0) * xc)
    a3 = a.reshape(LRU_STEPS, SUBLANES, LANES)
    u3 = u.reshape(LRU_STEPS, SUBLANES, LANES)
    loc = [u3[0]]
    prod = [a3[0]]
    for i in range(1, LRU_STEPS):
        loc.append(a3[i] * loc[-1] + u3[i])
        prod.append(a3[i] * prod[-1])
    blk_a, blk_u = prod[-1], loc[-1]
    for s in (1, 2, 4):
        keep = sub >= s
        a_prev = jnp.where(keep, pltpu.roll(blk_a, s, 0), 1.0)
        u_prev = jnp.where(keep, pltpu.roll(blk_u, s, 0), 0.0)
        blk_u = blk_u + blk_a * u_prev
        blk_a = blk_a * a_prev
    h_in = h_ref[g] * fresh
    end = blk_u + blk_a * h_in
    entry = jnp.where(sub == 0, h_in, pltpu.roll(end, 1, 0))
    h_ref[g] = jnp.broadcast_to(end[SUBLANES - 1:SUBLANES, :], (SUBLANES, LANES))
    for i in range(LRU_STEPS):
        hs_ref[g, i * SUBLANES:(i + 1) * SUBLANES, :] = loc[i] + prod[i] * entry
    rows = []
    for n in range(LRU_TILE // SUBLANES):
        j, i0 = divmod(n * SUBLANES, LRU_STEPS)
        rows.append(hs_ref[g, pl.ds(i0 * SUBLANES + j, SUBLANES, stride=SUBLANES), :])
    hs = jnp.concatenate(rows, axis=0)
    o_ref[row0:row0 + LRU_TILE, sl] = (hs * gy_ref[row0:row0 + LRU_TILE, sl]).astype(BF16)


def _proj_kernel(x_ref, mod_ref, gpre_ref, w_ref, cw_ref, cb_ref, w2_ref, ba_ref, bx_ref, lam_ref,
                 q_ref, k_ref, v_ref, rec_ref, sa_ref, sr_ref,
                 xr_s, gy_s, tail_ref, h_ref, hs_ref, *, tiles_per_seq, n_tiles):
    i = pl.program_id(0)
    rows = x_ref.shape[0]
    cw = PROJ_CHUNK
    per_chunk = cw // LANES
    lru0 = 3 * ATT_WIDTH
    n_att, n_lru = ATT_WIDTH // cw, LRU_WIDTH // cw

    @pl.when(i == 0)
    def _():
        for ref in (xr_s, gy_s, tail_ref, h_ref):
            ref[...] = jnp.zeros_like(ref)

    def scan_units():
        neg_lam = -lam_ref[...]
        softplus = jnp.maximum(neg_lam, 0.0) + jnp.log1p(jnp.exp(-jnp.abs(neg_lam)))
        decay = (0.25 * LRU_C) * softplus
        first = jnp.where(lax.rem(i - 1, tiles_per_seq) == 0, 0.0, 1.0)

        def unit(row0, g):
            _lru_unit(g, xr_s, row0, gy_s, rec_ref, first if row0 == 0 else 1.0, cw_ref, cb_ref,
                      w2_ref, ba_ref, bx_ref, decay, tail_ref, h_ref, hs_ref)

        return [functools.partial(unit, row0, g)
                for row0 in range(0, rows, LRU_TILE) for g in range(LRU_WIDTH // LANES)]

    @pl.when(i < n_tiles)
    def _():
        x = x_ref[...]
        shift, scale, _ = _mod_rows(mod_ref, 1)
        h = _modulated_norm(x, gpre_ref[1:2, :], shift, scale).astype(BF16)

        def proj(col):
            return _dot(h, w_ref[:, col:col + cw])

        def q_chunk(c):
            q = (proj(c * cw) * (ATT_HEAD_DIM ** -0.5 * LOG2_E)).astype(BF16)
            for p in range(per_chunk):
                q_ref[c * per_chunk + p] = q[:, p * LANES:(p + 1) * LANES]

        def k_chunk(c):
            k = proj(ATT_WIDTH + c * cw).astype(BF16)
            for p in range(per_chunk):
                k_ref[c * per_chunk + p] = k[:, p * LANES:(p + 1) * LANES]

        def v_chunk(c):
            heads = cw // ATT_HEAD_DIM
            v_t = proj(2 * ATT_WIDTH + c * cw).T.astype(BF16)
            v_ref[c * heads:(c + 1) * heads] = v_t.reshape(heads, ATT_HEAD_DIM, rows)

        def gate_chunk(ref, col0, c):
            ref[:, c * cw:(c + 1) * cw] = proj(col0 + c * cw).astype(BF16)

        def gy_chunk(c):
            gy_s[:, c * cw:(c + 1) * cw] = _gelu_tanh(proj(lru0 + LRU_WIDTH + c * cw))

        def xr_chunk(c):
            xr = proj(lru0 + c * cw)
            for src_row in range(0, rows, LRU_STEPS):
                tile0, j = src_row // LRU_TILE * LRU_TILE, src_row % LRU_TILE // LRU_STEPS
                for p in range(per_chunk):
                    xr_s[c * per_chunk + p, pl.ds(tile0 + j, LRU_STEPS, stride=SUBLANES), :] = (
                        xr[src_row:src_row + LRU_STEPS, p * LANES:(p + 1) * LANES])

        chunks = ([functools.partial(q_chunk, c) for c in range(n_att)]
                  + [functools.partial(k_chunk, c) for c in range(n_att)]
                  + [functools.partial(v_chunk, c) for c in range(n_att)]
                  + [functools.partial(gate_chunk, sa_ref, lru0 + 2 * LRU_WIDTH, c) for c in range(n_lru)]
                  + [functools.partial(gate_chunk, sr_ref, lru0 + 3 * LRU_WIDTH, c) for c in range(n_lru)]
                  + [functools.partial(gy_chunk, c) for c in range(n_lru)]
                  + [functools.partial(xr_chunk, c) for c in range(n_lru)])
        units = scan_units()
        for n, chunk in enumerate(chunks):
            chunk()
            if n < len(units):
                units[n]()

    @pl.when(i == n_tiles)
    def _():
        for unit in scan_units():
            unit()


def _proj(x2, mod, norm_pre, w_in, conv_w, conv_b, w2, ba, bx, lam, *, seq):
    t, d = x2.shape
    tiles_per_seq = seq // TOKEN_TILE
    n_tiles = t // TOKEN_TILE
    cur = lambda i: jnp.minimum(i, n_tiles - 1)
    row = lambda n: pl.BlockSpec((TOKEN_TILE, n), lambda i: (cur(i), 0))
    tok = lambda n, dt: jax.ShapeDtypeStruct((t, n), dt)
    bsz, pairs, groups = t // seq, ATT_WIDTH // LANES, LRU_WIDTH // LANES
    qk_spec = pl.BlockSpec((None, pairs, TOKEN_TILE, LANES),
                           lambda i: (cur(i) // tiles_per_seq, 0, cur(i) % tiles_per_seq, 0))
    qk_shape = jax.ShapeDtypeStruct((bsz, pairs, seq, LANES), BF16)
    v_spec = pl.BlockSpec((None, ATT_HEADS, ATT_HEAD_DIM, TOKEN_TILE),
                          lambda i: (cur(i) // tiles_per_seq, 0, 0, cur(i) % tiles_per_seq))
    v_shape = jax.ShapeDtypeStruct((bsz, ATT_HEADS, ATT_HEAD_DIM, seq), BF16)
    rec_spec = pl.BlockSpec((TOKEN_TILE, LRU_WIDTH), lambda i: (jnp.maximum(i - 1, 0), 0))
    return pl.pallas_call(
        functools.partial(_proj_kernel, tiles_per_seq=tiles_per_seq, n_tiles=n_tiles),
        grid=(n_tiles + 1,),
        in_specs=[row(d),
                  pl.BlockSpec((None, 3 * N_SUB, d), lambda i: (cur(i) // tiles_per_seq, 0, 0)),
                  _resident(norm_pre.shape), _resident(w_in.shape),
                  _resident(conv_w.shape), _resident(conv_b.shape), _resident(w2.shape),
                  _resident(ba.shape), _resident(bx.shape), _resident(lam.shape)],
        out_specs=[qk_spec, qk_spec, v_spec, rec_spec, row(D_MODEL), row(D_MODEL)],
        out_shape=[qk_shape, qk_shape, v_shape, tok(LRU_WIDTH, BF16),
                   tok(D_MODEL, BF16), tok(D_MODEL, BF16)],
        scratch_shapes=[pltpu.VMEM((groups, TOKEN_TILE, LANES), F32),
                        pltpu.VMEM((TOKEN_TILE, LRU_WIDTH), F32),
                        pltpu.VMEM((groups, CONV_WIDTH - 1, SUBLANES, LANES), F32),
                        pltpu.VMEM((groups, SUBLANES, LANES), F32),
                        pltpu.VMEM((groups, LRU_TILE, LANES), F32)],
        compiler_params=_params("arbitrary"),
        name="mixer_proj",
    )(x2, mod, norm_pre, w_in, conv_w, conv_b, w2, ba, bx, lam)


REL_PAD = 384
TOEPLITZ = 1024


def _bias_kernel(tab_ref, o_ref):
    tab = tab_ref[...]
    hi = tab.astype(BF16)
    r1 = tab - hi.astype(F32)
    mid = r1.astype(BF16)
    lo = (r1 - mid.astype(F32)).astype(BF16)
    d_idx = lax.broadcasted_iota(jnp.int32, (REL_PAD, TOEPLITZ), 0)
    m_idx = lax.broadcasted_iota(jnp.int32, (REL_PAD, TOEPLITZ), 1)
    rel = jnp.clip(m_idx - (Q_TILE - 1), -MAX_REL, MAX_REL) + MAX_REL
    onehot = jnp.where(d_idx == rel, 1.0, 0.0).astype(BF16)
    profile = ((_dot(lo, onehot) + _dot(mid, onehot)) + _dot(hi, onehot)) * LOG2_E

    kk = lax.broadcasted_iota(jnp.int32, (BAND, Q_TILE), 0)
    r = lax.broadcasted_iota(jnp.int32, (BAND, Q_TILE), 1)
    qc = r // CHUNK
    kc = kk // CHUNK
    band = jnp.where((kc >= qc) & (kc <= qc + LEFT_CHUNKS), 0.0, MASK_VALUE)
    for h in range(ATT_HEADS):
        rows = jnp.broadcast_to(profile[h:h + 1, :], (BAND, TOEPLITZ))
        skew = pltpu.roll(rows, TOEPLITZ - (BAND - 1), 1, stride=1, stride_axis=0)
        o_ref[h] = skew[:, :Q_TILE] + band


def _attn_kernel(q_ref, *refs):
    n_kv = K_BLOCKS - 1 + Q_STEP_TILES
    k_refs, v_refs = refs[:n_kv], refs[n_kv:2 * n_kv]
    tab_ref, o_ref, acc_ref, bias_ref = refs[2 * n_kv:2 * n_kv + 4]
    s_refs, p_refs = refs[-4:-2], refs[-2:]
    first_tile = pl.program_id(1) * Q_STEP_TILES
    lane_head = lax.broadcasted_iota(jnp.int32, (1, LANES), 1) // ATT_HEAD_DIM
    zero = jnp.zeros((), BF16)
    live = [(2 * c * CHUNK, (2 * c + 2 + LEFT_CHUNKS) * CHUNK) for c in range(Q_TILE // LANES)]

    @pl.when((pl.program_id(0) == 0) & (pl.program_id(1) == 0))
    def _():
        for p_ref in p_refs:
            p_ref[...] = jnp.zeros_like(p_ref)
        _bias_kernel(tab_ref, bias_ref)

    def start_pens(u):
        return [jnp.where(first_tile + u + j < K_BLOCKS - 1, MASK_VALUE, 0.0)
                for j in range(K_BLOCKS - 1)] + [0.0]

    def scores(u, h, slot):
        pair = h // 2
        qh = jnp.where(lane_head == h % 2, q_ref[pair, u * Q_TILE:(u + 1) * Q_TILE, :], zero)
        kcat = jnp.concatenate([r[pair] for r in k_refs[u:u + K_BLOCKS]], axis=0)
        s = lax.dot_general(kcat, qh, (((1,), (1,)), ((), ())), preferred_element_type=F32)
        s_refs[slot][...] = s + bias_ref[h]

    def col_reduce(parts, op):
        while len(parts) > 1:
            parts = [op(parts[i], parts[i + 1]) if i + 1 < len(parts) else parts[i]
                     for i in range(0, len(parts), 2)]
        return parts[0]

    def softmax(u, slot):
        s_ref, p_ref = s_refs[slot], p_refs[slot]
        pens = start_pens(u)
        groups = CHUNK // SUBLANES
        inv = []
        for c, (r0, r1) in enumerate(live):
            cl = slice(c * LANES, (c + 1) * LANES)
            blocks = range(r0, r1, CHUNK)
            part = [s_ref[r:r + CHUNK, cl].reshape(groups, SUBLANES, LANES).max(axis=0)
                    + pens[r // Q_TILE] for r in blocks]
            m = jnp.max(col_reduce(part, jnp.maximum), axis=0, keepdims=True)
            shifted = [m - pen for pen in pens]
            sums = []
            for r in blocks:
                pr = jnp.exp2(s_ref[r:r + CHUNK, cl] - shifted[r // Q_TILE])
                sums.append(pr.reshape(groups, SUBLANES, LANES).sum(axis=0))
                p_ref[r:r + CHUNK, cl] = pr.astype(BF16)
            l = jnp.sum(col_reduce(sums, jnp.add), axis=0, keepdims=True)
            inv.append(1.0 / l)
        return jnp.concatenate(inv, axis=1)

    def weighted_values(u, h, slot, inv_l):
        v_t = jnp.concatenate([r[h] for r in v_refs[u:u + K_BLOCKS]], axis=1)
        acc_ref[u, h] = _dot(v_t, p_refs[slot][...]) * inv_l

    items = [(u, h) for u in range(Q_STEP_TILES) for h in range(ATT_HEADS)]
    scores(*items[0], 0)
    inv_prev = None
    for n, (u, h) in enumerate(items):
        if n + 1 < len(items):
            scores(*items[n + 1], (n + 1) % 2)
        inv_l = softmax(u, n % 2)
        if n > 0:
            weighted_values(*items[n - 1], (n - 1) % 2, inv_prev)
        inv_prev = inv_l
    weighted_values(*items[-1], (len(items) - 1) % 2, inv_prev)
    for u in range(Q_STEP_TILES):
        for p in range(ATT_HEADS // 2):
            pair_t = acc_ref[u, 2 * p:2 * p + 2].reshape(LANES, Q_TILE)
            o_ref[u * Q_TILE:(u + 1) * Q_TILE, p * LANES:(p + 1) * LANES] = pair_t.T.astype(BF16)


def _attention(q, k, v_t, rel_bias):
    tab = jnp.pad(rel_bias, ((0, 0), (0, REL_PAD - rel_bias.shape[1])))
    bsz, pairs, seq, _ = q.shape
    step_rows = Q_STEP_TILES * Q_TILE
    rel = range(-(K_BLOCKS - 1), Q_STEP_TILES)
    blk = lambda t, n: jnp.maximum(t * Q_STEP_TILES + n, 0)
    key = lambda n: pl.BlockSpec((None, pairs, Q_TILE, LANES), lambda b, t: (b, 0, blk(t, n), 0))
    val = lambda n: pl.BlockSpec((None, ATT_HEADS, ATT_HEAD_DIM, Q_TILE),
                                 lambda b, t: (b, 0, 0, blk(t, n)))
    slot = lambda dt: pltpu.VMEM((BAND, Q_TILE), dt)
    return pl.pallas_call(
        _attn_kernel,
        grid=(bsz, seq // step_rows),
        in_specs=[pl.BlockSpec((None, pairs, step_rows, LANES), lambda b, t: (b, 0, t, 0))]
                 + [key(n) for n in rel] + [val(n) for n in rel] + [_resident(tab.shape)],
        out_specs=pl.BlockSpec((None, step_rows, ATT_WIDTH), lambda b, t: (b, t, 0)),
        out_shape=jax.ShapeDtypeStruct((bsz, seq, ATT_WIDTH), BF16),
        scratch_shapes=[pltpu.VMEM((Q_STEP_TILES, ATT_HEADS, ATT_HEAD_DIM, Q_TILE), F32),
                        pltpu.VMEM((ATT_HEADS, BAND, Q_TILE), F32),
                        slot(F32), slot(F32), slot(BF16), slot(BF16)],
        compiler_params=_params("arbitrary", "arbitrary"),
        name="chunk_attn",
    )(q, *([k] * len(rel)), *([v_t] * len(rel)), tab)


def _pair_block_diag(wa, wx):
    def pairs(w):
        z = jnp.zeros_like(w[0::2])
        top = jnp.concatenate([w[0::2], z], axis=2)
        bot = jnp.concatenate([z, w[1::2]], axis=2)
        return jnp.concatenate([top, bot], axis=1)
    return jnp.concatenate([pairs(wa), pairs(wx)], axis=2)


def _mixout_kernel(x_ref, mod_ref, gpost_ref, att_ref, rec_ref, sa_ref, sr_ref,
                   wao_ref, wro_ref, wout_ref, *rest):
    n_cast = (len(rest) - 2) // 2
    o_ref, y_ref = rest[n_cast], rest[-1]
    _cast_slabs(rest[:n_cast], rest[n_cast + 1:-1])
    _, _, gate = _mod_rows(mod_ref, 1)
    n_sub = x_ref.shape[0] // MIX_SUB_TILE

    def rows(s):
        return slice(s * MIX_SUB_TILE, (s + 1) * MIX_SUB_TILE)

    def project(s):
        att = _dot(att_ref[rows(s), :], wao_ref[...])
        rec = _dot(rec_ref[rows(s), :], wro_ref[...])
        gate_att = _sigmoid(sa_ref[rows(s), :].astype(F32))
        gate_rec = _sigmoid(sr_ref[rows(s), :].astype(F32))
        merged = gate_att * att + gate_rec * rec
        y_ref[s % 2] = _dot(merged.astype(BF16), wout_ref[...])

    def epilogue(s):
        o_ref[rows(s), :] = x_ref[rows(s), :] + gate * _rms(y_ref[s % 2], gpost_ref[1:2, :])

    for s in range(n_sub):
        project(s)
        if s > 0:
            epilogue(s - 1)
    epilogue(n_sub - 1)


def _mixout(x2, mod, norm_post, att, rec, sa, sr, w_att_o, w_rec_o, w_out, *, seq, layer,
            casts=()):
    t, d = x2.shape
    tiles_per_seq = seq // MIX_TOKEN_TILE
    n_steps = t // MIX_TOKEN_TILE
    row = lambda n: pl.BlockSpec((MIX_TOKEN_TILE, n), lambda i: (i, 0))
    cast_in, cast_specs, cast_shapes = _cast_streams(casts, layer, n_steps)
    out, *cast_out = pl.pallas_call(
        _mixout_kernel,
        grid=(n_steps,),
        in_specs=[row(d),
                  pl.BlockSpec((None, 3 * N_SUB, d), lambda i: (i // tiles_per_seq, 0, 0)),
                  _resident(norm_post.shape),
                  row(ATT_WIDTH), row(LRU_WIDTH), row(d), row(d),
                  _resident(w_att_o.shape), _resident(w_rec_o.shape), _resident(w_out.shape)]
                 + cast_in,
        out_specs=[row(d)] + cast_specs,
        out_shape=[jax.ShapeDtypeStruct((t, d), F32)] + cast_shapes,
        scratch_shapes=[pltpu.VMEM((2, MIX_SUB_TILE, d), F32)],
        compiler_params=_params("arbitrary"),
        name="mixer_out",
    )(x2, mod, norm_post, att, rec, sa, sr, w_att_o, w_rec_o, w_out, *casts)
    return out, cast_out


def _layer(x2, c, l, bsz, seq, w_ada, b_ada, norm_pre, norm_post, ffn1_w_gu, ffn1_w_down, w_in,
           rel_bias, conv_w, conv_b, lru_wa, lru_ba, lru_wx, lru_bx, lru_lambda, w_att_o,
           w_rec_o, w_out, ffn2_w_gu, ffn2_w_down):
    mod, (w_gu1_b, w_down1_b) = _ada(c, w_ada, b_ada, layer=l, casts=(ffn1_w_gu, ffn1_w_down))
    mod = mod.reshape(bsz, 3 * N_SUB, D_MODEL)
    npre, npost = norm_pre[l], norm_post[l]

    x2, (w_in_b, w_att_o_b, w_rec_o_b, w_out_b, w_down2_b) = _ffn(
        x2, mod, npre, npost, w_gu1_b, w_down1_b, sub=0, res_w=0.5, seq=seq, layer=l,
        casts=(w_in, w_att_o, w_rec_o, w_out, ffn2_w_down))

    w2 = _pair_block_diag(lru_wa[l], lru_wx[l]).astype(BF16)
    vec = lambda p: p[l].reshape(1, LRU_WIDTH)
    q, k, v, rec, sa, sr = _proj(x2, mod, npre, w_in_b, conv_w[l], vec(conv_b), w2,
                                 vec(lru_ba), vec(lru_bx), vec(lru_lambda), seq=seq)
    att = _attention(q, k, v, rel_bias[l])
    x2, (w_gu2_b,) = _mixout(
        x2, mod, npost, att.reshape(bsz * seq, ATT_WIDTH), rec, sa, sr,
        w_att_o_b, w_rec_o_b, w_out_b, seq=seq, layer=l, casts=(ffn2_w_gu,))

    return _ffn(x2, mod, npre, npost, w_gu2_b, w_down2_b, sub=2, res_w=0.5, seq=seq, layer=l)[0]


def kernel(x, c, w_ada, b_ada, norm_pre, norm_post, ffn1_w_gu, ffn1_w_down, w_in, rel_bias, conv_w, conv_b, lru_wa, lru_ba, lru_wx, lru_bx, lru_lambda, w_att_o, w_rec_o, w_out, ffn2_w_gu, ffn2_w_down):
    bsz, seq, d = x.shape
    tiles = (TOKEN_TILE, MIX_TOKEN_TILE, FFN_TOKEN_TILE, Q_TILE * Q_STEP_TILES)
    assert d == D_MODEL and all(seq % n == 0 for n in tiles)
    assert TOKEN_TILE % LRU_TILE == 0
    x2 = x.reshape(bsz * seq, d)
    for l in range(w_ada.shape[0]):
        x2 = _layer(x2, c, l, bsz, seq, w_ada, b_ada, norm_pre, norm_post, ffn1_w_gu,
                    ffn1_w_down, w_in, rel_bias, conv_w, conv_b, lru_wa, lru_ba, lru_wx, lru_bx,
                    lru_lambda, w_att_o, w_rec_o, w_out, ffn2_w_gu, ffn2_w_down)
    return x2.reshape(bsz, seq, d)
```
